```python
import math
import jax
import jax.numpy as jnp
from jax import lax
import numpy as np

D_MODEL = 2048
BATCH = 4
SEQ = 4096
DEPTH = 4

D_MIX = D_MODEL
S5_WIDTH = D_MIX // 4
S5_GROUP = 16
S5_GROUPS = S5_WIDTH // S5_GROUP
S5_STATE = 64
MLA_HEADS = 8
MLA_NOPE = 128
MLA_ROPE = 64
MLA_V = 128
MLA_Q_RANK = D_MODEL // 4
MLA_KV_RANK = D_MODEL // 8
MLA_WIDTH = MLA_HEADS * MLA_V
ROPE_THETA = 10000.0
Q_BLOCK = 128
MASK_VALUE = -1e30
HG_HEADS = 4
HG_DK = 128
HG_DV = (D_MIX - S5_WIDTH - MLA_WIDTH) // HG_HEADS
HG_WIDTH = HG_HEADS * HG_DV
HG_CHUNK = 64
D_FF = 5504
CONV_W = 3
EPS = 1e-6
IN_SIZES = (S5_WIDTH, MLA_Q_RANK, MLA_KV_RANK, MLA_ROPE,
            HG_HEADS * HG_DK, HG_HEADS * HG_DK, HG_WIDTH, HG_WIDTH)
IN_OFFSETS = tuple(int(v) for v in np.cumsum(IN_SIZES)[:-1])
D_IN = int(sum(IN_SIZES))

kernel_name = 'hybrid_s5_mla_hgrn2_block'


def rmsnorm(x, gain):
    xf = x.astype(jnp.float32)
    xf = xf * lax.rsqrt(jnp.mean(xf * xf, axis=-1, keepdims=True) + EPS)
    return (xf * gain.astype(jnp.float32)).astype(x.dtype)


def rope_tables(positions):
    inv_freq = 1.0 / (ROPE_THETA ** (jnp.arange(0, MLA_ROPE, 2, dtype=jnp.float32) / MLA_ROPE))
    ang = positions.astype(jnp.float32)[..., None] * inv_freq
    return jnp.cos(ang), jnp.sin(ang)


def apply_rope(x, cos, sin):
    half = x.shape[-1] // 2
    x1, x2 = x[..., :half], x[..., half:]
    cos = cos.astype(x.dtype)
    sin = sin.astype(x.dtype)
    return jnp.concatenate([x1 * cos - x2 * sin, x2 * cos + x1 * sin], axis=-1)


def s5_mixer(u, lam_re, lam_im, log_dt, b_re, b_im, c_re, c_im, d, w_glu):
    bsz, seq, _ = u.shape
    uf = u.astype(jnp.float32).reshape(bsz, seq, S5_GROUPS, S5_GROUP)
    lam = lax.complex(lam_re.astype(jnp.float32), lam_im.astype(jnp.float32))
    dt = jnp.exp(log_dt.astype(jnp.float32))[:, None]
    lam_bar = jnp.exp(lam * dt)
    b = lax.complex(b_re.astype(jnp.float32), b_im.astype(jnp.float32))
    b_bar = ((lam_bar - 1.0) / lam)[..., None] * b
    bu = jnp.einsum('gpc,bsgc->bsgp', b_bar, uf.astype(jnp.complex64))
    a = jnp.broadcast_to(lam_bar, bu.shape)

    def combine(left, right):
        a_l, b_l = left
        a_r, b_r = right
        return a_r * a_l, a_r * b_l + b_r

    _, h = lax.associative_scan(combine, (a, bu), axis=1)
    cm = lax.complex(c_re.astype(jnp.float32), c_im.astype(jnp.float32))
    y = (jnp.einsum('gcp,bsgp->bsgc', cm, h).real
         + d.astype(jnp.float32).reshape(S5_GROUPS, S5_GROUP) * uf)
    y = jax.nn.gelu(y.reshape(bsz, seq, S5_WIDTH))
    out = y * jax.nn.sigmoid(y @ w_glu.astype(jnp.float32))
    return out.astype(u.dtype)


def causal_attention_blocks(q_nope, q_rope, k_nope, k_rope, v):
    seq = q_nope.shape[1]
    scale = (MLA_NOPE + MLA_ROPE) ** -0.5
    outs = []
    for blk in range(seq // Q_BLOCK):
        q0, q1 = blk * Q_BLOCK, (blk + 1) * Q_BLOCK
        s = (jnp.einsum('bqhd,bkhd->bhqk', q_nope[:, q0:q1], k_nope[:, :q1])
             + jnp.einsum('bqhr,bkr->bhqk', q_rope[:, q0:q1], k_rope[:, :q1])).astype(jnp.float32) * scale
        causal = jnp.arange(q1)[None, :] <= jnp.arange(q0, q1)[:, None]
        p = jax.nn.softmax(jnp.where(causal, s, MASK_VALUE), axis=-1).astype(v.dtype)
        outs.append(jnp.einsum('bhqk,bkhd->bqhd', p, v[:, :q1]))
    return jnp.concatenate(outs, axis=1)


def mla_mixer(c_q, c_kv, k_rope_in, q_norm, w_uq, kv_norm, w_ukv, cos, sin):
    bsz, seq, _ = c_q.shape
    q = (rmsnorm(c_q, q_norm) @ w_uq).reshape(bsz, seq, MLA_HEADS, MLA_NOPE + MLA_ROPE)
    q_nope = q[..., :MLA_NOPE]
    q_rope = apply_rope(q[..., MLA_NOPE:], cos[:, :, None, :], sin[:, :, None, :])
    kv = (rmsnorm(c_kv, kv_norm) @ w_ukv).reshape(bsz, seq, MLA_HEADS, MLA_NOPE + MLA_V)
    k_nope, v = kv[..., :MLA_NOPE], kv[..., MLA_NOPE:]
    k_rope = apply_rope(k_rope_in, cos, sin)
    o = causal_attention_blocks(q_nope, q_rope, k_nope, k_rope, v)
    return o.reshape(bsz, seq, MLA_WIDTH)


def hgrn2_chunkwise(q, k, v, logf):
    bsz, seq, nh, dk = q.shape
    dv = v.shape[-1]
    n_chunks = seq // HG_CHUNK

    def to_chunks(t):
        return t.reshape(bsz, n_chunks, HG_CHUNK, nh, t.shape[-1]).transpose(1, 0, 3, 2, 4)

    mask = jnp.tril(jnp.ones((HG_CHUNK, HG_CHUNK), dtype=bool))[:, :, None]

    def step(state, inp):
        qc, kc, vc, gc = inp
        b = jnp.cumsum(gc, axis=2)
        o_inter = jnp.einsum('bhtk,bhkv->bhtv', qc * jnp.exp(b), state)
        diff = b[:, :, :, None, :] - b[:, :, None, :, :]
        decay = jnp.where(mask, jnp.exp(jnp.where(mask, diff, 0.0)), 0.0)
        attn = jnp.einsum('bhtk,bhsk,bhtsk->bhts', qc, kc, decay)
        o_intra = jnp.einsum('bhts,bhsv->bhtv', attn, vc)
        b_last = b[:, :, -1:, :]
        new_state = (jnp.exp(b_last[:, :, 0, :])[..., None] * state
                     + jnp.einsum('bhsk,bhsv->bhkv', kc * jnp.exp(b_last - b), vc))
        return new_state, o_inter + o_intra

    state0 = jnp.zeros((bsz, nh, dk, dv), jnp.float32)
    _, o = lax.scan(step, state0, (to_chunks(q), to_chunks(k), to_chunks(v), to_chunks(logf)))
    return o.transpose(1, 0, 3, 2, 4).reshape(bsz, seq, nh, dv)


def hgrn2_mixer(q_in, f_in, i_in, g_in, lb, out_norm):
    bsz, seq, _ = q_in.shape
    z = f_in.astype(jnp.float32)
    lb = lb.astype(jnp.float32)
    logf = jnp.log(lb + (1.0 - lb) * jax.nn.sigmoid(z))
    k = (1.0 - lb) * jax.nn.sigmoid(-z)
    q = jax.nn.silu(q_in.astype(jnp.float32))

    def heads(t):
        return t.reshape(bsz, seq, HG_HEADS, t.shape[-1] // HG_HEADS)

    o = hgrn2_chunkwise(heads(q), heads(k), heads(i_in.astype(jnp.float32)), heads(logf))
    o = rmsnorm(o, out_norm) * jax.nn.silu(heads(g_in.astype(jnp.float32)))
    return o.reshape(bsz, seq, HG_WIDTH).astype(q_in.dtype)


def causal_dwconv(u, w, b):
    seq = u.shape[1]
    taps = w.shape[0]
    up = jnp.pad(u, ((0, 0), (taps - 1, 0), (0, 0)))
    out = b
    for j in range(taps):
        out = out + up[:, j:j + seq] * w[j]
    return out


def conv_geglu_ffn(h, w_up, conv_w, conv_b, w_down):
    u = causal_dwconv(h @ w_up, conv_w, conv_b)
    gate, val = jnp.split(u, 2, axis=-1)
    return (jax.nn.gelu(gate, approximate=True) * val) @ w_down


def setup_inputs(seed: int = 0) -> dict:
    key = jax.random.key(seed)
    k = jax.random.split(key, 32)
    L = DEPTH

    def nrm(i, shape, scale=1.0):
        return scale * jax.random.normal(k[i], shape, jnp.float32)

    def gain(i, shape):
        return 1.0 + nrm(i, shape, 0.1)

    x = nrm(0, (BATCH, SEQ, D_MODEL))
    c = nrm(1, (BATCH, D_MODEL))
    offsets = jax.random.randint(k[2], (BATCH, 1), 0, 1024, dtype=jnp.int32)
    positions = offsets + jnp.arange(SEQ, dtype=jnp.int32)[None, :]
    w_in = nrm(3, (L, D_MODEL, D_IN), D_MODEL ** -0.5)
    s5_lambda_re = -0.5 + nrm(4, (L, S5_GROUPS, S5_STATE), 0.01)
    s5_lambda_im = math.pi * jnp.arange(S5_STATE, dtype=jnp.float32) + nrm(5, (L, S5_GROUPS, S5_STATE), 0.01)
    s5_log_dt = jax.random.uniform(k[6], (L, S5_GROUPS), jnp.float32, math.log(1e-3), math.log(1e-1))
    s5_b_re = nrm(7, (L, S5_GROUPS, S5_STATE, S5_GROUP), (2 * S5_GROUP) ** -0.5)
    s5_b_im = nrm(8, (L, S5_GROUPS, S5_STATE, S5_GROUP), (2 * S5_GROUP) ** -0.5)
    s5_c_re = nrm(9, (L, S5_GROUPS, S5_GROUP, S5_STATE), (2 * S5_STATE) ** -0.5)
    s5_c_im = nrm(10, (L, S5_GROUPS, S5_GROUP, S5_STATE), (2 * S5_STATE) ** -0.5)
    s5_d = nrm(11, (L, S5_WIDTH))
    s5_w_glu = nrm(12, (L, S5_WIDTH, S5_WIDTH), S5_WIDTH ** -0.5)
    mla_q_norm = gain(13, (L, MLA_Q_RANK))
    mla_w_uq = nrm(14, (L, MLA_Q_RANK, MLA_HEADS * (MLA_NOPE + MLA_ROPE)), MLA_Q_RANK ** -0.5)
    mla_kv_norm = gain(15, (L, MLA_KV_RANK))
    mla_w_ukv = nrm(16, (L, MLA_KV_RANK, MLA_HEADS * (MLA_NOPE + MLA_V)), MLA_KV_RANK ** -0.5)
    hg_lb_logits = nrm(17, (L, HG_HEADS * HG_DK), 0.5)
    hg_out_norm = gain(18, (L, HG_DV))
    w_out = nrm(19, (L, D_MIX, D_MODEL), D_MIX ** -0.5)
    mix_pre_norm = gain(20, (L, D_MODEL))
    mix_post_norm = gain(21, (L, D_MODEL))
    ffn_pre_norm = gain(22, (L, D_MODEL))
    ffn_post_norm = gain(23, (L, D_MODEL))
    ffn_w_up = nrm(24, (L, D_MODEL, 2 * D_FF), D_MODEL ** -0.5)
    ffn_conv_w = nrm(25, (L, CONV_W, 2 * D_FF), CONV_W ** -0.5)
    ffn_conv_b = nrm(26, (L, 2 * D_FF), 0.02)
    ffn_w_down = nrm(27, (L, D_FF, D_MODEL), D_FF ** -0.5)
    w_ada = nrm(28, (L, D_MODEL, 6 * D_MODEL), 0.5 * D_MODEL ** -0.5)
    b_ada = nrm(29, (L, 6 * D_MODEL), 0.02)
    return {'x': x, 'c': c, 'positions': positions, 'w_in': w_in,
            's5_lambda_re': s5_lambda_re, 's5_lambda_im': s5_lambda_im, 's5_log_dt': s5_log_dt,
            's5_b_re': s5_b_re, 's5_b_im': s5_b_im, 's5_c_re': s5_c_re, 's5_c_im': s5_c_im,
            's5_d': s5_d, 's5_w_glu': s5_w_glu,
            'mla_q_norm': mla_q_norm, 'mla_w_uq': mla_w_uq, 'mla_kv_norm': mla_kv_norm, 'mla_w_ukv': mla_w_ukv,
            'hg_lb_logits': hg_lb_logits, 'hg_out_norm': hg_out_norm, 'w_out': w_out,
            'mix_pre_norm': mix_pre_norm, 'mix_post_norm': mix_post_norm,
            'ffn_pre_norm': ffn_pre_norm, 'ffn_post_norm': ffn_post_norm,
            'ffn_w_up': ffn_w_up, 'ffn_conv_w': ffn_conv_w, 'ffn_conv_b': ffn_conv_b, 'ffn_w_down': ffn_w_down,
            'w_ada': w_ada, 'b_ada': b_ada}


def reference(x, c, positions, w_in, s5_lambda_re, s5_lambda_im, s5_log_dt, s5_b_re, s5_b_im,
              s5_c_re, s5_c_im, s5_d, s5_w_glu, mla_q_norm, mla_w_uq, mla_kv_norm, mla_w_ukv,
              hg_lb_logits, hg_out_norm, w_out, mix_pre_norm, mix_post_norm, ffn_pre_norm, ffn_post_norm,
              ffn_w_up, ffn_conv_w, ffn_conv_b, ffn_w_down, w_ada, b_ada):
    cos, sin = rope_tables(positions)
    probs = jax.nn.softmax(hg_lb_logits.astype(jnp.float32), axis=0)
    lower_bounds = jnp.cumsum(probs, axis=0) - probs[0:1]
    c_act = jax.nn.silu(c)
    for l in range(DEPTH):
        mod = c_act @ w_ada[l] + b_ada[l]
        sh1, sc1, g1, sh2, sc2, g2 = jnp.split(mod[:, None, :], 6, axis=-1)
        h = rmsnorm(x, mix_pre_norm[l]) * (1.0 + sc1) + sh1
        proj = h @ w_in[l]
        u_s5, c_q, c_kv, k_rope, hq, hf, hi, hg = jnp.split(proj, IN_OFFSETS, axis=-1)
        y_s5 = s5_mixer(u_s5, s5_lambda_re[l], s5_lambda_im[l], s5_log_dt[l], s5_b_re[l], s5_b_im[l],
                        s5_c_re[l], s5_c_im[l], s5_d[l], s5_w_glu[l])
        y_mla = mla_mixer(c_q, c_kv, k_rope, mla_q_norm[l], mla_w_uq[l], mla_kv_norm[l], mla_w_ukv[l], cos, sin)
        y_hg = hgrn2_mixer(hq, hf, hi, hg, lower_bounds[l], hg_out_norm[l])
        mixed = jnp.concatenate([y_s5, y_mla, y_hg], axis=-1) @ w_out[l]
        x = x + g1 * rmsnorm(mixed, mix_post_norm[l])
        h = rmsnorm(x, ffn_pre_norm[l]) * (1.0 + sc2) + sh2
        y = conv_geglu_ffn(h, ffn_w_up[l], ffn_conv_w[l], ffn_conv_b[l], ffn_w_down[l])
        x = x + g2 * rmsnorm(y, ffn_post_norm[l])
    return x
```

```python
import functools
import math

import jax
import jax.numpy as jnp
from jax import lax
from jax.experimental import pallas as pl
from jax.experimental.pallas import tpu as pltpu

F32 = jnp.float32
BF16 = jnp.bfloat16

D_MODEL = 2048
DEPTH = 4
S5_WIDTH = 512
S5_GROUP = 16
S5_GROUPS = 32
S5_STATE = 64
MLA_HEADS = 8
MLA_NOPE = 128
MLA_ROPE = 64
MLA_V = 128
MLA_Q_RANK = 512
MLA_KV_RANK = 256
MLA_WIDTH = MLA_HEADS * MLA_V
ROPE_THETA = 10000.0
MASK_VALUE = -1e30
HG_HEADS = 4
HG_DK = 128
HG_DV = 128
HG_WIDTH = HG_HEADS * HG_DV
D_FF = 5504
CONV_W = 3
EPS = 1e-6

LANES = 128
SUBLANES = 8
MXU_DIM = 256
VMEM_LIMIT_BYTES = 56 * 1024 * 1024

COL_BLK = 512
N_COL_BLKS = 7
D_IN_PAD = N_COL_BLKS * COL_BLK
MLA_QK = 2 * LANES
D_FF_PAD = 5632

ADA_TN = 1024
PROJ_TM = 512
PROJ_TN = D_IN_PAD // 2
S5_LS = 16
S5_HALF = S5_WIDTH // 2
S5_HSTATE = (S5_GROUPS // 2) * S5_STATE
PREP_TM = 512
FLASH_T = 512
HG_CHUNK = 128
HG_LEVELS = (16, 32, 64, 128)
OUT_TM = 512
FFN_TM = 512
FFN_TF = 512
HALO = SUBLANES


def _sigmoid(x):
    return 1.0 / (1.0 + jnp.exp(-x))


def _silu(x):
    return x * _sigmoid(x)


def _gelu_tanh(x):
    c = math.sqrt(2.0 / math.pi)
    return x * (0.5 * (1.0 + jnp.tanh(c * (x + 0.044715 * (x * x * x)))))


def _rms(x, gain):
    return x * lax.rsqrt(jnp.mean(x * x, axis=-1, keepdims=True) + EPS) * gain


def _dot(a, b):
    return jnp.dot(a, b, preferred_element_type=F32)


def _dot_nt(a, b):
    return lax.dot_general(a, b, (((1,), (1,)), ((), ())), preferred_element_type=F32)


def _dot_tn(a, b):
    return lax.dot_general(a, b, (((0,), (0,)), ((), ())), preferred_element_type=F32)


def _params(*sem):
    return pltpu.CompilerParams(dimension_semantics=sem, vmem_limit_bytes=VMEM_LIMIT_BYTES)


def _adaln_kernel(c_ref, w_ref, b_ref, o_ref):
    ca = _silu(c_ref[...]).astype(BF16)
    o_ref[0] = _dot(ca, w_ref[0].astype(BF16)) + b_ref[0]


def _adaln(c, w_ada, b_ada):
    nb, d = c.shape
    nl, _, n = w_ada.shape
    rows = -(-nb // SUBLANES) * SUBLANES
    c_pad = jnp.pad(c, ((0, rows - nb), (0, 0)))
    out = pl.pallas_call(
        _adaln_kernel,
        grid=(nl, n // ADA_TN),
        in_specs=[
            pl.BlockSpec((rows, d), lambda l, j: (0, 0)),
            pl.BlockSpec((1, d, ADA_TN), lambda l, j: (l, 0, j)),
            pl.BlockSpec((1, 1, ADA_TN), lambda l, j: (l, 0, j)),
        ],
        out_specs=pl.BlockSpec((1, rows, ADA_TN), lambda l, j: (l, 0, j)),
        out_shape=jax.ShapeDtypeStruct((nl, rows, n), F32),
        compiler_params=_params("parallel", "parallel"),
        name="adaln",
    )(c_pad, w_ada, b_ada.reshape(nl, 1, n))
    return out[:, :nb]


def _proj_in_kernel(x_ref, sc_ref, sh_ref, gain_ref, w_ref, o_ref, h_ref):
    @pl.when(pl.program_id(2) == 0)
    def _():
        h = _rms(x_ref[0], gain_ref[...]) * (1.0 + sc_ref[0]) + sh_ref[0]
        h_ref[...] = h.astype(BF16)

    o_ref[0] = _dot(h_ref[...], w_ref[...])


def _proj_in(x, sc, sh, gain, w):
    nb, s, d = x.shape
    n = w.shape[1]
    return pl.pallas_call(
        _proj_in_kernel,
        grid=(nb, s // PROJ_TM, n // PROJ_TN),
        in_specs=[
            pl.BlockSpec((1, PROJ_TM, d), lambda b, i, j: (b, i, 0)),
            pl.BlockSpec((1, 1, d), lambda b, i, j: (b, 0, 0)),
            pl.BlockSpec((1, 1, d), lambda b, i, j: (b, 0, 0)),
            pl.BlockSpec((1, d), lambda b, i, j: (0, 0)),
            pl.BlockSpec((d, PROJ_TN), lambda b, i, j: (0, j)),
        ],
        out_specs=pl.BlockSpec((1, PROJ_TM, PROJ_TN), lambda b, i, j: (b, i, j)),
        out_shape=jax.ShapeDtypeStruct((nb, s, n), F32),
        scratch_shapes=[pltpu.VMEM((PROJ_TM, d), BF16)],
        compiler_params=_params("parallel", "parallel", "arbitrary"),
        name="proj_in",
    )(x, sc, sh, gain, w)


def _s5_kernel(u_ref, bm_ref, cm_ref, lam_ref, lam_chunk_ref, d_ref, wglu_ref, o_ref,
               sre, sim, hre, him, yloc):
    ph = pl.program_id(1)
    s = pl.program_id(2)
    kc = sre.shape[1]

    @pl.when(ph == 0)
    def _():
        @pl.when(s == 0)
        def _():
            sre[...] = jnp.zeros_like(sre)
            sim[...] = jnp.zeros_like(sim)

        u = u_ref[0]
        ub = u.astype(BF16)
        for h in range(2):
            cols = slice(h * S5_HALF, (h + 1) * S5_HALF)
            v = _dot(ub[:, cols], bm_ref[h])
            lr = lam_ref[h:h + 1, :]
            li = lam_ref[2 + h:3 + h, :]
            re = sre[h]
            im = sim[h]
            nre = lr * re - li * im + v[:, :S5_HSTATE]
            nim = lr * im + li * re + v[:, S5_HSTATE:]
            sre[h] = nre
            sim[h] = nim
            yl = _dot(nre.astype(BF16), cm_ref[h, 0]) + _dot(nim.astype(BF16), cm_ref[h, 1])
            yloc[s, :, cols] = yl + d_ref[:, cols] * u[:, cols]

        @pl.when(s == S5_LS - 1)
        def _():
            def body(k, carry):
                new = []
                for h in range(2):
                    cre, cim = carry[2 * h], carry[2 * h + 1]
                    hre[h, pl.ds(k, 1), :] = cre
                    him[h, pl.ds(k, 1), :] = cim
                    lr = lam_chunk_ref[h:h + 1, :]
                    li = lam_chunk_ref[2 + h:3 + h, :]
                    new.append(lr * cre - li * cim + sre[h, pl.ds(k, 1), :])
                    new.append(lr * cim + li * cre + sim[h, pl.ds(k, 1), :])
                return tuple(new)

            zero = jnp.zeros((1, S5_HSTATE), F32)
            lax.fori_loop(0, kc, body, (zero, zero, zero, zero))

    @pl.when(ph == 1)
    def _():
        ys = []
        for h in range(2):
            cols = slice(h * S5_HALF, (h + 1) * S5_HALF)
            lr = lam_ref[h:h + 1, :]
            li = lam_ref[2 + h:3 + h, :]
            pre = hre[h]
            pim = him[h]
            nre = lr * pre - li * pim
            nim = lr * pim + li * pre
            hre[h] = nre
            him[h] = nim
            yc = _dot(nre.astype(BF16), cm_ref[h, 0]) + _dot(nim.astype(BF16), cm_ref[h, 1])
            ys.append(yloc[s, :, cols] + yc)
        g = _gelu_tanh(jnp.concatenate(ys, axis=1))
        gate = _sigmoid(_dot(g.astype(BF16), wglu_ref[...]))
        o_ref[0] = (g * gate).astype(o_ref.dtype)


def _s5(proj, bm, cm, lam, lam_chunk, d, wglu):
    nb, s, n = proj.shape
    kc = s // S5_LS
    pv = proj.reshape(nb, kc, S5_LS * n)
    blks = n // COL_BLK
    out = pl.pallas_call(
        _s5_kernel,
        grid=(nb, 2, S5_LS),
        in_specs=[
            pl.BlockSpec((1, kc, S5_WIDTH),
                         lambda b, p, t: (b, 0, jnp.where(p == 0, t, S5_LS - 1) * blks)),
            pl.BlockSpec(bm.shape, lambda b, p, t: (0, 0, 0)),
            pl.BlockSpec(cm.shape, lambda b, p, t: (0, 0, 0, 0)),
            pl.BlockSpec(lam.shape, lambda b, p, t: (0, 0)),
            pl.BlockSpec(lam_chunk.shape, lambda b, p, t: (0, 0)),
            pl.BlockSpec(d.shape, lambda b, p, t: (0, 0)),
            pl.BlockSpec(wglu.shape, lambda b, p, t: (0, 0)),
        ],
        out_specs=pl.BlockSpec((1, kc, S5_WIDTH), lambda b, p, t: (b, 0, t * p)),
        out_shape=jax.ShapeDtypeStruct((nb, kc, S5_LS * S5_WIDTH), BF16),
        scratch_shapes=[pltpu.VMEM((2, kc, S5_HSTATE), F32)] * 4
        + [pltpu.VMEM((S5_LS, kc, S5_WIDTH), F32)],
        compiler_params=_params("parallel", "arbitrary", "arbitrary"),
        name="s5",
    )(pv, bm, cm, lam, lam_chunk, d, wglu)
    return out.reshape(nb, s, S5_WIDTH)


def _s5_prep(lam_re, lam_im, log_dt, b_re, b_im, c_re, c_im):
    lam = lax.complex(lam_re.astype(F32), lam_im.astype(F32))
    dt = jnp.exp(log_dt.astype(F32))[:, None]
    lam_bar = jnp.exp(lam * dt)
    b_bar = ((lam_bar - 1.0) / lam)[..., None] * lax.complex(b_re.astype(F32), b_im.astype(F32))
    lam_chunk = lam_bar
    for _ in range(int(math.log2(S5_LS))):
        lam_chunk = lam_chunk * lam_chunk
    gh = S5_GROUPS // 2
    eye = jnp.eye(gh, dtype=F32)

    def b_tiles(part):
        t = part.reshape(2, gh, S5_STATE, S5_GROUP)
        return jnp.einsum('ab,hapc->hacbp', eye, t).reshape(2, S5_HALF, S5_HSTATE)

    def c_tiles(part):
        t = part.reshape(2, gh, S5_GROUP, S5_STATE)
        return jnp.einsum('ab,hbcp->hapbc', eye, t).reshape(2, S5_HSTATE, S5_HALF)

    bm = jnp.concatenate([b_tiles(jnp.real(b_bar)), b_tiles(jnp.imag(b_bar))], axis=-1).astype(BF16)
    cm = jnp.stack([c_tiles(c_re.astype(F32)), -c_tiles(c_im.astype(F32))], axis=1).astype(BF16)

    def rows(z):
        return jnp.concatenate([jnp.real(z).reshape(2, S5_HSTATE), jnp.imag(z).reshape(2, S5_HSTATE)], 0)

    return bm, cm, rows(lam_bar), rows(lam_chunk)


def _mla_prep_kernel(cq_ref, ckv_ref, qn_ref, kvn_ref, wq_ref, wkv_ref, cos_ref, s1_ref, s2_ref,
                     q_ref, k_ref, v_ref):
    cos_t = cos_ref[0]
    s1 = s1_ref[0]
    s2 = s2_ref[0]
    half = MLA_ROPE // 2

    def rope(x):
        return x * cos_t + pltpu.roll(x, half, 1) * s1 + pltpu.roll(x, LANES - half, 1) * s2

    scale = (MLA_NOPE + MLA_ROPE) ** -0.5
    q = _dot(_rms(cq_ref[0], qn_ref[...]).astype(BF16), wq_ref[...])
    blk = ckv_ref[0]
    kv = _dot(_rms(blk[:, :MLA_KV_RANK], kvn_ref[...]).astype(BF16), wkv_ref[...])
    kr = rope(blk[:, MLA_KV_RANK:MLA_KV_RANK + LANES]).astype(BF16)
    for h in range(MLA_HEADS):
        qn = q[:, h * MLA_QK:h * MLA_QK + MLA_NOPE] * scale
        qr = rope(q[:, h * MLA_QK + MLA_NOPE:(h + 1) * MLA_QK]) * scale
        q_ref[0, h] = jnp.concatenate([qn, qr], axis=1).astype(BF16)
        k_ref[0, h] = jnp.concatenate([kv[:, h * MLA_NOPE:(h + 1) * MLA_NOPE].astype(BF16), kr], axis=1)
        v_ref[0, h] = kv[:, MLA_WIDTH + h * MLA_V:MLA_WIDTH + (h + 1) * MLA_V].astype(BF16)


def _mla_prep(proj, qn, kvn, wq, wkv, cos_t, s1_t, s2_t):
    nb, s, _ = proj.shape
    tab = pl.BlockSpec((1, PREP_TM, LANES), lambda b, i: (b, i, 0))
    return pl.pallas_call(
        _mla_prep_kernel,
        grid=(nb, s // PREP_TM),
        in_specs=[
            pl.BlockSpec((1, PREP_TM, COL_BLK), lambda b, i: (b, i, 1)),
            pl.BlockSpec((1, PREP_TM, COL_BLK), lambda b, i: (b, i, 2)),
            pl.BlockSpec(qn.shape, lambda b, i: (0, 0)),
            pl.BlockSpec(kvn.shape, lambda b, i: (0, 0)),
            pl.BlockSpec(wq.shape, lambda b, i: (0, 0)),
            pl.BlockSpec(wkv.shape, lambda b, i: (0, 0)),
            tab, tab, tab,
        ],
        out_specs=[
            pl.BlockSpec((1, MLA_HEADS, PREP_TM, MLA_QK), lambda b, i: (b, 0, i, 0)),
            pl.BlockSpec((1, MLA_HEADS, PREP_TM, MLA_QK), lambda b, i: (b, 0, i, 0)),
            pl.BlockSpec((1, MLA_HEADS, PREP_TM, MLA_V), lambda b, i: (b, 0, i, 0)),
        ],
        out_shape=[
            jax.ShapeDtypeStruct((nb, MLA_HEADS, s, MLA_QK), BF16),
            jax.ShapeDtypeStruct((nb, MLA_HEADS, s, MLA_QK), BF16),
            jax.ShapeDtypeStruct((nb, MLA_HEADS, s, MLA_V), BF16),
        ],
        compiler_params=_params("parallel", "parallel"),
        name="mla_prep",
    )(proj, proj, qn, kvn, wq, wkv, cos_t, s1_t, s2_t)


def _flash_kernel(q_ref, k_ref, v_ref, o_ref):
    qi = pl.program_id(2)
    q = q_ref[0, 0]
    t = FLASH_T

    def step(j, carry, diagonal):
        m, l, acc = carry
        start = pl.multiple_of(j * t, t)
        k = k_ref[0, 0, pl.ds(start, t), :]
        v = v_ref[0, 0, pl.ds(start, t), :]
        sc = _dot_nt(q, k)
        if diagonal:
            row = lax.broadcasted_iota(jnp.int32, sc.shape, 0)
            col = lax.broadcasted_iota(jnp.int32, sc.shape, 1)
            sc = jnp.where(col <= row, sc, MASK_VALUE)
        m_new = jnp.maximum(m, jnp.max(sc, axis=-1, keepdims=True))
        alpha = jnp.exp(m - m_new)
        p = jnp.exp(sc - m_new)
        l = alpha * l + jnp.sum(p, axis=-1, keepdims=True)
        acc = alpha * acc + _dot(p.astype(BF16), v)
        return m_new, l, acc

    init = (jnp.full((t, 1), -jnp.inf, F32), jnp.zeros((t, 1), F32), jnp.zeros((t, MLA_V), F32))
    carry = lax.fori_loop(0, qi, lambda j, c: step(j, c, False), init)
    _, l, acc = step(qi, carry, True)
    o_ref[0] = (acc / l).astype(o_ref.dtype)


def _flash(q, k, v):
    nb, nh, s, _ = q.shape
    return pl.pallas_call(
        _flash_kernel,
        grid=(nb, nh, s // FLASH_T),
        in_specs=[
            pl.BlockSpec((1, 1, FLASH_T, MLA_QK), lambda b, h, i: (b, h, i, 0)),
            pl.BlockSpec((1, 1, s, MLA_QK), lambda b, h, i: (b, h, 0, 0)),
            pl.BlockSpec((1, 1, s, MLA_V), lambda b, h, i: (b, h, 0, 0)),
        ],
        out_specs=pl.BlockSpec((1, FLASH_T, MLA_V), lambda b, h, i: (b, i, h)),
        out_shape=jax.ShapeDtypeStruct((nb, s, nh * MLA_V), BF16),
        compiler_params=_params("parallel", "parallel", "arbitrary"),
        name="flash",
    )(q, k, v)


def _cumsum_rows(x):
    rows = x.shape[0]
    row = lax.broadcasted_iota(jnp.int32, x.shape, 0)
    d = 1
    while d < rows:
        x = x + jnp.where(row >= d, pltpu.roll(x, d, 0), 0.0)
        d *= 2
    return x


def _hgrn_kernel(q_ref, f_ref, i_ref, g_ref, lb_ref, on_ref, ones_ref, o_ref, st_ref):
    @pl.when(pl.program_id(1) == 0)
    def _():
        st_ref[...] = jnp.zeros_like(st_ref)

    c, w = HG_CHUNK, HG_WIDTH
    z = f_ref[0]
    lb = lb_ref[...]
    e = jnp.exp(-jnp.abs(z))
    r = 1.0 / (1.0 + e)
    pos = z >= 0.0
    sig_z = jnp.where(pos, r, e * r)
    sig_mz = jnp.where(pos, e * r, r)
    logf = jnp.log(lb + (1.0 - lb) * sig_z)
    kk = (1.0 - lb) * sig_mz
    q = _silu(q_ref[0])
    v = i_ref[0]
    gate = _silu(g_ref[0])
    b = _cumsum_rows(logf)
    row = lax.broadcasted_iota(jnp.int32, (c, w), 0)

    q_lvls, k_lvls = [], []
    for m in HG_LEVELS:
        half = m // 2
        b3 = b.reshape(c // m, m, w)
        ref_b = jnp.broadcast_to(b3[:, half - 1:half, :], b3.shape).reshape(c, w)
        x = jnp.exp(-jnp.abs(b - ref_b))
        second = (row & (m - 1)) >= half
        q_lvls.append(jnp.where(second, q * x, 0.0).astype(BF16))
        k_lvls.append(jnp.where(second, 0.0, kk * x).astype(BF16))

    pos8 = row & (SUBLANES - 1)
    o_diag = jnp.zeros((c, w), F32)
    for dl in range(SUBLANES):
        if dl == 0:
            wgt, vr = q * kk, v
        else:
            valid = pos8 >= dl
            kr = pltpu.roll(kk, dl, 0)
            br = pltpu.roll(b, dl, 0)
            vr = pltpu.roll(v, dl, 0)
            wgt = jnp.where(valid, q * kr * jnp.exp(jnp.where(valid, b - br, 0.0)), 0.0)
        wb = wgt.astype(BF16)
        a = jnp.concatenate([_dot(wb[:, :MXU_DIM], ones_ref[...]), _dot(wb[:, MXU_DIM:], ones_ref[...])], axis=1)
        o_diag = o_diag + a * vr

    tt = lax.broadcasted_iota(jnp.int32, (c, c), 0)
    ss = lax.broadcasted_iota(jnp.int32, (c, c), 1)
    same_blk = {m: (tt // m) == (ss // m) for m in HG_LEVELS}

    b_last = b[c - 1:c, :]
    qe =(q * jnp.exp(b)).astype(BF16)
    kd = (kk * jnp.exp(b_last - b)).astype(BF16)
    decay = jnp.exp(b_last)
    vb = v.astype(BF16)
    for h in range(HG_HEADS):
        hs = slice(h * HG_DK, (h + 1) * HG_DK)
        st = st_ref[h]
        attn = jnp.zeros((c, c), F32)
        for m, ql, kl in zip(HG_LEVELS, q_lvls, k_lvls):
            part = _dot_nt(ql[:, hs], kl[:, hs])
            attn = attn + (part if m == c else jnp.where(same_blk[m], part, 0.0))
        o_h =_dot_nt(qe[:, hs], st.astype(BF16)) + _dot(attn.astype(BF16), vb[:, hs]) + o_diag[:, hs]
        st_ref[h] = st * decay[:, hs] + _dot_tn(vb[:, hs], kd[:, hs])
        o_ref[0, :, hs] = (_rms(o_h, on_ref[...]) * gate[:, hs]).astype(o_ref.dtype)


def _hgrn(proj, lb, onorm, ones_blk):
    nb, s, _ = proj.shape
    spec = lambda cb: pl.BlockSpec((1, HG_CHUNK, COL_BLK), lambda b, i: (b, i, cb))
    return pl.pallas_call(
        _hgrn_kernel,
        grid=(nb, s // HG_CHUNK),
        in_specs=[spec(3), spec(4), spec(5), spec(6),
                  pl.BlockSpec(lb.shape, lambda b, i: (0, 0)),
                  pl.BlockSpec(onorm.shape, lambda b, i: (0, 0)),
                  pl.BlockSpec(ones_blk.shape, lambda b, i: (0, 0))],
        out_specs=pl.BlockSpec((1, HG_CHUNK, HG_WIDTH), lambda b, i: (b, i, 0)),
        out_shape=jax.ShapeDtypeStruct((nb, s, HG_WIDTH), BF16),
        scratch_shapes=[pltpu.VMEM((HG_HEADS, HG_DV, HG_DK), F32)],
        compiler_params=_params("parallel", "arbitrary"),
        name="hgrn",
    )(proj, proj, proj, proj, lb, onorm, ones_blk)


def _out_proj_kernel(ys_ref, ym_ref, yh_ref, w_ref, x_ref, gain_ref, g_ref, o_ref):
    mixed = (_dot(ys_ref[0], w_ref[:S5_WIDTH, :])
             + _dot(ym_ref[0], w_ref[S5_WIDTH:S5_WIDTH + MLA_WIDTH, :])
             + _dot(yh_ref[0], w_ref[S5_WIDTH + MLA_WIDTH:, :]))
    o_ref[0] = x_ref[0] + g_ref[0] * _rms(mixed, gain_ref[...])


def _out_proj(y_s5, y_mla, y_hg, w, x, gain, g):
    nb, s, d = x.shape
    row = lambda n: pl.BlockSpec((1, OUT_TM, n), lambda b, i: (b, i, 0))
    return pl.pallas_call(
        _out_proj_kernel,
        grid=(nb, s // OUT_TM),
        in_specs=[row(S5_WIDTH), row(MLA_WIDTH), row(HG_WIDTH),
                  pl.BlockSpec(w.shape, lambda b, i: (0, 0)),
                  row(d),
                  pl.BlockSpec((1, d), lambda b, i: (0, 0)),
                  pl.BlockSpec((1, 1, d), lambda b, i: (b, 0, 0))],
        out_specs=row(d),
        out_shape=jax.ShapeDtypeStruct((nb, s, d), F32),
        compiler_params=_params("parallel", "parallel"),
        name="out_proj",
    )(y_s5, y_mla, y_hg, w, x, gain, g)


def _ffn_kernel(x_ref, xp_ref, sc_ref, sh_ref, g_ref, pre_ref, post_ref,
                wg_ref, wv_ref, cwg_ref, cbg_ref, cwv_ref, cbv_ref, wd_ref, o_ref, h_ref, acc_ref):
    i = pl.program_id(1)
    j = pl.program_id(2)

    @pl.when(j == 0)
    def _():
        def norm(x):
            return _rms(x, pre_ref[...]) * (1.0 + sc_ref[0]) + sh_ref[0]

        hp = jnp.where(i == 0, 0.0, norm(xp_ref[0]))
        h_ref[:HALO, :] = hp.astype(BF16)
        h_ref[HALO:, :] = norm(x_ref[0]).astype(BF16)
        acc_ref[...] = jnp.zeros_like(acc_ref)

    def conv(u, cw_ref, cb_ref):
        u1 = pltpu.roll(u, 1, 0)
        u2 = pltpu.roll(u, 2, 0)
        return (cb_ref[...] + u2[HALO:] * cw_ref[0:1, :] + u1[HALO:] * cw_ref[1:2, :]
                + u[HALO:] * cw_ref[2:3, :])

    h = h_ref[...]
    gate = conv(_dot(h, wg_ref[...]), cwg_ref, cbg_ref)
    val = conv(_dot(h, wv_ref[...]), cwv_ref, cbv_ref)
    act = (_gelu_tanh(gate) * val).astype(BF16)
    acc_ref[...] += _dot(act, wd_ref[...])

    @pl.when(j == pl.num_programs(2) - 1)
    def _():
        o_ref[0] = x_ref[0] + g_ref[0] * _rms(acc_ref[...], post_ref[...])


def _ffn(x, sc, sh, g, pre, post, wg, wv, cwg, cbg, cwv, cbv, wd):
    nb, s, d = x.shape
    ff = wg.shape[1]
    halo_blocks = FFN_TM // HALO
    vec = pl.BlockSpec((1, 1, d), lambda b, i, j: (b, 0, 0))
    par = pl.BlockSpec((1, d), lambda b, i, j: (0, 0))
    col = lambda r: pl.BlockSpec((r, FFN_TF), lambda b, i, j: (0, j))
    return pl.pallas_call(
        _ffn_kernel,
        grid=(nb, s // FFN_TM, ff // FFN_TF),
        in_specs=[
            pl.BlockSpec((1, FFN_TM, d), lambda b, i, j: (b, i, 0)),
            pl.BlockSpec((1, HALO, d), lambda b, i, j: (b, jnp.maximum(i * halo_blocks - 1, 0), 0)),
            vec, vec, vec, par, par,
            col(d), col(d), col(CONV_W), col(1), col(CONV_W), col(1),
            pl.BlockSpec((FFN_TF, d), lambda b, i, j: (j, 0)),
        ],
        out_specs=pl.BlockSpec((1, FFN_TM, d), lambda b, i, j: (b, i, 0)),
        out_shape=jax.ShapeDtypeStruct((nb, s, d), F32),
        scratch_shapes=[pltpu.VMEM((FFN_TM + HALO, d), BF16), pltpu.VMEM((FFN_TM, d), F32)],
        compiler_params=_params("parallel", "parallel", "arbitrary"),
        name="ffn",
    )(x, x, sc, sh, g, pre, post, wg, wv, cwg, cbg, cwv, cbv, wd)


def _pad_cols(a, n):
    return jnp.pad(a, [(0, 0)] * (a.ndim - 1) + [(0, n - a.shape[-1])])


def _layout_w_in(w_in):
    o = 0
    parts = []
    for width in (S5_WIDTH, MLA_Q_RANK, MLA_KV_RANK, MLA_ROPE):
        parts.append(w_in[..., o:o + width])
        o += width
    parts.append(jnp.zeros(w_in.shape[:-1] + (COL_BLK - MLA_KV_RANK - MLA_ROPE,), w_in.dtype))
    parts.append(w_in[..., o:])
    return jnp.concatenate(parts, axis=-1).astype(BF16)


def _layout_w_uq(w):
    nl, r, _ = w.shape
    w = w.reshape(nl, r, MLA_HEADS, MLA_NOPE + MLA_ROPE)
    return _pad_cols(w, MLA_QK).reshape(nl, r, MLA_HEADS * MLA_QK).astype(BF16)


def _layout_w_ukv(w):
    nl, r, _ = w.shape
    w = w.reshape(nl, r, MLA_HEADS, MLA_NOPE + MLA_V)
    k = w[..., :MLA_NOPE].reshape(nl, r, MLA_HEADS * MLA_NOPE)
    v = w[..., MLA_NOPE:].reshape(nl, r, MLA_HEADS * MLA_V)
    return jnp.concatenate([k, v], axis=-1).astype(BF16)


def _rope_tables(positions):
    inv_freq = 1.0 / (ROPE_THETA ** (jnp.arange(0, MLA_ROPE, 2, dtype=F32) / MLA_ROPE))
    ang = positions.astype(F32)[..., None] * inv_freq
    cos, sin = jnp.cos(ang), jnp.sin(ang)
    z32 = jnp.zeros_like(cos)
    z64 = jnp.concatenate([z32, z32], axis=-1)
    return (jnp.concatenate([cos, cos, z64], axis=-1),
            jnp.concatenate([z32, sin, z64], axis=-1),
            jnp.concatenate([-sin, z32, z64], axis=-1))


def kernel(x, c, positions, w_in, s5_lambda_re, s5_lambda_im, s5_log_dt, s5_b_re, s5_b_im, s5_c_re, s5_c_im, s5_d, s5_w_glu, mla_q_norm, mla_w_uq, mla_kv_norm, mla_w_ukv, hg_lb_logits, hg_out_norm, w_out, mix_pre_norm, mix_post_norm, ffn_pre_norm, ffn_post_norm, ffn_w_up, ffn_conv_w, ffn_conv_b, ffn_w_down, w_ada, b_ada):
    nb, s, d = x.shape
    nl = w_in.shape[0]
    assert s % FLASH_T == 0 and s % FFN_TM == 0 and s % (S5_LS * SUBLANES) == 0

    mod = _adaln(c, w_ada, b_ada)
    cos_t, s1_t, s2_t = _rope_tables(positions)
    probs = jax.nn.softmax(hg_lb_logits.astype(F32), axis=0)
    lower_bounds = jnp.cumsum(probs, axis=0) - probs[0:1]

    w_in_p = _layout_w_in(w_in)
    w_uq_p = _layout_w_uq(mla_w_uq)
    w_ukv_p = _layout_w_ukv(mla_w_ukv)
    w_out_b = w_out.astype(BF16)
    w_glu_b = s5_w_glu.astype(BF16)
    w_gate = _pad_cols(ffn_w_up[..., :D_FF], D_FF_PAD).astype(BF16)
    w_val = _pad_cols(ffn_w_up[..., D_FF:], D_FF_PAD).astype(BF16)
    cw_gate = _pad_cols(ffn_conv_w[..., :D_FF], D_FF_PAD)
    cw_val = _pad_cols(ffn_conv_w[..., D_FF:], D_FF_PAD)
    cb_gate = _pad_cols(ffn_conv_b[..., :D_FF], D_FF_PAD)[:, None, :]
    cb_val = _pad_cols(ffn_conv_b[..., D_FF:], D_FF_PAD)[:, None, :]
    w_down = jnp.pad(ffn_w_down, ((0, 0), (0, D_FF_PAD - D_FF), (0, 0))).astype(BF16)
    head_id = jnp.arange(MXU_DIM) // HG_DK
    ones_blk = (head_id[:, None] == head_id[None, :]).astype(BF16)

    for l in range(nl):
        sh1, sc1, g1, sh2, sc2, g2 = [m[:, None, :] for m in jnp.split(mod[l], 6, axis=-1)]
        proj = _proj_in(x, sc1, sh1, mix_pre_norm[l][None, :], w_in_p[l])
        bm, cm, lam, lam_chunk = _s5_prep(s5_lambda_re[l], s5_lambda_im[l], s5_log_dt[l],
                                          s5_b_re[l], s5_b_im[l], s5_c_re[l], s5_c_im[l])
        y_s5 = _s5(proj, bm, cm, lam, lam_chunk, s5_d[l][None, :], w_glu_b[l])
        q, k, v = _mla_prep(proj, mla_q_norm[l][None, :], mla_kv_norm[l][None, :], w_uq_p[l], w_ukv_p[l],
                            cos_t, s1_t, s2_t)
        y_mla = _flash(q, k, v)
        y_hg = _hgrn(proj, lower_bounds[l][None, :], hg_out_norm[l][None, :], ones_blk)
        x = _out_proj(y_s5, y_mla, y_hg, w_out_b[l], x, mix_post_norm[l][None, :], g1)
        x = _ffn(x, sc2, sh2, g2, ffn_pre_norm[l][None, :], ffn_post_norm[l][None, :],
                 w_gate[l], w_val[l], cw_gate[l], cb_gate[l], cw_val[l], cb_val[l], w_down[l])
    return x
```

```python
import functools
import math

import jax
import jax.numpy as jnp
from jax import lax
from jax.experimental import pallas as pl
from jax.experimental.pallas import tpu as pltpu

F32 = jnp.float32
BF16 = jnp.bfloat16

D_MODEL = 2048
DEPTH = 4
S5_WIDTH = 512
S5_GROUP = 16
S5_GROUPS = 32
S5_STATE = 64
MLA_HEADS = 8
MLA_NOPE = 128
MLA_ROPE = 64
MLA_V = 128
MLA_Q_RANK = 512
MLA_KV_RANK = 256
MLA_WIDTH = MLA_HEADS * MLA_V
ROPE_THETA = 10000.0
MASK_VALUE = -1e30
HG_HEADS = 4
HG_DK = 128
HG_DV = 128
HG_WIDTH = HG_HEADS * HG_DV
D_FF = 5504
CONV_W = 3
EPS = 1e-6

LANES = 128
SUBLANES = 8
MXU_DIM = 256
VMEM_LIMIT_BYTES = 56 * 1024 * 1024

COL_BLK = 512
N_COL_BLKS = 7
D_IN_PAD = N_COL_BLKS * COL_BLK
MLA_QK = 2 * LANES
D_FF_PAD = 5632

ADA_TN = 1024
PROJ_TM = 512
PROJ_TN = D_IN_PAD // 2
S5_LS = 16
S5_HALF = S5_WIDTH // 2
S5_HSTATE = (S5_GROUPS // 2) * S5_STATE
PREP_TM = 512
FLASH_T = 512
FLASH_HEADS = 2
FLASH_ROWS = 32
HG_CHUNK = 128
HG_LEVELS = (16, 32, 64, 128)
OUT_TM = 512
FFN_TM = 512
FFN_TF = 512
FFN_ROWS = 64
HALO = SUBLANES


def _sigmoid(x):
    return 1.0 / (1.0 + jnp.exp(-x))


def _silu(x):
    return x * _sigmoid(x)


def _gelu_tanh(x):
    c = math.sqrt(2.0 / math.pi)
    return x * (0.5 * (1.0 + jnp.tanh(c * (x + 0.044715 * (x * x * x)))))


def _rms(x, gain):
    return x * lax.rsqrt(jnp.mean(x * x, axis=-1, keepdims=True) + EPS) * gain


def _dot(a, b):
    return jnp.dot(a, b, preferred_element_type=F32)


def _dot_nt(a, b):
    return lax.dot_general(a, b, (((1,), (1,)), ((), ())), preferred_element_type=F32)


def _dot_tn(a, b):
    return lax.dot_general(a, b, (((0,), (0,)), ((), ())), preferred_element_type=F32)


def _params(*sem):
    return pltpu.CompilerParams(dimension_semantics=sem, vmem_limit_bytes=VMEM_LIMIT_BYTES)


def _adaln_kernel(c_ref, w_ref, b_ref, o_ref):
    ca = _silu(c_ref[...]).astype(BF16)
    o_ref[0] = _dot(ca, w_ref[0].astype(BF16)) + b_ref[0]


def _adaln(c, w_ada, b_ada):
    nb, d = c.shape
    nl, _, n = w_ada.shape
    rows = -(-nb // SUBLANES) * SUBLANES
    c_pad = jnp.pad(c, ((0, rows - nb), (0, 0)))
    out = pl.pallas_call(
        _adaln_kernel,
        grid=(nl, n // ADA_TN),
        in_specs=[
            pl.BlockSpec((rows, d), lambda l, j: (0, 0)),
            pl.BlockSpec((1, d, ADA_TN), lambda l, j: (l, 0, j)),
            pl.BlockSpec((1, 1, ADA_TN), lambda l, j: (l, 0, j)),
        ],
        out_specs=pl.BlockSpec((1, rows, ADA_TN), lambda l, j: (l, 0, j)),
        out_shape=jax.ShapeDtypeStruct((nl, rows, n), F32),
        compiler_params=_params("parallel", "parallel"),
        name="adaln",
    )(c_pad, w_ada, b_ada.reshape(nl, 1, n))
    return out[:, :nb]


def _proj_in_kernel(x_ref, sc_ref, sh_ref, gain_ref, w_ref, o_ref, h_ref):
    @pl.when(pl.program_id(2) == 0)
    def _():
        h = _rms(x_ref[0], gain_ref[...]) * (1.0 + sc_ref[0]) + sh_ref[0]
        h_ref[...] = h.astype(BF16)

    o_ref[0] = _dot(h_ref[...], w_ref[...])


def _proj_in(x, sc, sh, gain, w):
    nb, s, d = x.shape
    n = w.shape[1]
    return pl.pallas_call(
        _proj_in_kernel,
        grid=(nb, s // PROJ_TM, n // PROJ_TN),
        in_specs=[
            pl.BlockSpec((1, PROJ_TM, d), lambda b, i, j: (b, i, 0)),
            pl.BlockSpec((1, 1, d), lambda b, i, j: (b, 0, 0)),
            pl.BlockSpec((1, 1, d), lambda b, i, j: (b, 0, 0)),
            pl.BlockSpec((1, d), lambda b, i, j: (0, 0)),
            pl.BlockSpec((d, PROJ_TN), lambda b, i, j: (0, j)),
        ],
        out_specs=pl.BlockSpec((1, PROJ_TM, PROJ_TN), lambda b, i, j: (b, i, j)),
        out_shape=jax.ShapeDtypeStruct((nb, s, n), F32),
        scratch_shapes=[pltpu.VMEM((PROJ_TM, d), BF16)],
        compiler_params=_params("parallel", "parallel", "arbitrary"),
        name="proj_in",
    )(x, sc, sh, gain, w)


def _s5_kernel(u_ref, bm_ref, cm_ref, lam_ref, lam_chunk_ref, d_ref, wglu_ref, o_ref,
               sre, sim, hre, him, yloc):
    ph = pl.program_id(1)
    s = pl.program_id(2)
    kc = sre.shape[1]

    @pl.when(ph == 0)
    def _():
        @pl.when(s == 0)
        def _():
            sre[...] = jnp.zeros_like(sre)
            sim[...] = jnp.zeros_like(sim)

        u = u_ref[0]
        ub = u.astype(BF16)
        for h in range(2):
            cols = slice(h * S5_HALF, (h + 1) * S5_HALF)
            v = _dot(ub[:, cols], bm_ref[h])
            lr = lam_ref[h:h + 1, :]
            li = lam_ref[2 + h:3 + h, :]
            re = sre[h]
            im = sim[h]
            nre = lr * re - li * im + v[:, :S5_HSTATE]
            nim = lr * im + li * re + v[:, S5_HSTATE:]
            sre[h] = nre
            sim[h] = nim
            yl = _dot(nre.astype(BF16), cm_ref[h, 0]) + _dot(nim.astype(BF16), cm_ref[h, 1])
            yloc[s, :, cols] = yl + d_ref[:, cols] * u[:, cols]

        @pl.when(s == S5_LS - 1)
        def _():
            def body(k, carry):
                new = []
                for h in range(2):
                    cre, cim = carry[2 * h], carry[2 * h + 1]
                    hre[h, pl.ds(k, 1), :] = cre
                    him[h, pl.ds(k, 1), :] = cim
                    lr = lam_chunk_ref[h:h + 1, :]
                    li = lam_chunk_ref[2 + h:3 + h, :]
                    new.append(lr * cre - li * cim + sre[h, pl.ds(k, 1), :])
                    new.append(lr * cim + li * cre + sim[h, pl.ds(k, 1), :])
                return tuple(new)

            zero = jnp.zeros((1, S5_HSTATE), F32)
            lax.fori_loop(0, kc, body, (zero, zero, zero, zero))

    @pl.when(ph == 1)
    def _():
        ys = []
        for h in range(2):
            cols = slice(h * S5_HALF, (h + 1) * S5_HALF)
            lr = lam_ref[h:h + 1, :]
            li = lam_ref[2 + h:3 + h, :]
            pre = hre[h]
            pim = him[h]
            nre = lr * pre - li * pim
            nim = lr * pim + li * pre
            hre[h] = nre
            him[h] = nim
            yc = _dot(nre.astype(BF16), cm_ref[h, 0]) + _dot(nim.astype(BF16), cm_ref[h, 1])
            ys.append(yloc[s, :, cols] + yc)
        g = _gelu_tanh(jnp.concatenate(ys, axis=1))
        gate = _sigmoid(_dot(g.astype(BF16), wglu_ref[...]))
        o_ref[0] = (g * gate).astype(o_ref.dtype)


def _s5(proj, bm, cm, lam, lam_chunk, d, wglu):
    nb, s, n = proj.shape
    kc = s // S5_LS
    pv = proj.reshape(nb, kc, S5_LS * n)
    blks = n // COL_BLK
    out = pl.pallas_call(
        _s5_kernel,
        grid=(nb, 2, S5_LS),
        in_specs=[
            pl.BlockSpec((1, kc, S5_WIDTH),
                         lambda b, p, t: (b, 0, jnp.where(p == 0, t, S5_LS - 1) * blks)),
            pl.BlockSpec(bm.shape, lambda b, p, t: (0, 0, 0)),
            pl.BlockSpec(cm.shape, lambda b, p, t: (0, 0, 0, 0)),
            pl.BlockSpec(lam.shape, lambda b, p, t: (0, 0)),
            pl.BlockSpec(lam_chunk.shape, lambda b, p, t: (0, 0)),
            pl.BlockSpec(d.shape, lambda b, p, t: (0, 0)),
            pl.BlockSpec(wglu.shape, lambda b, p, t: (0, 0)),
        ],
        out_specs=pl.BlockSpec((1, kc, S5_WIDTH), lambda b, p, t: (b, 0, t * p)),
        out_shape=jax.ShapeDtypeStruct((nb, kc, S5_LS * S5_WIDTH), BF16),
        scratch_shapes=[pltpu.VMEM((2, kc, S5_HSTATE), F32)] * 4
        + [pltpu.VMEM((S5_LS, kc, S5_WIDTH), F32)],
        compiler_params=_params("parallel", "arbitrary", "arbitrary"),
        name="s5",
    )(pv, bm, cm, lam, lam_chunk, d, wglu)
    return out.reshape(nb, s, S5_WIDTH)


def _s5_prep(lam_re, lam_im, log_dt, b_re, b_im, c_re, c_im):
    lam = lax.complex(lam_re.astype(F32), lam_im.astype(F32))
    dt = jnp.exp(log_dt.astype(F32))[:, None]
    lam_bar = jnp.exp(lam * dt)
    b_bar = ((lam_bar - 1.0) / lam)[..., None] * lax.complex(b_re.astype(F32), b_im.astype(F32))
    lam_chunk = lam_bar
    for _ in range(int(math.log2(S5_LS))):
        lam_chunk = lam_chunk * lam_chunk
    gh = S5_GROUPS // 2
    eye = jnp.eye(gh, dtype=F32)

    def b_tiles(part):
        t = part.reshape(2, gh, S5_STATE, S5_GROUP)
        return jnp.einsum('ab,hapc->hacbp', eye, t).reshape(2, S5_HALF, S5_HSTATE)

    def c_tiles(part):
        t = part.reshape(2, gh, S5_GROUP, S5_STATE)
        return jnp.einsum('ab,hbcp->hapbc', eye, t).reshape(2, S5_HSTATE, S5_HALF)

    bm = jnp.concatenate([b_tiles(jnp.real(b_bar)), b_tiles(jnp.imag(b_bar))], axis=-1).astype(BF16)
    cm = jnp.stack([c_tiles(c_re.astype(F32)), -c_tiles(c_im.astype(F32))], axis=1).astype(BF16)

    def rows(z):
        return jnp.concatenate([jnp.real(z).reshape(2, S5_HSTATE), jnp.imag(z).reshape(2, S5_HSTATE)], 0)

    return bm, cm, rows(lam_bar), rows(lam_chunk)


def _mla_prep_kernel(cq_ref, ckv_ref, qn_ref, kvn_ref, wq_ref, wkv_ref, cos_ref, s1_ref, s2_ref,
                     q_ref, k_ref, v_ref):
    cos_t = cos_ref[0]
    s1 = s1_ref[0]
    s2 = s2_ref[0]
    half = MLA_ROPE // 2

    def rope(x):
        return x * cos_t + pltpu.roll(x, half, 1) * s1 + pltpu.roll(x, LANES - half, 1) * s2

    scale = (MLA_NOPE + MLA_ROPE) ** -0.5 * math.log2(math.e)
    q = _dot(_rms(cq_ref[0], qn_ref[...]).astype(BF16), wq_ref[...])
    blk = ckv_ref[0]
    kv = _dot(_rms(blk[:, :MLA_KV_RANK], kvn_ref[...]).astype(BF16), wkv_ref[...])
    kr = rope(blk[:, MLA_KV_RANK:MLA_KV_RANK + LANES]).astype(BF16)
    for h in range(MLA_HEADS):
        qn = q[:, h * MLA_QK:h * MLA_QK + MLA_NOPE] * scale
        qr = rope(q[:, h * MLA_QK + MLA_NOPE:(h + 1) * MLA_QK]) * scale
        q_ref[0, h] = jnp.concatenate([qn, qr], axis=1).astype(BF16)
        k_ref[0, h] = jnp.concatenate([kv[:, h * MLA_NOPE:(h + 1) * MLA_NOPE].astype(BF16), kr], axis=1)
        vh = kv[:, MLA_WIDTH + h * MLA_V:MLA_WIDTH + (h + 1) * MLA_V]
        v_ref[0, h] = jnp.concatenate([vh, jnp.ones_like(vh)], axis=1).astype(BF16)


def _mla_prep(proj, qn, kvn, wq, wkv, cos_t, s1_t, s2_t):
    nb, s, _ = proj.shape
    tab = pl.BlockSpec((1, PREP_TM, LANES), lambda b, i: (b, i, 0))
    return pl.pallas_call(
        _mla_prep_kernel,
        grid=(nb, s // PREP_TM),
        in_specs=[
            pl.BlockSpec((1, PREP_TM, COL_BLK), lambda b, i: (b, i, 1)),
            pl.BlockSpec((1, PREP_TM, COL_BLK), lambda b, i: (b, i, 2)),
            pl.BlockSpec(qn.shape, lambda b, i: (0, 0)),
            pl.BlockSpec(kvn.shape, lambda b, i: (0, 0)),
            pl.BlockSpec(wq.shape, lambda b, i: (0, 0)),
            pl.BlockSpec(wkv.shape, lambda b, i: (0, 0)),
            tab, tab, tab,
        ],
        out_specs=[
            pl.BlockSpec((1, MLA_HEADS, PREP_TM, MLA_QK), lambda b, i: (b, 0, i, 0)),
            pl.BlockSpec((1, MLA_HEADS, PREP_TM, MLA_QK), lambda b, i: (b, 0, i, 0)),
            pl.BlockSpec((1, MLA_HEADS, PREP_TM, 2 * MLA_V), lambda b, i: (b, 0, i, 0)),
        ],
        out_shape=[
            jax.ShapeDtypeStruct((nb, MLA_HEADS, s, MLA_QK), BF16),
            jax.ShapeDtypeStruct((nb, MLA_HEADS, s, MLA_QK), BF16),
            jax.ShapeDtypeStruct((nb, MLA_HEADS, s, 2 * MLA_V), BF16),
        ],
        compiler_params=_params("parallel", "parallel"),
        name="mla_prep",
    )(proj, proj, qn, kvn, wq, wkv, cos_t, s1_t, s2_t)


def _flash_kernel(q_ref, k_ref, v_ref, o_ref, s_ref, p_ref, m_ref, a_ref, acc_ref):
    qi = pl.program_id(2)
    t = FLASH_T
    heads = range(FLASH_HEADS)
    nh = FLASH_HEADS
    m_ref[...] = jnp.full(m_ref.shape, -jnp.inf, F32)
    acc_ref[...] = jnp.zeros_like(acc_ref)
    p_ref[nh:] = jnp.zeros((nh,) + p_ref.shape[1:], BF16)
    a_ref[nh:] = jnp.ones((nh,) + a_ref.shape[1:], F32)

    def scores(j, slot):
        start = pl.multiple_of(j * t, t)
        for h in heads:
            s_ref[slot * nh + h] = _dot_nt(q_ref[0, h], k_ref[0, h, pl.ds(start, t), :])

    def softmax(slot, diagonal):
        for h in heads:
            for r0 in range(0, t, FLASH_ROWS):
                rows = slice(r0, r0 + FLASH_ROWS)
                sc = s_ref[slot * nh + h, rows, :]
                if diagonal:
                    row = lax.broadcasted_iota(jnp.int32, sc.shape, 0) + r0
                    col = lax.broadcasted_iota(jnp.int32, sc.shape, 1)
                    sc = jnp.where(col <= row, sc, MASK_VALUE)
                m_old = m_ref[h, rows, :]
                m_new = jnp.maximum(m_old, jnp.max(sc, axis=-1, keepdims=True))
                p_ref[slot * nh + h, rows, :] = jnp.exp2(sc - m_new[:, :1]).astype(BF16)
                a_ref[slot * nh + h, rows, :] = jnp.exp2(m_old - m_new)
                m_ref[h, rows, :] = m_new

    def accumulate(j, slot):
        start = pl.multiple_of(j * t, t)
        for h in heads:
            pv = _dot(p_ref[slot * nh + h], v_ref[0, h, pl.ds(start, t), :])
            alpha = a_ref[slot * nh + h]
            acc_ref[h] = jnp.concatenate([alpha, alpha], axis=1) * acc_ref[h] + pv

    scores(0, 0)

    def stage(j, slot):
        softmax(slot, False)
        scores(j + 1, 1 - slot)
        accumulate(jnp.maximum(j - 1, 0), 1 - slot)

    def body(jj, carry):
        j = 2 * jj
        stage(j, 0)

        @pl.when(j + 1 < qi)
        def _():
            stage(j + 1, 1)

        return carry

    lax.fori_loop(0, (qi + 1) // 2, body, 0)
    for slot in range(2):
        @pl.when(lax.rem(qi, 2) == slot)
        def _():
            softmax(slot, True)
            accumulate(jnp.maximum(qi - 1, 0), 1 - slot)
            accumulate(qi, slot)

    for h in heads:
        acc = acc_ref[h]
        o_ref[0, :, h * MLA_V:(h + 1) * MLA_V] = (acc[:, :MLA_V] / acc[:, MLA_V:]).astype(o_ref.dtype)


def _flash(q, k, v):
    nb, nh, s, _ = q.shape
    hb = FLASH_HEADS
    return pl.pallas_call(
        _flash_kernel,
        grid=(nb, nh // hb, s // FLASH_T),
        in_specs=[
            pl.BlockSpec((1, hb, FLASH_T, MLA_QK), lambda b, h, i: (b, h, i, 0)),
            pl.BlockSpec((1, hb, s, MLA_QK), lambda b, h, i: (b, h, 0, 0)),
            pl.BlockSpec((1, hb, s, 2 * MLA_V), lambda b, h, i: (b, h, 0, 0)),
        ],
        out_specs=pl.BlockSpec((1, FLASH_T, hb * MLA_V), lambda b, h, i: (b, i, h)),
        out_shape=jax.ShapeDtypeStruct((nb, s, nh * MLA_V), BF16),
        scratch_shapes=[pltpu.VMEM((2 * hb, FLASH_T, FLASH_T), F32), pltpu.VMEM((2 * hb, FLASH_T, FLASH_T), BF16),
                        pltpu.VMEM((hb, FLASH_T, LANES), F32), pltpu.VMEM((2 * hb, FLASH_T, LANES), F32),
                        pltpu.VMEM((hb, FLASH_T, 2 * MLA_V), F32)],
        compiler_params=_params("parallel", "parallel", "arbitrary"),
        name="flash",
    )(q, k, v)


def _cumsum_rows(x):
    rows = x.shape[0]
    row = lax.broadcasted_iota(jnp.int32, x.shape, 0)
    d = 1
    while d < rows:
        x = x + jnp.where(row >= d, pltpu.roll(x, d, 0), 0.0)
        d *= 2
    return x


def _hgrn_kernel(q_ref, f_ref, i_ref, g_ref, lb_ref, on_ref, ones_ref, o_ref, st_ref):
    @pl.when(pl.program_id(1) == 0)
    def _():
        st_ref[...] = jnp.zeros_like(st_ref)

    c, w = HG_CHUNK, HG_WIDTH
    z = f_ref[0]
    lb = lb_ref[...]
    e = jnp.exp(-jnp.abs(z))
    r = 1.0 / (1.0 + e)
    pos = z >= 0.0
    sig_z = jnp.where(pos, r, e * r)
    sig_mz = jnp.where(pos, e * r, r)
    logf = jnp.log(lb + (1.0 - lb) * sig_z)
    kk = (1.0 - lb) * sig_mz
    q = _silu(q_ref[0])
    v = i_ref[0]
    gate = _silu(g_ref[0])
    b = _cumsum_rows(logf)
    row = lax.broadcasted_iota(jnp.int32, (c, w), 0)

    q_lvls, k_lvls = [], []
    for m in HG_LEVELS:
        half = m // 2
        b3 = b.reshape(c // m, m, w)
        ref_b = jnp.broadcast_to(b3[:, half - 1:half, :], b3.shape).reshape(c, w)
        x = jnp.exp(-jnp.abs(b - ref_b))
        second = (row & (m - 1)) >= half
        q_lvls.append(jnp.where(second, q * x, 0.0).astype(BF16))
        k_lvls.append(jnp.where(second, 0.0, kk * x).astype(BF16))

    pos8 = row & (SUBLANES - 1)
    o_diag = jnp.zeros((c, w), F32)
    for dl in range(SUBLANES):
        if dl == 0:
            wgt, vr = q * kk, v
        else:
            valid = pos8 >= dl
            kr = pltpu.roll(kk, dl, 0)
            br = pltpu.roll(b, dl, 0)
            vr = pltpu.roll(v, dl, 0)
            wgt = jnp.where(valid, q * kr * jnp.exp(jnp.where(valid, b - br, 0.0)), 0.0)
        wb = wgt.astype(BF16)
        a = jnp.concatenate([_dot(wb[:, :MXU_DIM], ones_ref[...]), _dot(wb[:, MXU_DIM:], ones_ref[...])], axis=1)
        o_diag = o_diag + a * vr

    tt = lax.broadcasted_iota(jnp.int32, (c, c), 0)
    ss = lax.broadcasted_iota(jnp.int32, (c, c), 1)
    same_blk = {m: (tt // m) == (ss // m) for m in HG_LEVELS}

    b_last = b[c - 1:c, :]
    qe =(q * jnp.exp(b)).astype(BF16)
    kd = (kk * jnp.exp(b_last - b)).astype(BF16)
    decay = jnp.exp(b_last)
    vb = v.astype(BF16)
    for h in range(HG_HEADS):
        hs = slice(h * HG_DK, (h + 1) * HG_DK)
        st = st_ref[h]
        attn = jnp.zeros((c, c), F32)
        for m, ql, kl in zip(HG_LEVELS, q_lvls, k_lvls):
            part = _dot_nt(ql[:, hs], kl[:, hs])
            attn = attn + (part if m == c else jnp.where(same_blk[m], part, 0.0))
        o_h =_dot_nt(qe[:, hs], st.astype(BF16)) + _dot(attn.astype(BF16), vb[:, hs]) + o_diag[:, hs]
        st_ref[h] = st * decay[:, hs] + _dot_tn(vb[:, hs], kd[:, hs])
        o_ref[0, :, hs] = (_rms(o_h, on_ref[...]) * gate[:, hs]).astype(o_ref.dtype)


def _hgrn(proj, lb, onorm, ones_blk):
    nb, s, _ = proj.shape
    spec = lambda cb: pl.BlockSpec((1, HG_CHUNK, COL_BLK), lambda b, i: (b, i, cb))
    return pl.pallas_call(
        _hgrn_kernel,
        grid=(nb, s // HG_CHUNK),
        in_specs=[spec(3), spec(4), spec(5), spec(6),
                  pl.BlockSpec(lb.shape, lambda b, i: (0, 0)),
                  pl.BlockSpec(onorm.shape, lambda b, i: (0, 0)),
                  pl.BlockSpec(ones_blk.shape, lambda b, i: (0, 0))],
        out_specs=pl.BlockSpec((1, HG_CHUNK, HG_WIDTH), lambda b, i: (b, i, 0)),
        out_shape=jax.ShapeDtypeStruct((nb, s, HG_WIDTH), BF16),
        scratch_shapes=[pltpu.VMEM((HG_HEADS, HG_DV, HG_DK), F32)],
        compiler_params=_params("parallel", "arbitrary"),
        name="hgrn",
    )(proj, proj, proj, proj, lb, onorm, ones_blk)


def _out_proj_kernel(ys_ref, ym_ref, yh_ref, w_ref, x_ref, gain_ref, g_ref, o_ref):
    mixed = (_dot(ys_ref[0], w_ref[:S5_WIDTH, :])
             + _dot(ym_ref[0], w_ref[S5_WIDTH:S5_WIDTH + MLA_WIDTH, :])
             + _dot(yh_ref[0], w_ref[S5_WIDTH + MLA_WIDTH:, :]))
    o_ref[0] = x_ref[0] + g_ref[0] * _rms(mixed, gain_ref[...])


def _out_proj(y_s5, y_mla, y_hg, w, x, gain, g):
    nb, s, d = x.shape
    row = lambda n: pl.BlockSpec((1, OUT_TM, n), lambda b, i: (b, i, 0))
    return pl.pallas_call(
        _out_proj_kernel,
        grid=(nb, s // OUT_TM),
        in_specs=[row(S5_WIDTH), row(MLA_WIDTH), row(HG_WIDTH),
                  pl.BlockSpec(w.shape, lambda b, i: (0, 0)),
                  row(d),
                  pl.BlockSpec((1, d), lambda b, i: (0, 0)),
                  pl.BlockSpec((1, 1, d), lambda b, i: (b, 0, 0))],
        out_specs=row(d),
        out_shape=jax.ShapeDtypeStruct((nb, s, d), F32),
        compiler_params=_params("parallel", "parallel"),
        name="out_proj",
    )(y_s5, y_mla, y_hg, w, x, gain, g)


def _ffn_kernel(x_ref, xp_ref, sc_ref, sh_ref, g_ref, pre_ref, post_ref,
                wg_ref, wv_ref, cwg_ref, cbg_ref, cwv_ref, cbv_ref, wd_ref, o_ref,
                h_ref, acc_ref, ug_ref, uv_ref, act_ref):
    i = pl.program_id(1)
    j = pl.program_id(2)

    @pl.when(j == 0)
    def _():
        def norm(x):
            return _rms(x, pre_ref[...]) * (1.0 + sc_ref[0]) + sh_ref[0]

        hp = jnp.where(i == 0, 0.0, norm(xp_ref[0]))
        h_ref[:HALO, :] = hp.astype(BF16)
        h_ref[HALO:, :] = norm(x_ref[0]).astype(BF16)
        acc_ref[...] = jnp.zeros_like(acc_ref)

    ug_ref[...] = _dot(h_ref[...], wg_ref[...])
    uv_ref[...] = _dot(h_ref[...], wv_ref[...])

    def conv(u_ref, cw_ref, cb_ref, r0):
        return (cb_ref[...] + u_ref[r0 + HALO - 2:r0 + HALO - 2 + FFN_ROWS, :] * cw_ref[0:1, :]
                + u_ref[r0 + HALO - 1:r0 + HALO - 1 + FFN_ROWS, :] * cw_ref[1:2, :]
                + u_ref[r0 + HALO:r0 + HALO + FFN_ROWS, :] * cw_ref[2:3, :])

    for r0 in range(0, FFN_TM, FFN_ROWS):
        gate = conv(ug_ref, cwg_ref, cbg_ref, r0)
        val = conv(uv_ref, cwv_ref, cbv_ref, r0)
        act_ref[r0:r0 + FFN_ROWS, :] = (_gelu_tanh(gate) * val).astype(BF16)
    acc_ref[...] += _dot(act_ref[...], wd_ref[...])

    @pl.when(j == pl.num_programs(2) - 1)
    def _():
        o_ref[0] = x_ref[0] + g_ref[0] * _rms(acc_ref[...], post_ref[...])


def _ffn(x, sc, sh, g, pre, post, wg, wv, cwg, cbg, cwv, cbv, wd):
    nb, s, d = x.shape
    ff = wg.shape[1]
    halo_blocks = FFN_TM // HALO
    vec = pl.BlockSpec((1, 1, d), lambda b, i, j: (b, 0, 0))
    par = pl.BlockSpec((1, d), lambda b, i, j: (0, 0))
    col = lambda r: pl.BlockSpec((r, FFN_TF), lambda b, i, j: (0, j))
    return pl.pallas_call(
        _ffn_kernel,
        grid=(nb, s // FFN_TM, ff // FFN_TF),
        in_specs=[
            pl.BlockSpec((1, FFN_TM, d), lambda b, i, j: (b, i, 0)),
            pl.BlockSpec((1, HALO, d), lambda b, i, j: (b, jnp.maximum(i * halo_blocks - 1, 0), 0)),
            vec, vec, vec, par, par,
            col(d), col(d), col(CONV_W), col(1), col(CONV_W), col(1),
            pl.BlockSpec((FFN_TF, d), lambda b, i, j: (j, 0)),
        ],
        out_specs=pl.BlockSpec((1, FFN_TM, d), lambda b, i, j: (b, i, 0)),
        out_shape=jax.ShapeDtypeStruct((nb, s, d), F32),
        scratch_shapes=[pltpu.VMEM((FFN_TM + HALO, d), BF16), pltpu.VMEM((FFN_TM, d), F32),
                        pltpu.VMEM((FFN_TM + HALO, FFN_TF), F32), pltpu.VMEM((FFN_TM + HALO, FFN_TF), F32),
                        pltpu.VMEM((FFN_TM, FFN_TF), BF16)],
        compiler_params=_params("parallel", "parallel", "arbitrary"),
        name="ffn",
    )(x, x, sc, sh, g, pre, post, wg, wv, cwg, cbg, cwv, cbv, wd)


def _pad_cols(a, n):
    return jnp.pad(a, [(0, 0)] * (a.ndim - 1) + [(0, n - a.shape[-1])])


def _layout_w_in(w_in):
    o = 0
    parts = []
    for width in (S5_WIDTH, MLA_Q_RANK, MLA_KV_RANK, MLA_ROPE):
        parts.append(w_in[..., o:o + width])
        o += width
    parts.append(jnp.zeros(w_in.shape[:-1] + (COL_BLK - MLA_KV_RANK - MLA_ROPE,), w_in.dtype))
    parts.append(w_in[..., o:])
    return jnp.concatenate(parts, axis=-1).astype(BF16)


def _layout_w_uq(w):
    nl, r, _ = w.shape
    w = w.reshape(nl, r, MLA_HEADS, MLA_NOPE + MLA_ROPE)
    return _pad_cols(w, MLA_QK).reshape(nl, r, MLA_HEADS * MLA_QK).astype(BF16)


def _layout_w_ukv(w):
    nl, r, _ = w.shape
    w = w.reshape(nl, r, MLA_HEADS, MLA_NOPE + MLA_V)
    k = w[..., :MLA_NOPE].reshape(nl, r, MLA_HEADS * MLA_NOPE)
    v = w[..., MLA_NOPE:].reshape(nl, r, MLA_HEADS * MLA_V)
    return jnp.concatenate([k, v], axis=-1).astype(BF16)


def _rope_tables(positions):
    inv_freq = 1.0 / (ROPE_THETA ** (jnp.arange(0, MLA_ROPE, 2, dtype=F32) / MLA_ROPE))
    ang = positions.astype(F32)[..., None] * inv_freq
    cos, sin = jnp.cos(ang), jnp.sin(ang)
    z32 = jnp.zeros_like(cos)
    z64 = jnp.concatenate([z32, z32], axis=-1)
    return (jnp.concatenate([cos, cos, z64], axis=-1),
            jnp.concatenate([z32, sin, z64], axis=-1),
            jnp.concatenate([-sin, z32, z64], axis=-1))


def kernel(x, c, positions, w_in, s5_lambda_re, s5_lambda_im, s5_log_dt, s5_b_re, s5_b_im, s5_c_re, s5_c_im, s5_d, s5_w_glu, mla_q_norm, mla_w_uq, mla_kv_norm, mla_w_ukv, hg_lb_logits, hg_out_norm, w_out, mix_pre_norm, mix_post_norm, ffn_pre_norm, ffn_post_norm, ffn_w_up, ffn_conv_w, ffn_conv_b, ffn_w_down, w_ada, b_ada):
    nb, s, d = x.shape
    nl = w_in.shape[0]
    assert s % FLASH_T == 0 and s % FFN_TM == 0 and s % (S5_LS * SUBLANES) == 0

    mod = _adaln(c, w_ada, b_ada)
    cos_t, s1_t, s2_t = _rope_tables(positions)
    probs = jax.nn.softmax(hg_lb_logits.astype(F32), axis=0)
    lower_bounds = jnp.cumsum(probs, axis=0) - probs[0:1]

    w_in_p = _layout_w_in(w_in)
    w_uq_p = _layout_w_uq(mla_w_uq)
    w_ukv_p = _layout_w_ukv(mla_w_ukv)
    w_out_b = w_out.astype(BF16)
    w_glu_b = s5_w_glu.astype(BF16)
    w_gate = _pad_cols(ffn_w_up[..., :D_FF], D_FF_PAD).astype(BF16)
    w_val = _pad_cols(ffn_w_up[..., D_FF:], D_FF_PAD).astype(BF16)
    cw_gate = _pad_cols(ffn_conv_w[..., :D_FF], D_FF_PAD)
    cw_val = _pad_cols(ffn_conv_w[..., D_FF:], D_FF_PAD)
    cb_gate = _pad_cols(ffn_conv_b[..., :D_FF], D_FF_PAD)[:, None, :]
    cb_val = _pad_cols(ffn_conv_b[..., D_FF:], D_FF_PAD)[:, None, :]
    w_down = jnp.pad(ffn_w_down, ((0, 0), (0, D_FF_PAD - D_FF), (0, 0))).astype(BF16)
    head_id = jnp.arange(MXU_DIM) // HG_DK
    ones_blk = (head_id[:, None] == head_id[None, :]).astype(BF16)

    for l in range(nl):
        sh1, sc1, g1, sh2, sc2, g2 = [m[:, None, :] for m in jnp.split(mod[l], 6, axis=-1)]
        proj = _proj_in(x, sc1, sh1, mix_pre_norm[l][None, :], w_in_p[l])
        bm, cm, lam, lam_chunk = _s5_prep(s5_lambda_re[l], s5_lambda_im[l], s5_log_dt[l],
                                          s5_b_re[l], s5_b_im[l], s5_c_re[l], s5_c_im[l])
        y_s5 = _s5(proj, bm, cm, lam, lam_chunk, s5_d[l][None, :], w_glu_b[l])
        q, k, v = _mla_prep(proj, mla_q_norm[l][None, :], mla_kv_norm[l][None, :], w_uq_p[l], w_ukv_p[l],
                            cos_t, s1_t, s2_t)
        y_mla = _flash(q, k, v)
        y_hg = _hgrn(proj, lower_bounds[l][None, :], hg_out_norm[l][None, :], ones_blk)
        x = _out_proj(y_s5, y_mla, y_hg, w_out_b[l], x, mix_post_norm[l][None, :], g1)
        x = _ffn(x, sc2, sh2, g2, ffn_pre_norm[l][None, :], ffn_post_norm[l][None, :],
                 w_gate[l], w_val[l], cw_gate[l], cb_gate[l], cw_val[l], cb_val[l], w_down[l])
    return x
```

```python
import functools
import math

import jax
import jax.numpy as jnp
from jax import lax
from jax.experimental import pallas as pl
from jax.experimental.pallas import tpu as pltpu

F32 = jnp.float32
BF16 = jnp.bfloat16

D_MODEL = 2048
DEPTH = 4
S5_WIDTH = 512
S5_GROUP = 16
S5_GROUPS = 32
S5_STATE = 64
MLA_HEADS = 8
MLA_NOPE = 128
MLA_ROPE = 64
MLA_V = 128
MLA_Q_RANK = 512
MLA_KV_RANK = 256
MLA_WIDTH = MLA_HEADS * MLA_V
ROPE_THETA = 10000.0
MASK_VALUE = -1e30
HG_HEADS = 4
HG_DK = 128
HG_DV = 128
HG_WIDTH = HG_HEADS * HG_DV
D_FF = 5504
CONV_W = 3
EPS = 1e-6

LANES = 128
SUBLANES = 8
MXU_DIM = 256
VMEM_LIMIT_BYTES = 56 * 1024 * 1024

COL_BLK = 512
N_COL_BLKS = 7
D_IN_PAD = N_COL_BLKS * COL_BLK
MLA_QK = 2 * LANES
D_FF_PAD = 5632

ADA_TN = 1024
PROJ_TM = 512
PROJ_TN = D_IN_PAD // 2
S5_LS = 16
S5_HALF = S5_WIDTH // 2
S5_HSTATE = (S5_GROUPS // 2) * S5_STATE
PREP_TM = 512
FLASH_T = 512
FLASH_HEADS = 2
FLASH_ROWS = 32
HG_CHUNK = 128
HG_LEVELS = (16, 32, 64, 128)
HG_SEQS = 2
OUT_TM = 512
FFN_TM = 512
FFN_TF = 512
FFN_ROWS = 64
HALO = SUBLANES


def _sigmoid(x):
    return 1.0 / (1.0 + jnp.exp(-x))


def _silu(x):
    return x * _sigmoid(x)


def _gelu_tanh(x):
    c = math.sqrt(2.0 / math.pi)
    return x * (0.5 * (1.0 + jnp.tanh(c * (x + 0.044715 * (x * x * x)))))


def _rms(x, gain):
    return x * lax.rsqrt(jnp.mean(x * x, axis=-1, keepdims=True) + EPS) * gain


def _dot(a, b):
    return jnp.dot(a, b, preferred_element_type=F32)


def _dot_nt(a, b):
    return lax.dot_general(a, b, (((1,), (1,)), ((), ())), preferred_element_type=F32)


def _dot_tn(a, b):
    return lax.dot_general(a, b, (((0,), (0,)), ((), ())), preferred_element_type=F32)


def _params(*sem):
    return pltpu.CompilerParams(dimension_semantics=sem, vmem_limit_bytes=VMEM_LIMIT_BYTES)


def _adaln_kernel(c_ref, w_ref, b_ref, o_ref):
    ca = _silu(c_ref[...]).astype(BF16)
    o_ref[0] = _dot(ca, w_ref[0].astype(BF16)) + b_ref[0]


def _adaln(c, w_ada, b_ada):
    nb, d = c.shape
    nl, _, n = w_ada.shape
    rows = -(-nb // SUBLANES) * SUBLANES
    c_pad = jnp.pad(c, ((0, rows - nb), (0, 0)))
    out = pl.pallas_call(
        _adaln_kernel,
        grid=(nl, n // ADA_TN),
        in_specs=[
            pl.BlockSpec((rows, d), lambda l, j: (0, 0)),
            pl.BlockSpec((1, d, ADA_TN), lambda l, j: (l, 0, j)),
            pl.BlockSpec((1, 1, ADA_TN), lambda l, j: (l, 0, j)),
        ],
        out_specs=pl.BlockSpec((1, rows, ADA_TN), lambda l, j: (l, 0, j)),
        out_shape=jax.ShapeDtypeStruct((nl, rows, n), F32),
        compiler_params=_params("parallel", "parallel"),
        name="adaln",
    )(c_pad, w_ada, b_ada.reshape(nl, 1, n))
    return out[:, :nb]


def _proj_in_kernel(x_ref, sc_ref, sh_ref, gain_ref, w_ref, o_ref, h_ref):
    @pl.when(pl.program_id(2) == 0)
    def _():
        h = _rms(x_ref[0], gain_ref[...]) * (1.0 + sc_ref[0]) + sh_ref[0]
        h_ref[...] = h.astype(BF16)

    o_ref[0] = _dot(h_ref[...], w_ref[...])


def _proj_in(x, sc, sh, gain, w, l):
    nb, s, d = x.shape
    n = w.shape[2]
    return pl.pallas_call(
        _proj_in_kernel,
        grid=(nb, s // PROJ_TM, n // PROJ_TN),
        in_specs=[
            pl.BlockSpec((1, PROJ_TM, d), lambda b, i, j: (b, i, 0)),
            pl.BlockSpec((1, 1, d), lambda b, i, j: (b, 0, 0)),
            pl.BlockSpec((1, 1, d), lambda b, i, j: (b, 0, 0)),
            pl.BlockSpec((1, d), lambda b, i, j: (0, 0)),
            pl.BlockSpec((None, d, PROJ_TN), lambda b, i, j: (l, 0, j)),
        ],
        out_specs=pl.BlockSpec((1, PROJ_TM, PROJ_TN), lambda b, i, j: (b, i, j)),
        out_shape=jax.ShapeDtypeStruct((nb, s, n), F32),
        scratch_shapes=[pltpu.VMEM((PROJ_TM, d), BF16)],
        compiler_params=_params("parallel", "parallel", "arbitrary"),
        name="proj_in",
    )(x, sc, sh, gain, w)


def _s5_kernel(u_ref, bm_ref, cm_ref, lam_ref, lam_chunk_ref, d_ref, wglu_ref, o_ref,
               sre, sim, hre, him, yloc):
    ph = pl.program_id(1)
    s = pl.program_id(2)
    kc = sre.shape[1]

    @pl.when(ph == 0)
    def _():
        @pl.when(s == 0)
        def _():
            sre[...] = jnp.zeros_like(sre)
            sim[...] = jnp.zeros_like(sim)

        u = u_ref[0]
        ub = u.astype(BF16)
        for h in range(2):
            cols = slice(h * S5_HALF, (h + 1) * S5_HALF)
            v = _dot(ub[:, cols], bm_ref[h])
            lr = lam_ref[h:h + 1, :]
            li = lam_ref[2 + h:3 + h, :]
            re = sre[h]
            im = sim[h]
            nre = lr * re - li * im + v[:, :S5_HSTATE]
            nim = lr * im + li * re + v[:, S5_HSTATE:]
            sre[h] = nre
            sim[h] = nim
            yl = _dot(nre.astype(BF16), cm_ref[h, 0]) + _dot(nim.astype(BF16), cm_ref[h, 1])
            yloc[s, :, cols] = yl + d_ref[:, cols] * u[:, cols]

        @pl.when(s == S5_LS - 1)
        def _():
            def body(k, carry):
                new = []
                for h in range(2):
                    cre, cim = carry[2 * h], carry[2 * h + 1]
                    hre[h, pl.ds(k, 1), :] = cre
                    him[h, pl.ds(k, 1), :] = cim
                    lr = lam_chunk_ref[h:h + 1, :]
                    li = lam_chunk_ref[2 + h:3 + h, :]
                    new.append(lr * cre - li * cim + sre[h, pl.ds(k, 1), :])
                    new.append(lr * cim + li * cre + sim[h, pl.ds(k, 1), :])
                return tuple(new)

            zero = jnp.zeros((1, S5_HSTATE), F32)
            lax.fori_loop(0, kc, body, (zero, zero, zero, zero))

    @pl.when(ph == 1)
    def _():
        ys = []
        for h in range(2):
            cols = slice(h * S5_HALF, (h + 1) * S5_HALF)
            lr = lam_ref[h:h + 1, :]
            li = lam_ref[2 + h:3 + h, :]
            pre = hre[h]
            pim = him[h]
            nre = lr * pre - li * pim
            nim = lr * pim + li * pre
            hre[h] = nre
            him[h] = nim
            yc = _dot(nre.astype(BF16), cm_ref[h, 0]) + _dot(nim.astype(BF16), cm_ref[h, 1])
            ys.append(yloc[s, :, cols] + yc)
        g = _gelu_tanh(jnp.concatenate(ys, axis=1))
        gate = _sigmoid(_dot(g.astype(BF16), wglu_ref[...]))
        o_ref[0] = (g * gate).astype(o_ref.dtype)


def _s5(proj, bm, cm, lam, lam_chunk, d, wglu):
    nb, s, _ = proj.shape
    kc = s // S5_LS
    pv = proj[..., :S5_WIDTH].reshape(nb, kc, S5_LS * S5_WIDTH)
    blks = 1
    out = pl.pallas_call(
        _s5_kernel,
        grid=(nb, 2, S5_LS),
        in_specs=[
            pl.BlockSpec((1, kc, S5_WIDTH),
                         lambda b, p, t: (b, 0, jnp.where(p == 0, t, S5_LS - 1) * blks)),
            pl.BlockSpec(bm.shape, lambda b, p, t: (0, 0, 0)),
            pl.BlockSpec(cm.shape, lambda b, p, t: (0, 0, 0, 0)),
            pl.BlockSpec(lam.shape, lambda b, p, t: (0, 0)),
            pl.BlockSpec(lam_chunk.shape, lambda b, p, t: (0, 0)),
            pl.BlockSpec(d.shape, lambda b, p, t: (0, 0)),
            pl.BlockSpec(wglu.shape, lambda b, p, t: (0, 0)),
        ],
        out_specs=pl.BlockSpec((1, kc, S5_WIDTH), lambda b, p, t: (b, 0, t * p)),
        out_shape=jax.ShapeDtypeStruct((nb, kc, S5_LS * S5_WIDTH), BF16),
        scratch_shapes=[pltpu.VMEM((2, kc, S5_HSTATE), F32)] * 4
        + [pltpu.VMEM((S5_LS, kc, S5_WIDTH), F32)],
        compiler_params=_params("parallel", "arbitrary", "arbitrary"),
        name="s5",
    )(pv, bm, cm, lam, lam_chunk, d, wglu)
    return out.reshape(nb, s, S5_WIDTH)


def _s5_prep(lam_re, lam_im, log_dt, b_re, b_im, c_re, c_im):
    lam_re, lam_im = lam_re.astype(F32), lam_im.astype(F32)
    dt = jnp.exp(log_dt.astype(F32))[:, None]
    mag = jnp.exp(lam_re * dt)
    bar_re, bar_im = mag * jnp.cos(lam_im * dt), mag * jnp.sin(lam_im * dt)
    den = lam_re * lam_re + lam_im * lam_im
    co_re = ((bar_re - 1.0) * lam_re + bar_im * lam_im) / den
    co_im = (bar_im * lam_re - (bar_re - 1.0) * lam_im) / den
    b_re, b_im = b_re.astype(F32), b_im.astype(F32)
    bb_re = co_re[..., None] * b_re - co_im[..., None] * b_im
    bb_im = co_re[..., None] * b_im + co_im[..., None] * b_re
    ch_re, ch_im = bar_re, bar_im
    for _ in range(int(math.log2(S5_LS))):
        ch_re, ch_im = ch_re * ch_re - ch_im * ch_im, 2.0 * ch_re * ch_im
    gh = S5_GROUPS // 2
    eye = jnp.eye(gh, dtype=F32)

    def b_tiles(part):
        t = part.reshape(2, gh, S5_STATE, S5_GROUP)
        return jnp.einsum('ab,hapc->hacbp', eye, t).reshape(2, S5_HALF, S5_HSTATE)

    def c_tiles(part):
        t = part.reshape(2, gh, S5_GROUP, S5_STATE)
        return jnp.einsum('ab,hbcp->hapbc', eye, t).reshape(2, S5_HSTATE, S5_HALF)

    bm = jnp.concatenate([b_tiles(bb_re), b_tiles(bb_im)], axis=-1).astype(BF16)
    cm = jnp.stack([c_tiles(c_re.astype(F32)), -c_tiles(c_im.astype(F32))], axis=1).astype(BF16)

    def rows(re, im):
        return jnp.concatenate([re.reshape(2, S5_HSTATE), im.reshape(2, S5_HSTATE)], 0)

    return bm, cm, rows(bar_re, bar_im), rows(ch_re, ch_im)


def _mla_prep_kernel(cq_ref, ckv_ref, qn_ref, kvn_ref, wq_ref, wkv_ref, cos_ref, s1_ref, s2_ref,
                     q_ref, k_ref, v_ref):
    cos_t = cos_ref[0]
    s1 = s1_ref[0]
    s2 = s2_ref[0]
    half = MLA_ROPE // 2

    def rope(x):
        return x * cos_t + pltpu.roll(x, half, 1) * s1 + pltpu.roll(x, LANES - half, 1) * s2

    scale = (MLA_NOPE + MLA_ROPE) ** -0.5 * math.log2(math.e)
    q = _dot(_rms(cq_ref[0], qn_ref[...]).astype(BF16), wq_ref[...])
    blk = ckv_ref[0]
    kv = _dot(_rms(blk[:, :MLA_KV_RANK], kvn_ref[...]).astype(BF16), wkv_ref[...])
    kr = rope(blk[:, MLA_KV_RANK:MLA_KV_RANK + LANES]).astype(BF16)
    for h in range(MLA_HEADS):
        qn = q[:, h * MLA_QK:h * MLA_QK + MLA_NOPE] * scale
        qr = rope(q[:, h * MLA_QK + MLA_NOPE:(h + 1) * MLA_QK]) * scale
        q_ref[0, h] = jnp.concatenate([qn, qr], axis=1).astype(BF16)
        k_ref[0, h] = jnp.concatenate([kv[:, h * MLA_NOPE:(h + 1) * MLA_NOPE].astype(BF16), kr], axis=1)
        vh = kv[:, MLA_WIDTH + h * MLA_V:MLA_WIDTH + (h + 1) * MLA_V]
        v_ref[0, h] = jnp.concatenate([vh, jnp.ones_like(vh)], axis=1).astype(BF16)


def _mla_prep(proj, qn, kvn, wq, wkv, cos_t, s1_t, s2_t):
    nb, s, _ = proj.shape
    tab = pl.BlockSpec((1, PREP_TM, LANES), lambda b, i: (b, i, 0))
    return pl.pallas_call(
        _mla_prep_kernel,
        grid=(nb, s // PREP_TM),
        in_specs=[
            pl.BlockSpec((1, PREP_TM, COL_BLK), lambda b, i: (b, i, 1)),
            pl.BlockSpec((1, PREP_TM, COL_BLK), lambda b, i: (b, i, 2)),
            pl.BlockSpec(qn.shape, lambda b, i: (0, 0)),
            pl.BlockSpec(kvn.shape, lambda b, i: (0, 0)),
            pl.BlockSpec(wq.shape, lambda b, i: (0, 0)),
            pl.BlockSpec(wkv.shape, lambda b, i: (0, 0)),
            tab, tab, tab,
        ],
        out_specs=[
            pl.BlockSpec((1, MLA_HEADS, PREP_TM, MLA_QK), lambda b, i: (b, 0, i, 0)),
            pl.BlockSpec((1, MLA_HEADS, PREP_TM, MLA_QK), lambda b, i: (b, 0, i, 0)),
            pl.BlockSpec((1, MLA_HEADS, PREP_TM, 2 * MLA_V), lambda b, i: (b, 0, i, 0)),
        ],
        out_shape=[
            jax.ShapeDtypeStruct((nb, MLA_HEADS, s, MLA_QK), BF16),
            jax.ShapeDtypeStruct((nb, MLA_HEADS, s, MLA_QK), BF16),
            jax.ShapeDtypeStruct((nb, MLA_HEADS, s, 2 * MLA_V), BF16),
        ],
        compiler_params=_params("parallel", "parallel"),
        name="mla_prep",
    )(proj, proj, qn, kvn, wq, wkv, cos_t, s1_t, s2_t)


def _flash_kernel(q_ref, k_ref, v_ref, o_ref, s_ref, p_ref, m_ref, a_ref, acc_ref):
    qi = pl.program_id(2)
    t = FLASH_T
    heads = range(FLASH_HEADS)
    nh = FLASH_HEADS
    m_ref[...] = jnp.full(m_ref.shape, -jnp.inf, F32)
    acc_ref[...] = jnp.zeros_like(acc_ref)
    p_ref[nh:] = jnp.zeros((nh,) + p_ref.shape[1:], BF16)
    a_ref[nh:] = jnp.ones((nh,) + a_ref.shape[1:], F32)

    def scores(j, slot):
        start = pl.multiple_of(j * t, t)
        for h in heads:
            s_ref[slot * nh + h] = _dot_nt(q_ref[0, h], k_ref[0, h, pl.ds(start, t), :])

    def softmax(slot, diagonal):
        for h in heads:
            for r0 in range(0, t, FLASH_ROWS):
                rows = slice(r0, r0 + FLASH_ROWS)
                sc = s_ref[slot * nh + h, rows, :]
                if diagonal:
                    row = lax.broadcasted_iota(jnp.int32, sc.shape, 0) + r0
                    col = lax.broadcasted_iota(jnp.int32, sc.shape, 1)
                    sc = jnp.where(col <= row, sc, MASK_VALUE)
                m_old = m_ref[h, rows, :]
                m_new = jnp.maximum(m_old, jnp.max(sc, axis=-1, keepdims=True))
                p_ref[slot * nh + h, rows, :] = jnp.exp2(sc - m_new[:, :1]).astype(BF16)
                a_ref[slot * nh + h, rows, :] = jnp.exp2(m_old - m_new)
                m_ref[h, rows, :] = m_new

    def accumulate(j, slot):
        start = pl.multiple_of(j * t, t)
        for h in heads:
            pv = _dot(p_ref[slot * nh + h], v_ref[0, h, pl.ds(start, t), :])
            alpha = a_ref[slot * nh + h]
            acc_ref[h] = jnp.concatenate([alpha, alpha], axis=1) * acc_ref[h] + pv

    def stage(j, slot, with_accumulate=True):
        softmax(slot, False)
        scores(j + 1, 1 - slot)
        if with_accumulate:
            accumulate(jnp.maximum(j - 1, 0), 1 - slot)

    odd = lax.rem(qi, 2)
    for slot in range(2):
        @pl.when(odd == slot)
        def _():
            scores(0, slot)

    @pl.when(odd == 1)
    def _():
        stage(0, 1, with_accumulate=False)

    def body(jj, carry):
        j = 2 * jj + odd
        stage(j, 0)
        stage(j + 1, 1)
        return carry

    lax.fori_loop(0, qi // 2, body, 0)
    softmax(0, True)
    accumulate(jnp.maximum(qi - 1, 0), 1)
    accumulate(qi, 0)

    for h in heads:
        acc = acc_ref[h]
        o_ref[0, :, h * MLA_V:(h + 1) * MLA_V] = (acc[:, :MLA_V] / acc[:, MLA_V:]).astype(o_ref.dtype)


def _flash(q, k, v):
    nb, nh, s, _ = q.shape
    hb = FLASH_HEADS
    return pl.pallas_call(
        _flash_kernel,
        grid=(nb, nh // hb, s // FLASH_T),
        in_specs=[
            pl.BlockSpec((1, hb, FLASH_T, MLA_QK), lambda b, h, i: (b, h, i, 0)),
            pl.BlockSpec((1, hb, s, MLA_QK), lambda b, h, i: (b, h, 0, 0)),
            pl.BlockSpec((1, hb, s, 2 * MLA_V), lambda b, h, i: (b, h, 0, 0)),
        ],
        out_specs=pl.BlockSpec((1, FLASH_T, hb * MLA_V), lambda b, h, i: (b, i, h)),
        out_shape=jax.ShapeDtypeStruct((nb, s, nh * MLA_V), BF16),
        scratch_shapes=[pltpu.VMEM((2 * hb, FLASH_T, FLASH_T), F32), pltpu.VMEM((2 * hb, FLASH_T, FLASH_T), BF16),
                        pltpu.VMEM((hb, FLASH_T, LANES), F32), pltpu.VMEM((2 * hb, FLASH_T, LANES), F32),
                        pltpu.VMEM((hb, FLASH_T, 2 * MLA_V), F32)],
        compiler_params=_params("parallel", "parallel", "arbitrary"),
        name="flash",
    )(q, k, v)


def _cumsum_rows(x, ltri):
    hi = x.astype(BF16)
    r1 = x - hi.astype(F32)
    mid = r1.astype(BF16)
    lo = (r1 - mid.astype(F32)).astype(BF16)
    return _dot(ltri, hi) + _dot(ltri, mid) + _dot(ltri, lo)


def _hgrn_kernel(q_ref, f_ref, i_ref, g_ref, lb_ref, on_ref, ones_ref, ltri_ref, o_ref, st_ref):
    @pl.when(pl.program_id(1) == 0)
    def _():
        st_ref[...] = jnp.zeros_like(st_ref)

    for n in range(q_ref.shape[0]):
        _hgrn_chunk(n, q_ref, f_ref, i_ref, g_ref, lb_ref, on_ref, ones_ref, ltri_ref, o_ref, st_ref)


def _hgrn_chunk(n, q_ref, f_ref, i_ref, g_ref, lb_ref, on_ref, ones_ref, ltri_ref, o_ref, st_ref):
    c, w = HG_CHUNK, HG_WIDTH
    z = f_ref[n]
    lb = lb_ref[...]
    e = jnp.exp(-jnp.abs(z))
    r = 1.0 / (1.0 + e)
    pos = z >= 0.0
    sig_z = jnp.where(pos, r, e * r)
    sig_mz = jnp.where(pos, e * r, r)
    kk = (1.0 - lb) * sig_mz
    q = _silu(q_ref[n])
    v = i_ref[n]
    gate = _silu(g_ref[n])
    f = lb + (1.0 - lb) * sig_z
    b = _cumsum_rows(jnp.log2(f), ltri_ref[...])
    row = lax.broadcasted_iota(jnp.int32, (c, w), 0)

    q_lvls, k_lvls = [], []
    for m in HG_LEVELS:
        half = m // 2
        b4 = b.reshape(c // m, 2, half, w)
        ref_b = b4[:, 0, half - 1:half, :]
        k_first = kk.reshape(c // m, 2, half, w)[:, 0] * jnp.exp2(ref_b - b4[:, 0])
        q_second = q.reshape(c // m, 2, half, w)[:, 1] * jnp.exp2(b4[:, 1] - ref_b)
        zero = jnp.zeros_like(k_first)
        q_lvls.append(jnp.stack([zero, q_second], axis=1).reshape(c, w).astype(BF16))
        k_lvls.append(jnp.stack([k_first, zero], axis=1).reshape(c, w).astype(BF16))

    pos8 = row & (SUBLANES - 1)

    def back1(x):
        x3 = x.reshape(c // SUBLANES, SUBLANES, w)
        return pltpu.roll(x3, 1, 1).reshape(c, w)

    f_in = jnp.where(pos8 >= 1, f, 0.0)
    kd_dl, vr = kk, v
    o_diag = jnp.zeros((c, w), F32)
    for dl in range(SUBLANES):
        if dl > 0:
            kd_dl = f_in * back1(kd_dl)
            vr = back1(vr)
        wb = (q * kd_dl).astype(BF16)
        a = jnp.concatenate([_dot(wb[:, :MXU_DIM], ones_ref[...]), _dot(wb[:, MXU_DIM:], ones_ref[...])], axis=1)
        o_diag = o_diag + a * vr

    tt = lax.broadcasted_iota(jnp.int32, (c, c), 0)
    ss = lax.broadcasted_iota(jnp.int32, (c, c), 1)
    same_blk = {m: (tt // m) == (ss // m) for m in HG_LEVELS}

    b_last = b[c - 1:c, :]
    qe = (q * jnp.exp2(b)).astype(BF16)
    kd = (kk * jnp.exp2(b_last - b)).astype(BF16)
    decay = jnp.exp2(b_last)
    vb = v.astype(BF16)
    for h in range(HG_HEADS):
        hs = slice(h * HG_DK, (h + 1) * HG_DK)
        st = st_ref[n, h]
        attn = jnp.zeros((c, c), F32)
        for m, ql, kl in zip(HG_LEVELS, q_lvls, k_lvls):
            part = _dot_nt(ql[:, hs], kl[:, hs])
            attn = attn + (part if m == c else jnp.where(same_blk[m], part, 0.0))
        o_h =_dot_nt(qe[:, hs], st.astype(BF16)) + _dot(attn.astype(BF16), vb[:, hs]) + o_diag[:, hs]
        st_ref[n, h] = st * decay[:, hs] + _dot_tn(vb[:, hs], kd[:, hs])
        o_ref[n, :, hs] = (_rms(o_h, on_ref[...]) * gate[:, hs]).astype(o_ref.dtype)


def _hgrn(proj, lb, onorm):
    nb, s, _ = proj.shape
    head_id = jnp.arange(MXU_DIM) // HG_DK
    ones_blk = (head_id[:, None] == head_id[None, :]).astype(BF16)
    t_id = jnp.arange(HG_CHUNK)
    ltri = (t_id[:, None] >= t_id[None, :]).astype(BF16)
    ns = HG_SEQS if nb % HG_SEQS == 0 else 1
    spec = lambda cb: pl.BlockSpec((ns, HG_CHUNK, COL_BLK), lambda b, i: (b, i, cb))
    return pl.pallas_call(
        _hgrn_kernel,
        grid=(nb // ns, s // HG_CHUNK),
        in_specs=[spec(3), spec(4), spec(5), spec(6),
                  pl.BlockSpec(lb.shape, lambda b, i: (0, 0)),
                  pl.BlockSpec(onorm.shape, lambda b, i: (0, 0)),
                  pl.BlockSpec(ones_blk.shape, lambda b, i: (0, 0)),
                  pl.BlockSpec(ltri.shape, lambda b, i: (0, 0))],
        out_specs=pl.BlockSpec((ns, HG_CHUNK, HG_WIDTH), lambda b, i: (b, i, 0)),
        out_shape=jax.ShapeDtypeStruct((nb, s, HG_WIDTH), BF16),
        scratch_shapes=[pltpu.VMEM((ns, HG_HEADS, HG_DV, HG_DK), F32)],
        compiler_params=_params("parallel", "arbitrary"),
        name="hgrn",
    )(proj, proj, proj, proj, lb, onorm, ones_blk, ltri)


def _out_proj_kernel(ys_ref, ym_ref, yh_ref, w_ref, x_ref, gain_ref, g_ref, o_ref):
    mixed = (_dot(ys_ref[0], w_ref[:S5_WIDTH, :])
             + _dot(ym_ref[0], w_ref[S5_WIDTH:S5_WIDTH + MLA_WIDTH, :])
             + _dot(yh_ref[0], w_ref[S5_WIDTH + MLA_WIDTH:, :]))
    o_ref[0] = x_ref[0] + g_ref[0] * _rms(mixed, gain_ref[...])


def _out_proj(y_s5, y_mla, y_hg, w, l, x, gain, g):
    nb, s, d = x.shape
    row = lambda n: pl.BlockSpec((1, OUT_TM, n), lambda b, i: (b, i, 0))
    return pl.pallas_call(
        _out_proj_kernel,
        grid=(nb, s // OUT_TM),
        in_specs=[row(S5_WIDTH), row(MLA_WIDTH), row(HG_WIDTH),
                  pl.BlockSpec((None,) + w.shape[1:], lambda b, i: (l, 0, 0)),
                  row(d),
                  pl.BlockSpec((1, d), lambda b, i: (0, 0)),
                  pl.BlockSpec((1, 1, d), lambda b, i: (b, 0, 0))],
        out_specs=row(d),
        out_shape=jax.ShapeDtypeStruct((nb, s, d), F32),
        compiler_params=_params("parallel", "parallel"),
        name="out_proj",
    )(y_s5, y_mla, y_hg, w, x, gain, g)


def _ffn_kernel(x_ref, xp_ref, sc_ref, sh_ref, g_ref, pre_ref, post_ref,
                wg_ref, wv_ref, cwg_ref, cbg_ref, cwv_ref, cbv_ref, wd_ref, o_ref,
                h_ref, acc_ref, ug_ref, uv_ref, act_ref):
    i = pl.program_id(1)
    j = pl.program_id(2)

    @pl.when(j == 0)
    def _():
        def norm(x):
            return _rms(x, pre_ref[...]) * (1.0 + sc_ref[0]) + sh_ref[0]

        hp = jnp.where(i == 0, 0.0, norm(xp_ref[0]))
        h_ref[:HALO, :] = hp.astype(BF16)
        h_ref[HALO:, :] = norm(x_ref[0]).astype(BF16)
        acc_ref[...] = jnp.zeros_like(acc_ref)

    ug_ref[...] = _dot(h_ref[...], wg_ref[...])
    uv_ref[...] = _dot(h_ref[...], wv_ref[...])

    def conv(u_ref, cw_ref, cb_ref, r0):
        return (cb_ref[...] + u_ref[r0 + HALO - 2:r0 + HALO - 2 + FFN_ROWS, :] * cw_ref[0:1, :]
                + u_ref[r0 + HALO - 1:r0 + HALO - 1 + FFN_ROWS, :] * cw_ref[1:2, :]
                + u_ref[r0 + HALO:r0 + HALO + FFN_ROWS, :] * cw_ref[2:3, :])

    for r0 in range(0, FFN_TM, FFN_ROWS):
        gate = conv(ug_ref, cwg_ref, cbg_ref, r0)
        val = conv(uv_ref, cwv_ref, cbv_ref, r0)
        act_ref[r0:r0 + FFN_ROWS, :] = (_gelu_tanh(gate) * val).astype(BF16)
    acc_ref[...] += _dot(act_ref[...], wd_ref[...])

    @pl.when(j == pl.num_programs(2) - 1)
    def _():
        o_ref[0] = x_ref[0] + g_ref[0] * _rms(acc_ref[...], post_ref[...])


def _ffn(x, sc, sh, g, pre, post, w_up, conv_w, conv_b, w_down, l):
    nb, s, d = x.shape
    ff = w_down.shape[1]
    nf = ff // FFN_TF
    halo_blocks = FFN_TM // HALO
    vec = pl.BlockSpec((1, 1, d), lambda b, i, j: (b, 0, 0))
    par = pl.BlockSpec((1, d), lambda b, i, j: (0, 0))
    gate = lambda r: pl.BlockSpec((None, r, FFN_TF), lambda b, i, j: (l, 0, j))
    val = lambda r: pl.BlockSpec((None, r, FFN_TF), lambda b, i, j: (l, 0, j + nf))
    return pl.pallas_call(
        _ffn_kernel,
        grid=(nb, s // FFN_TM, nf),
        in_specs=[
            pl.BlockSpec((1, FFN_TM, d), lambda b, i, j: (b, i, 0)),
            pl.BlockSpec((1, HALO, d), lambda b, i, j: (b, jnp.maximum(i * halo_blocks - 1, 0), 0)),
            vec, vec, vec, par, par,
            gate(d), val(d), gate(CONV_W), gate(1), val(CONV_W), val(1),
            pl.BlockSpec((None, FFN_TF, d), lambda b, i, j: (l, j, 0)),
        ],
        out_specs=pl.BlockSpec((1, FFN_TM, d), lambda b, i, j: (b, i, 0)),
        out_shape=jax.ShapeDtypeStruct((nb, s, d), F32),
        scratch_shapes=[pltpu.VMEM((FFN_TM + HALO, d), BF16), pltpu.VMEM((FFN_TM, d), F32),
                        pltpu.VMEM((FFN_TM + HALO, FFN_TF), F32), pltpu.VMEM((FFN_TM + HALO, FFN_TF), F32),
                        pltpu.VMEM((FFN_TM, FFN_TF), BF16)],
        compiler_params=_params("parallel", "parallel", "arbitrary"),
        name="ffn",
    )(x, x, sc, sh, g, pre, post, w_up, w_up, conv_w, conv_b, conv_w, conv_b, w_down)


def _pad_cols(a, n):
    return jnp.pad(a, [(0, 0)] * (a.ndim - 1) + [(0, n - a.shape[-1])])


def _layout_w_in(w_in):
    o = 0
    parts = []
    for width in (S5_WIDTH, MLA_Q_RANK, MLA_KV_RANK, MLA_ROPE):
        parts.append(w_in[..., o:o + width])
        o += width
    parts.append(jnp.zeros(w_in.shape[:-1] + (COL_BLK - MLA_KV_RANK - MLA_ROPE,), w_in.dtype))
    parts.append(w_in[..., o:])
    return jnp.concatenate(parts, axis=-1).astype(BF16)


def _layout_w_uq(w):
    nl, r, _ = w.shape
    w = w.reshape(nl, r, MLA_HEADS, MLA_NOPE + MLA_ROPE)
    return _pad_cols(w, MLA_QK).reshape(nl, r, MLA_HEADS * MLA_QK).astype(BF16)


def _layout_w_ukv(w):
    nl, r, _ = w.shape
    w = w.reshape(nl, r, MLA_HEADS, MLA_NOPE + MLA_V)
    k = w[..., :MLA_NOPE].reshape(nl, r, MLA_HEADS * MLA_NOPE)
    v = w[..., MLA_NOPE:].reshape(nl, r, MLA_HEADS * MLA_V)
    return jnp.concatenate([k, v], axis=-1).astype(BF16)


def _rope_tables(positions):
    inv_freq = 1.0 / (ROPE_THETA ** (jnp.arange(0, MLA_ROPE, 2, dtype=F32) / MLA_ROPE))
    ang = positions.astype(F32)[..., None] * inv_freq
    cos, sin = jnp.cos(ang), jnp.sin(ang)
    z32 = jnp.zeros_like(cos)
    z64 = jnp.concatenate([z32, z32], axis=-1)
    return (jnp.concatenate([cos, cos, z64], axis=-1),
            jnp.concatenate([z32, sin, z64], axis=-1),
            jnp.concatenate([-sin, z32, z64], axis=-1))


def kernel(x, c, positions, w_in, s5_lambda_re, s5_lambda_im, s5_log_dt, s5_b_re, s5_b_im, s5_c_re, s5_c_im, s5_d, s5_w_glu, mla_q_norm, mla_w_uq, mla_kv_norm, mla_w_ukv, hg_lb_logits, hg_out_norm, w_out, mix_pre_norm, mix_post_norm, ffn_pre_norm, ffn_post_norm, ffn_w_up, ffn_conv_w, ffn_conv_b, ffn_w_down, w_ada, b_ada):
    nb, s, d = x.shape
    nl = w_in.shape[0]
    assert s % FLASH_T == 0 and s % FFN_TM == 0 and s % (S5_LS * SUBLANES) == 0

    mod = _adaln(c, w_ada, b_ada)
    cos_t, s1_t, s2_t = _rope_tables(positions)
    probs = jax.nn.softmax(hg_lb_logits.astype(F32), axis=0)
    lower_bounds = jnp.cumsum(probs, axis=0) - probs[0:1]

    w_in_p = _layout_w_in(w_in)
    w_uq_p = _layout_w_uq(mla_w_uq)
    w_ukv_p = _layout_w_ukv(mla_w_ukv)
    w_out_b = w_out.astype(BF16)
    w_glu_b = s5_w_glu.astype(BF16)
    def pad_halves(a):
        halves = a.reshape(a.shape[:-1] + (2, D_FF))
        return _pad_cols(halves, D_FF_PAD).reshape(a.shape[:-1] + (2 * D_FF_PAD,))

    w_up_p = pad_halves(ffn_w_up).astype(BF16)
    conv_w_p = pad_halves(ffn_conv_w)
    conv_b_p = pad_halves(ffn_conv_b)[:, None, :]
    w_down = jnp.pad(ffn_w_down, ((0, 0), (0, D_FF_PAD - D_FF), (0, 0))).astype(BF16)

    for l in range(nl):
        sh1, sc1, g1, sh2, sc2, g2 = [m[:, None, :] for m in jnp.split(mod[l], 6, axis=-1)]
        proj = _proj_in(x, sc1, sh1, mix_pre_norm[l][None, :], w_in_p, l)
        bm, cm, lam, lam_chunk = _s5_prep(s5_lambda_re[l], s5_lambda_im[l], s5_log_dt[l],
                                          s5_b_re[l], s5_b_im[l], s5_c_re[l], s5_c_im[l])
        y_s5 = _s5(proj, bm, cm, lam, lam_chunk, s5_d[l][None, :], w_glu_b[l])
        q, k, v = _mla_prep(proj, mla_q_norm[l][None, :], mla_kv_norm[l][None, :], w_uq_p[l], w_ukv_p[l],
                            cos_t, s1_t, s2_t)
        y_mla = _flash(q, k, v)
        y_hg = _hgrn(proj, lower_bounds[l][None, :], hg_out_norm[l][None, :])
        x = _out_proj(y_s5, y_mla, y_hg, w_out_b, l, x, mix_post_norm[l][None, :], g1)
        x = _ffn(x, sc2, sh2, g2, ffn_pre_norm[l][None, :], ffn_post_norm[l][None, :],
                 w_up_p, conv_w_p, conv_b_p, w_down, l)
    return x
```

```python
import functools
import math

import jax
import jax.numpy as jnp
from jax import lax
from jax.experimental import pallas as pl
from jax.experimental.pallas import tpu as pltpu

F32 = jnp.float32
BF16 = jnp.bfloat16

D_MODEL = 2048
DEPTH = 4
S5_WIDTH = 512
S5_GROUP = 16
S5_GROUPS = 32
S5_STATE = 64
MLA_HEADS = 8
MLA_NOPE = 128
MLA_ROPE = 64
MLA_V = 128
MLA_Q_RANK = 512
MLA_KV_RANK = 256
MLA_WIDTH = MLA_HEADS * MLA_V
ROPE_THETA = 10000.0
MASK_VALUE = -1e30
HG_HEADS = 4
HG_DK = 128
HG_DV = 128
HG_WIDTH = HG_HEADS * HG_DV
D_FF = 5504
CONV_W = 3
EPS = 1e-6

LANES = 128
SUBLANES = 8
MXU_DIM = 256
VMEM_LIMIT_BYTES = 56 * 1024 * 1024

COL_BLK = 512
N_COL_BLKS = 7
D_IN_PAD = N_COL_BLKS * COL_BLK
MLA_QK = 2 * LANES
D_FF_PAD = 5632

ADA_TN = 1024
PROJ_TM = 512
PROJ_TN = D_IN_PAD // 2
S5_LS = 16
S5_HALF = S5_WIDTH // 2
S5_HSTATE = (S5_GROUPS // 2) * S5_STATE
PREP_TM = 512
FLASH_T = 512
FLASH_HEADS = 2
FLASH_ROWS = 32
HG_CHUNK = 128
HG_LEVELS = (16, 32, 64, 128)
HG_SEQS = 2
OUT_TM = 512
FFN_TM = 512
FFN_TF = 512
FFN_ROWS = 64
NORM_ROWS = 16
HALO = SUBLANES


def _sigmoid(x):
    return 1.0 / (1.0 + jnp.exp(-x))


def _silu(x):
    return x * _sigmoid(x)


def _gelu_tanh(x):
    c = math.sqrt(2.0 / math.pi)
    return x * (0.5 * (1.0 + jnp.tanh(c * (x + 0.044715 * (x * x * x)))))


def _rms(x, gain):
    return x * lax.rsqrt(jnp.mean(x * x, axis=-1, keepdims=True) + EPS) * gain


def _dot(a, b):
    return jnp.dot(a, b, preferred_element_type=F32)


def _dot_nt(a, b):
    return lax.dot_general(a, b, (((1,), (1,)), ((), ())), preferred_element_type=F32)


def _dot_tn(a, b):
    return lax.dot_general(a, b, (((0,), (0,)), ((), ())), preferred_element_type=F32)


def _params(*sem):
    return pltpu.CompilerParams(dimension_semantics=sem, vmem_limit_bytes=VMEM_LIMIT_BYTES)


def _adaln_kernel(c_ref, w_ref, b_ref, o_ref):
    ca = _silu(c_ref[...]).astype(BF16)
    o_ref[0] = _dot(ca, w_ref[0].astype(BF16)) + b_ref[0]


def _adaln(c, w_ada, b_ada):
    nb, d = c.shape
    nl, _, n = w_ada.shape
    rows = -(-nb // SUBLANES) * SUBLANES
    c_pad = jnp.pad(c, ((0, rows - nb), (0, 0)))
    out = pl.pallas_call(
        _adaln_kernel,
        grid=(nl, n // ADA_TN),
        in_specs=[
            pl.BlockSpec((rows, d), lambda l, j: (0, 0)),
            pl.BlockSpec((1, d, ADA_TN), lambda l, j: (l, 0, j)),
            pl.BlockSpec((1, 1, ADA_TN), lambda l, j: (l, 0, j)),
        ],
        out_specs=pl.BlockSpec((1, rows, ADA_TN), lambda l, j: (l, 0, j)),
        out_shape=jax.ShapeDtypeStruct((nl, rows, n), F32),
        compiler_params=_params("parallel", "parallel"),
        name="adaln",
    )(c_pad, w_ada, b_ada.reshape(nl, 1, n))
    return out[:, :nb]


def _proj_in_kernel(x_ref, sc_ref, sh_ref, gain_ref, w_ref, o_ref, h_ref):
    @pl.when(pl.program_id(2) == 0)
    def _():
        h = _rms(x_ref[0], gain_ref[...]) * (1.0 + sc_ref[0]) + sh_ref[0]
        h_ref[...] = h.astype(BF16)

    o_ref[0] = _dot(h_ref[...], w_ref[...])


def _proj_in(x, sc, sh, gain, w, l):
    nb, s, d = x.shape
    n = w.shape[2]
    return pl.pallas_call(
        _proj_in_kernel,
        grid=(nb, s // PROJ_TM, n // PROJ_TN),
        in_specs=[
            pl.BlockSpec((1, PROJ_TM, d), lambda b, i, j: (b, i, 0)),
            pl.BlockSpec((1, 1, d), lambda b, i, j: (b, 0, 0)),
            pl.BlockSpec((1, 1, d), lambda b, i, j: (b, 0, 0)),
            pl.BlockSpec((1, d), lambda b, i, j: (0, 0)),
            pl.BlockSpec((None, d, PROJ_TN), lambda b, i, j: (l, 0, j)),
        ],
        out_specs=pl.BlockSpec((1, PROJ_TM, PROJ_TN), lambda b, i, j: (b, i, j)),
        out_shape=jax.ShapeDtypeStruct((nb, s, n), F32),
        scratch_shapes=[pltpu.VMEM((PROJ_TM, d), BF16)],
        compiler_params=_params("parallel", "parallel", "arbitrary"),
        name="proj_in",
    )(x, sc, sh, gain, w)


def _s5_kernel(u_ref, bm_ref, cm_ref, lam_ref, lam_chunk_ref, d_ref, wglu_ref, o_ref,
               sre, sim, hre, him, yloc):
    ph = pl.program_id(1)
    s = pl.program_id(2)
    kc = sre.shape[1]

    @pl.when(ph == 0)
    def _():
        @pl.when(s == 0)
        def _():
            sre[...] = jnp.zeros_like(sre)
            sim[...] = jnp.zeros_like(sim)

        u = u_ref[0]
        ub = u.astype(BF16)
        for h in range(2):
            cols = slice(h * S5_HALF, (h + 1) * S5_HALF)
            v = _dot(ub[:, cols], bm_ref[h])
            lr = lam_ref[h:h + 1, :]
            li = lam_ref[2 + h:3 + h, :]
            re = sre[h]
            im = sim[h]
            nre = lr * re - li * im + v[:, :S5_HSTATE]
            nim = lr * im + li * re + v[:, S5_HSTATE:]
            sre[h] = nre
            sim[h] = nim
            yl = _dot(nre.astype(BF16), cm_ref[h, 0]) + _dot(nim.astype(BF16), cm_ref[h, 1])
            yloc[s, :, cols] = yl + d_ref[:, cols] * u[:, cols]

        @pl.when(s == S5_LS - 1)
        def _():
            def body(k, carry):
                new = []
                for h in range(2):
                    cre, cim = carry[2 * h], carry[2 * h + 1]
                    hre[h, pl.ds(k, 1), :] = cre
                    him[h, pl.ds(k, 1), :] = cim
                    lr = lam_chunk_ref[h:h + 1, :]
                    li = lam_chunk_ref[2 + h:3 + h, :]
                    new.append(lr * cre - li * cim + sre[h, pl.ds(k, 1), :])
                    new.append(lr * cim + li * cre + sim[h, pl.ds(k, 1), :])
                return tuple(new)

            zero = jnp.zeros((1, S5_HSTATE), F32)
            lax.fori_loop(0, kc, body, (zero, zero, zero, zero))

    @pl.when(ph == 1)
    def _():
        ys = []
        for h in range(2):
            cols = slice(h * S5_HALF, (h + 1) * S5_HALF)
            lr = lam_ref[h:h + 1, :]
            li = lam_ref[2 + h:3 + h, :]
            pre = hre[h]
            pim = him[h]
            nre = lr * pre - li * pim
            nim = lr * pim + li * pre
            hre[h] = nre
            him[h] = nim
            yc = _dot(nre.astype(BF16), cm_ref[h, 0]) + _dot(nim.astype(BF16), cm_ref[h, 1])
            ys.append(yloc[s, :, cols] + yc)
        g = _gelu_tanh(jnp.concatenate(ys, axis=1))
        gate = _sigmoid(_dot(g.astype(BF16), wglu_ref[...]))
        o_ref[0] = (g * gate).astype(o_ref.dtype)


def _s5(proj, bm, cm, lam, lam_chunk, d, wglu):
    nb, s, _ = proj.shape
    kc = s // S5_LS
    pv = proj[..., :S5_WIDTH].reshape(nb, kc, S5_LS * S5_WIDTH)
    blks = 1
    out = pl.pallas_call(
        _s5_kernel,
        grid=(nb, 2, S5_LS),
        in_specs=[
            pl.BlockSpec((1, kc, S5_WIDTH),
                         lambda b, p, t: (b, 0, jnp.where(p == 0, t, S5_LS - 1) * blks)),
            pl.BlockSpec(bm.shape, lambda b, p, t: (0, 0, 0)),
            pl.BlockSpec(cm.shape, lambda b, p, t: (0, 0, 0, 0)),
            pl.BlockSpec(lam.shape, lambda b, p, t: (0, 0)),
            pl.BlockSpec(lam_chunk.shape, lambda b, p, t: (0, 0)),
            pl.BlockSpec(d.shape, lambda b, p, t: (0, 0)),
            pl.BlockSpec(wglu.shape, lambda b, p, t: (0, 0)),
        ],
        out_specs=pl.BlockSpec((1, kc, S5_WIDTH), lambda b, p, t: (b, 0, t * p)),
        out_shape=jax.ShapeDtypeStruct((nb, kc, S5_LS * S5_WIDTH), BF16),
        scratch_shapes=[pltpu.VMEM((2, kc, S5_HSTATE), F32)] * 4
        + [pltpu.VMEM((S5_LS, kc, S5_WIDTH), F32)],
        compiler_params=_params("parallel", "arbitrary", "arbitrary"),
        name="s5",
    )(pv, bm, cm, lam, lam_chunk, d, wglu)
    return out.reshape(nb, s, S5_WIDTH)


def _s5_prep(lam_re, lam_im, log_dt, b_re, b_im, c_re, c_im):
    lam_re, lam_im = lam_re.astype(F32), lam_im.astype(F32)
    dt = jnp.exp(log_dt.astype(F32))[:, None]
    mag = jnp.exp(lam_re * dt)
    bar_re, bar_im = mag * jnp.cos(lam_im * dt), mag * jnp.sin(lam_im * dt)
    den = lam_re * lam_re + lam_im * lam_im
    co_re = ((bar_re - 1.0) * lam_re + bar_im * lam_im) / den
    co_im = (bar_im * lam_re - (bar_re - 1.0) * lam_im) / den
    b_re, b_im = b_re.astype(F32), b_im.astype(F32)
    bb_re = co_re[..., None] * b_re - co_im[..., None] * b_im
    bb_im = co_re[..., None] * b_im + co_im[..., None] * b_re
    ch_re, ch_im = bar_re, bar_im
    for _ in range(int(math.log2(S5_LS))):
        ch_re, ch_im = ch_re * ch_re - ch_im * ch_im, 2.0 * ch_re * ch_im
    gh = S5_GROUPS // 2
    eye = jnp.eye(gh, dtype=F32)

    def b_tiles(part):
        t = part.reshape(2, gh, S5_STATE, S5_GROUP)
        return jnp.einsum('ab,hapc->hacbp', eye, t).reshape(2, S5_HALF, S5_HSTATE)

    def c_tiles(part):
        t = part.reshape(2, gh, S5_GROUP, S5_STATE)
        return jnp.einsum('ab,hbcp->hapbc', eye, t).reshape(2, S5_HSTATE, S5_HALF)

    bm = jnp.concatenate([b_tiles(bb_re), b_tiles(bb_im)], axis=-1).astype(BF16)
    cm = jnp.stack([c_tiles(c_re.astype(F32)), -c_tiles(c_im.astype(F32))], axis=1).astype(BF16)

    def rows(re, im):
        return jnp.concatenate([re.reshape(2, S5_HSTATE), im.reshape(2, S5_HSTATE)], 0)

    return bm, cm, rows(bar_re, bar_im), rows(ch_re, ch_im)


def _mla_prep_kernel(cq_ref, ckv_ref, qn_ref, kvn_ref, wq_ref, wkv_ref, cos_ref, s1_ref, s2_ref,
                     q_ref, k_ref, v_ref):
    cos_t = cos_ref[0]
    s1 = s1_ref[0]
    s2 = s2_ref[0]
    half = MLA_ROPE // 2

    def rope(x):
        return x * cos_t + pltpu.roll(x, half, 1) * s1 + pltpu.roll(x, LANES - half, 1) * s2

    scale = (MLA_NOPE + MLA_ROPE) ** -0.5 * math.log2(math.e)
    q = _dot(_rms(cq_ref[0], qn_ref[...]).astype(BF16), wq_ref[...])
    blk = ckv_ref[0]
    kv = _dot(_rms(blk[:, :MLA_KV_RANK], kvn_ref[...]).astype(BF16), wkv_ref[...])
    kr = rope(blk[:, MLA_KV_RANK:MLA_KV_RANK + LANES]).astype(BF16)
    for h in range(MLA_HEADS):
        qn = q[:, h * MLA_QK:h * MLA_QK + MLA_NOPE] * scale
        qr = rope(q[:, h * MLA_QK + MLA_NOPE:(h + 1) * MLA_QK]) * scale
        q_ref[0, h] = jnp.concatenate([qn, qr], axis=1).astype(BF16)
        k_ref[0, h] = jnp.concatenate([kv[:, h * MLA_NOPE:(h + 1) * MLA_NOPE].astype(BF16), kr], axis=1)
        vh = kv[:, MLA_WIDTH + h * MLA_V:MLA_WIDTH + (h + 1) * MLA_V]
        v_ref[0, h] = jnp.concatenate([vh, jnp.ones_like(vh)], axis=1).astype(BF16)


def _mla_prep(proj, qn, kvn, wq, wkv, cos_t, s1_t, s2_t):
    nb, s, _ = proj.shape
    tab = pl.BlockSpec((1, PREP_TM, LANES), lambda b, i: (b, i, 0))
    return pl.pallas_call(
        _mla_prep_kernel,
        grid=(nb, s // PREP_TM),
        in_specs=[
            pl.BlockSpec((1, PREP_TM, COL_BLK), lambda b, i: (b, i, 1)),
            pl.BlockSpec((1, PREP_TM, COL_BLK), lambda b, i: (b, i, 2)),
            pl.BlockSpec(qn.shape, lambda b, i: (0, 0)),
            pl.BlockSpec(kvn.shape, lambda b, i: (0, 0)),
            pl.BlockSpec(wq.shape, lambda b, i: (0, 0)),
            pl.BlockSpec(wkv.shape, lambda b, i: (0, 0)),
            tab, tab, tab,
        ],
        out_specs=[
            pl.BlockSpec((1, MLA_HEADS, PREP_TM, MLA_QK), lambda b, i: (b, 0, i, 0)),
            pl.BlockSpec((1, MLA_HEADS, PREP_TM, MLA_QK), lambda b, i: (b, 0, i, 0)),
            pl.BlockSpec((1, MLA_HEADS, PREP_TM, 2 * MLA_V), lambda b, i: (b, 0, i, 0)),
        ],
        out_shape=[
            jax.ShapeDtypeStruct((nb, MLA_HEADS, s, MLA_QK), BF16),
            jax.ShapeDtypeStruct((nb, MLA_HEADS, s, MLA_QK), BF16),
            jax.ShapeDtypeStruct((nb, MLA_HEADS, s, 2 * MLA_V), BF16),
        ],
        compiler_params=_params("parallel", "parallel"),
        name="mla_prep",
    )(proj, proj, qn, kvn, wq, wkv, cos_t, s1_t, s2_t)


def _flash_kernel(q_ref, k_ref, v_ref, o_ref, s_ref, p_ref, m_ref, a_ref, acc_ref):
    qi = pl.program_id(2)
    t = FLASH_T
    heads = range(FLASH_HEADS)
    nh = FLASH_HEADS
    m_ref[...] = jnp.full(m_ref.shape, -jnp.inf, F32)
    acc_ref[...] = jnp.zeros_like(acc_ref)
    p_ref[nh:] = jnp.zeros((nh,) + p_ref.shape[1:], BF16)
    a_ref[nh:] = jnp.ones((nh,) + a_ref.shape[1:], F32)

    def scores(j, slot):
        start = pl.multiple_of(j * t, t)
        for h in heads:
            s_ref[slot * nh + h] = _dot_nt(q_ref[0, h], k_ref[0, h, pl.ds(start, t), :])

    def softmax(slot, diagonal):
        for h in heads:
            for r0 in range(0, t, FLASH_ROWS):
                rows = slice(r0, r0 + FLASH_ROWS)
                sc = s_ref[slot * nh + h, rows, :]
                if diagonal:
                    row = lax.broadcasted_iota(jnp.int32, sc.shape, 0) + r0
                    col = lax.broadcasted_iota(jnp.int32, sc.shape, 1)
                    sc = jnp.where(col <= row, sc, MASK_VALUE)
                m_old = m_ref[h, rows, :]
                m_new = jnp.maximum(m_old, jnp.max(sc, axis=-1, keepdims=True))
                p_ref[slot * nh + h, rows, :] = jnp.exp2(sc - m_new[:, :1]).astype(BF16)
                a_ref[slot * nh + h, rows, :] = jnp.exp2(m_old - m_new)
                m_ref[h, rows, :] = m_new

    def accumulate(j, slot):
        start = pl.multiple_of(j * t, t)
        for h in heads:
            pv = _dot(p_ref[slot * nh + h], v_ref[0, h, pl.ds(start, t), :])
            alpha = a_ref[slot * nh + h]
            acc_ref[h] = jnp.concatenate([alpha, alpha], axis=1) * acc_ref[h] + pv

    def stage(j, slot, with_accumulate=True):
        softmax(slot, False)
        scores(j + 1, 1 - slot)
        if with_accumulate:
            accumulate(jnp.maximum(j - 1, 0), 1 - slot)

    odd = lax.rem(qi, 2)
    for slot in range(2):
        @pl.when(odd == slot)
        def _():
            scores(0, slot)

    @pl.when(odd == 1)
    def _():
        stage(0, 1, with_accumulate=False)

    def body(jj, carry):
        j = 2 * jj + odd
        stage(j, 0)
        stage(j + 1, 1)
        return carry

    lax.fori_loop(0, qi // 2, body, 0)
    softmax(0, True)
    accumulate(jnp.maximum(qi - 1, 0), 1)
    accumulate(qi, 0)

    for h in heads:
        acc = acc_ref[h]
        o_ref[0, :, h * MLA_V:(h + 1) * MLA_V] = (acc[:, :MLA_V] / acc[:, MLA_V:]).astype(o_ref.dtype)


def _flash(q, k, v):
    nb, nh, s, _ = q.shape
    hb = FLASH_HEADS
    return pl.pallas_call(
        _flash_kernel,
        grid=(nb, nh // hb, s // FLASH_T),
        in_specs=[
            pl.BlockSpec((1, hb, FLASH_T, MLA_QK), lambda b, h, i: (b, h, i, 0)),
            pl.BlockSpec((1, hb, s, MLA_QK), lambda b, h, i: (b, h, 0, 0)),
            pl.BlockSpec((1, hb, s, 2 * MLA_V), lambda b, h, i: (b, h, 0, 0)),
        ],
        out_specs=pl.BlockSpec((1, FLASH_T, hb * MLA_V), lambda b, h, i: (b, i, h)),
        out_shape=jax.ShapeDtypeStruct((nb, s, nh * MLA_V), BF16),
        scratch_shapes=[pltpu.VMEM((2 * hb, FLASH_T, FLASH_T), F32), pltpu.VMEM((2 * hb, FLASH_T, FLASH_T), BF16),
                        pltpu.VMEM((hb, FLASH_T, LANES), F32), pltpu.VMEM((2 * hb, FLASH_T, LANES), F32),
                        pltpu.VMEM((hb, FLASH_T, 2 * MLA_V), F32)],
        compiler_params=_params("parallel", "parallel", "arbitrary"),
        name="flash",
    )(q, k, v)


def _cumsum_rows(x, ltri):
    hi = x.astype(BF16)
    r1 = x - hi.astype(F32)
    mid = r1.astype(BF16)
    lo = (r1 - mid.astype(F32)).astype(BF16)
    return _dot(ltri, hi) + _dot(ltri, mid) + _dot(ltri, lo)


def _hgrn_kernel(q_ref, f_ref, i_ref, g_ref, lb_ref, on_ref, ones_ref, ltri_ref, o_ref, st_ref):
    @pl.when(pl.program_id(1) == 0)
    def _():
        st_ref[...] = jnp.zeros_like(st_ref)

    for n in range(q_ref.shape[0]):
        _hgrn_chunk(n, q_ref, f_ref, i_ref, g_ref, lb_ref, on_ref, ones_ref, ltri_ref, o_ref, st_ref)


def _hgrn_chunk(n, q_ref, f_ref, i_ref, g_ref, lb_ref, on_ref, ones_ref, ltri_ref, o_ref, st_ref):
    c, w = HG_CHUNK, HG_WIDTH
    z = f_ref[n]
    lb = lb_ref[...]
    e = jnp.exp(-jnp.abs(z))
    r = 1.0 / (1.0 + e)
    pos = z >= 0.0
    sig_z = jnp.where(pos, r, e * r)
    sig_mz = jnp.where(pos, e * r, r)
    kk = (1.0 - lb) * sig_mz
    q = _silu(q_ref[n])
    v = i_ref[n]
    gate = _silu(g_ref[n])
    f = lb + (1.0 - lb) * sig_z
    b = _cumsum_rows(jnp.log2(f), ltri_ref[...])
    row = lax.broadcasted_iota(jnp.int32, (c, w), 0)

    q_lvls, k_lvls = [], []
    for m in HG_LEVELS:
        half = m // 2
        b4 = b.reshape(c // m, 2, half, w)
        ref_b = b4[:, 0, half - 1:half, :]
        k_first = kk.reshape(c // m, 2, half, w)[:, 0] * jnp.exp2(ref_b - b4[:, 0])
        q_second = q.reshape(c // m, 2, half, w)[:, 1] * jnp.exp2(b4[:, 1] - ref_b)
        zero = jnp.zeros_like(k_first)
        q_lvls.append(jnp.stack([zero, q_second], axis=1).reshape(c, w).astype(BF16))
        k_lvls.append(jnp.stack([k_first, zero], axis=1).reshape(c, w).astype(BF16))

    pos8 = row & (SUBLANES - 1)

    def back1(x):
        x3 = x.reshape(c // SUBLANES, SUBLANES, w)
        return pltpu.roll(x3, 1, 1).reshape(c, w)

    f_in = jnp.where(pos8 >= 1, f, 0.0)
    kd_dl, vr = kk, v
    o_diag = jnp.zeros((c, w), F32)
    for dl in range(SUBLANES):
        if dl > 0:
            kd_dl = f_in * back1(kd_dl)
            vr = back1(vr)
        wb = (q * kd_dl).astype(BF16)
        a = jnp.concatenate([_dot(wb[:, :MXU_DIM], ones_ref[...]), _dot(wb[:, MXU_DIM:], ones_ref[...])], axis=1)
        o_diag = o_diag + a * vr

    tt = lax.broadcasted_iota(jnp.int32, (c, c), 0)
    ss = lax.broadcasted_iota(jnp.int32, (c, c), 1)
    same_blk = {m: (tt // m) == (ss // m) for m in HG_LEVELS}

    b_last = b[c - 1:c, :]
    qe = (q * jnp.exp2(b)).astype(BF16)
    kd = (kk * jnp.exp2(b_last - b)).astype(BF16)
    decay = jnp.exp2(b_last)
    vb = v.astype(BF16)
    for h in range(HG_HEADS):
        hs = slice(h * HG_DK, (h + 1) * HG_DK)
        st = st_ref[n, h]
        attn = jnp.zeros((c, c), F32)
        for m, ql, kl in zip(HG_LEVELS, q_lvls, k_lvls):
            part = _dot_nt(ql[:, hs], kl[:, hs])
            attn = attn + (part if m == c else jnp.where(same_blk[m], part, 0.0))
        o_h =_dot_nt(qe[:, hs], st.astype(BF16)) + _dot(attn.astype(BF16), vb[:, hs]) + o_diag[:, hs]
        st_ref[n, h] = st * decay[:, hs] + _dot_tn(vb[:, hs], kd[:, hs])
        o_ref[n, :, hs] = (_rms(o_h, on_ref[...]) * gate[:, hs]).astype(o_ref.dtype)


def _hgrn(proj, lb, onorm):
    nb, s, _ = proj.shape
    head_id = jnp.arange(MXU_DIM) // HG_DK
    ones_blk = (head_id[:, None] == head_id[None, :]).astype(BF16)
    t_id = jnp.arange(HG_CHUNK)
    ltri = (t_id[:, None] >= t_id[None, :]).astype(BF16)
    ns = HG_SEQS if nb % HG_SEQS == 0 else 1
    spec = lambda cb: pl.BlockSpec((ns, HG_CHUNK, COL_BLK), lambda b, i: (b, i, cb))
    return pl.pallas_call(
        _hgrn_kernel,
        grid=(nb // ns, s // HG_CHUNK),
        in_specs=[spec(3), spec(4), spec(5), spec(6),
                  pl.BlockSpec(lb.shape, lambda b, i: (0, 0)),
                  pl.BlockSpec(onorm.shape, lambda b, i: (0, 0)),
                  pl.BlockSpec(ones_blk.shape, lambda b, i: (0, 0)),
                  pl.BlockSpec(ltri.shape, lambda b, i: (0, 0))],
        out_specs=pl.BlockSpec((ns, HG_CHUNK, HG_WIDTH), lambda b, i: (b, i, 0)),
        out_shape=jax.ShapeDtypeStruct((nb, s, HG_WIDTH), BF16),
        scratch_shapes=[pltpu.VMEM((ns, HG_HEADS, HG_DV, HG_DK), F32)],
        compiler_params=_params("parallel", "arbitrary"),
        name="hgrn",
    )(proj, proj, proj, proj, lb, onorm, ones_blk, ltri)


def _out_proj_kernel(ys_ref, ym_ref, yh_ref, w_ref, x_ref, gain_ref, g_ref, o_ref):
    mixed = (_dot(ys_ref[0], w_ref[:S5_WIDTH, :])
             + _dot(ym_ref[0], w_ref[S5_WIDTH:S5_WIDTH + MLA_WIDTH, :])
             + _dot(yh_ref[0], w_ref[S5_WIDTH + MLA_WIDTH:, :]))
    o_ref[0] = x_ref[0] + g_ref[0] * _rms(mixed, gain_ref[...])


def _out_proj(y_s5, y_mla, y_hg, w, l, x, gain, g):
    nb, s, d = x.shape
    row = lambda n: pl.BlockSpec((1, OUT_TM, n), lambda b, i: (b, i, 0))
    return pl.pallas_call(
        _out_proj_kernel,
        grid=(nb, s // OUT_TM),
        in_specs=[row(S5_WIDTH), row(MLA_WIDTH), row(HG_WIDTH),
                  pl.BlockSpec((None,) + w.shape[1:], lambda b, i: (l, 0, 0)),
                  row(d),
                  pl.BlockSpec((1, d), lambda b, i: (0, 0)),
                  pl.BlockSpec((1, 1, d), lambda b, i: (b, 0, 0))],
        out_specs=row(d),
        out_shape=jax.ShapeDtypeStruct((nb, s, d), F32),
        compiler_params=_params("parallel", "parallel"),
        name="out_proj",
    )(y_s5, y_mla, y_hg, w, x, gain, g)


def _ffn_kernel(x_ref, sc_ref, sh_ref, g_ref, pre_ref, post_ref,
                wg_ref, wv_ref, cwg_ref, cbg_ref, cwv_ref, cbv_ref, wd_ref, o_ref,
                h_ref, acc_ref, ug_ref, uv_ref, act_ref, tail_ref):
    i = pl.program_id(1)
    j = pl.program_id(2)

    @pl.when(j == 0)
    def _():
        scale = pre_ref[...] * (1.0 + sc_ref[0])
        shift = sh_ref[0]
        for r0 in range(0, FFN_TM, NORM_ROWS):
            x = x_ref[0, r0:r0 + NORM_ROWS, :]
            inv = lax.rsqrt(jnp.mean(x * x, axis=-1, keepdims=True) + EPS)
            h_ref[r0:r0 + NORM_ROWS, :] = (x * inv * scale + shift).astype(BF16)
        acc_ref[...] = jnp.zeros_like(acc_ref)

    def up():
        for u_ref, w_ref, t in ((ug_ref, wg_ref, 0), (uv_ref, wv_ref, 1)):
            u_ref[:HALO, :] = jnp.where(i == 0, 0.0, tail_ref[j, t])
            u = _dot(h_ref[...], w_ref[...])
            u_ref[HALO:, :] = u
            tail_ref[j, t] = u[FFN_TM - HALO:, :]

    def conv(u_ref, cw_ref, cb_ref, r0):
        return (cb_ref[...] + u_ref[r0 + HALO - 2:r0 + HALO - 2 + FFN_ROWS, :] * cw_ref[0:1, :]
                + u_ref[r0 + HALO - 1:r0 + HALO - 1 + FFN_ROWS, :] * cw_ref[1:2, :]
                + u_ref[r0 + HALO:r0 + HALO + FFN_ROWS, :] * cw_ref[2:3, :])

    up()
    for r0 in range(0, FFN_TM, FFN_ROWS):
        gate = conv(ug_ref, cwg_ref, cbg_ref, r0)
        val = conv(uv_ref, cwv_ref, cbv_ref, r0)
        act_ref[r0:r0 + FFN_ROWS, :] = (_gelu_tanh(gate) * val).astype(BF16)
    acc_ref[...] += _dot(act_ref[...], wd_ref[...])

    @pl.when(j == pl.num_programs(2) - 1)
    def _():
        scale = post_ref[...] * g_ref[0]
        for r0 in range(0, FFN_TM, NORM_ROWS):
            y = acc_ref[r0:r0 + NORM_ROWS, :]
            inv = lax.rsqrt(jnp.mean(y * y, axis=-1, keepdims=True) + EPS)
            o_ref[0, r0:r0 + NORM_ROWS, :] = x_ref[0, r0:r0 + NORM_ROWS, :] + y * inv * scale


def _ffn(x, sc, sh, g, pre, post, w_up, conv_w, conv_b, w_down, l):
    nb, s, d = x.shape
    ff = w_down.shape[1]
    nf = ff // FFN_TF
    vec = pl.BlockSpec((1, 1, d), lambda b, i, j: (b, 0, 0))
    par = pl.BlockSpec((1, d), lambda b, i, j: (0, 0))
    gate = lambda r: pl.BlockSpec((None, r, FFN_TF), lambda b, i, j: (l, 0, j))
    val = lambda r: pl.BlockSpec((None, r, FFN_TF), lambda b, i, j: (l, 0, j + nf))
    return pl.pallas_call(
        _ffn_kernel,
        grid=(nb, s // FFN_TM, nf),
        in_specs=[
            pl.BlockSpec((1, FFN_TM, d), lambda b, i, j: (b, i, 0)),
            vec, vec, vec, par, par,
            gate(d), val(d), gate(CONV_W), gate(1), val(CONV_W), val(1),
            pl.BlockSpec((None, FFN_TF, d), lambda b, i, j: (l, j, 0)),
        ],
        out_specs=pl.BlockSpec((1, FFN_TM, d), lambda b, i, j: (b, i, 0)),
        out_shape=jax.ShapeDtypeStruct((nb, s, d), F32),
        scratch_shapes=[pltpu.VMEM((FFN_TM, d), BF16), pltpu.VMEM((FFN_TM, d), F32),
                        pltpu.VMEM((FFN_TM + HALO, FFN_TF), F32), pltpu.VMEM((FFN_TM + HALO, FFN_TF), F32),
                        pltpu.VMEM((FFN_TM, FFN_TF), BF16), pltpu.VMEM((nf, 2, HALO, FFN_TF), F32)],
        compiler_params=_params("parallel", "arbitrary", "arbitrary"),
        name="ffn",
    )(x, sc, sh, g, pre, post, w_up, w_up, conv_w, conv_b, conv_w, conv_b, w_down)


def _pad_cols(a, n):
    return jnp.pad(a, [(0, 0)] * (a.ndim - 1) + [(0, n - a.shape[-1])])


def _layout_w_in(w_in):
    o = 0
    parts = []
    for width in (S5_WIDTH, MLA_Q_RANK, MLA_KV_RANK, MLA_ROPE):
        parts.append(w_in[..., o:o + width])
        o += width
    parts.append(jnp.zeros(w_in.shape[:-1] + (COL_BLK - MLA_KV_RANK - MLA_ROPE,), w_in.dtype))
    parts.append(w_in[..., o:])
    return jnp.concatenate(parts, axis=-1).astype(BF16)


def _layout_w_uq(w):
    nl, r, _ = w.shape
    w = w.reshape(nl, r, MLA_HEADS, MLA_NOPE + MLA_ROPE)
    return _pad_cols(w, MLA_QK).reshape(nl, r, MLA_HEADS * MLA_QK).astype(BF16)


def _layout_w_ukv(w):
    nl, r, _ = w.shape
    w = w.reshape(nl, r, MLA_HEADS, MLA_NOPE + MLA_V)
    k = w[..., :MLA_NOPE].reshape(nl, r, MLA_HEADS * MLA_NOPE)
    v = w[..., MLA_NOPE:].reshape(nl, r, MLA_HEADS * MLA_V)
    return jnp.concatenate([k, v], axis=-1).astype(BF16)


def _rope_tables(positions):
    inv_freq = 1.0 / (ROPE_THETA ** (jnp.arange(0, MLA_ROPE, 2, dtype=F32) / MLA_ROPE))
    ang = positions.astype(F32)[..., None] * inv_freq
    cos, sin = jnp.cos(ang), jnp.sin(ang)
    z32 = jnp.zeros_like(cos)
    z64 = jnp.concatenate([z32, z32], axis=-1)
    return (jnp.concatenate([cos, cos, z64], axis=-1),
            jnp.concatenate([z32, sin, z64], axis=-1),
            jnp.concatenate([-sin, z32, z64], axis=-1))


def kernel(x, c, positions, w_in, s5_lambda_re, s5_lambda_im, s5_log_dt, s5_b_re, s5_b_im, s5_c_re, s5_c_im, s5_d, s5_w_glu, mla_q_norm, mla_w_uq, mla_kv_norm, mla_w_ukv, hg_lb_logits, hg_out_norm, w_out, mix_pre_norm, mix_post_norm, ffn_pre_norm, ffn_post_norm, ffn_w_up, ffn_conv_w, ffn_conv_b, ffn_w_down, w_ada, b_ada):
    nb, s, d = x.shape
    nl = w_in.shape[0]
    assert s % FLASH_T == 0 and s % FFN_TM == 0 and s % (S5_LS * SUBLANES) == 0

    mod = _adaln(c, w_ada, b_ada)
    cos_t, s1_t, s2_t = _rope_tables(positions)
    probs = jax.nn.softmax(hg_lb_logits.astype(F32), axis=0)
    lower_bounds = jnp.cumsum(probs, axis=0) - probs[0:1]

    w_in_p = _layout_w_in(w_in)
    w_uq_p = _layout_w_uq(mla_w_uq)
    w_ukv_p = _layout_w_ukv(mla_w_ukv)
    w_out_b = w_out.astype(BF16)
    w_glu_b = s5_w_glu.astype(BF16)
    def pad_halves(a):
        return jnp.concatenate([_pad_cols(a[..., :D_FF], D_FF_PAD), _pad_cols(a[..., D_FF:], D_FF_PAD)], axis=-1)

    w_up_p = pad_halves(ffn_w_up).astype(BF16)
    conv_w_p = pad_halves(ffn_conv_w)
    conv_b_p = pad_halves(ffn_conv_b)[:, None, :]
    w_down = jnp.pad(ffn_w_down, ((0, 0), (0, D_FF_PAD - D_FF), (0, 0))).astype(BF16)

    for l in range(nl):
        sh1, sc1, g1, sh2, sc2, g2 = [m[:, None, :] for m in jnp.split(mod[l], 6, axis=-1)]
        proj = _proj_in(x, sc1, sh1, mix_pre_norm[l][None, :], w_in_p, l)
        bm, cm, lam, lam_chunk = _s5_prep(s5_lambda_re[l], s5_lambda_im[l], s5_log_dt[l],
                                          s5_b_re[l], s5_b_im[l], s5_c_re[l], s5_c_im[l])
        y_s5 = _s5(proj, bm, cm, lam, lam_chunk, s5_d[l][None, :], w_glu_b[l])
        q, k, v = _mla_prep(proj, mla_q_norm[l][None, :], mla_kv_norm[l][None, :], w_uq_p[l], w_ukv_p[l],
                            cos_t, s1_t, s2_t)
        y_mla = _flash(q, k, v)
        y_hg = _hgrn(proj, lower_bounds[l][None, :], hg_out_norm[l][None, :])
        x = _out_proj(y_s5, y_mla, y_hg, w_out_b, l, x, mix_post_norm[l][None, :], g1)
        x = _ffn(x, sc2, sh2, g2, ffn_pre_norm[l][None, :], ffn_post_norm[l][None, :],
                 w_up_p, conv_w_p, conv_b_p, w_down, l)
    return x
```

```python
import functools
import math

import jax
import jax.numpy as jnp
from jax import lax
from jax.experimental import pallas as pl
from jax.experimental.pallas import tpu as pltpu

F32 = jnp.float32
BF16 = jnp.bfloat16

D_MODEL = 2048
DEPTH = 4
S5_WIDTH = 512
S5_GROUP = 16
S5_GROUPS = 32
S5_STATE = 64
MLA_HEADS = 8
MLA_NOPE = 128
MLA_ROPE = 64
MLA_V = 128
MLA_Q_RANK = 512
MLA_KV_RANK = 256
MLA_WIDTH = MLA_HEADS * MLA_V
ROPE_THETA = 10000.0
MASK_VALUE = -1e30
HG_HEADS = 4
HG_DK = 128
HG_DV = 128
HG_WIDTH = HG_HEADS * HG_DV
D_FF = 5504
CONV_W = 3
EPS = 1e-6

LANES = 128
SUBLANES = 8
MXU_DIM = 256
VMEM_LIMIT_BYTES = 56 * 1024 * 1024

COL_BLK = 512
N_COL_BLKS = 7
D_IN_PAD = N_COL_BLKS * COL_BLK
MLA_QK = 2 * LANES
D_FF_PAD = 5632

ADA_TN = 1024
PROJ_TM = 512
PROJ_TN = D_IN_PAD // 2
S5_LS = 16
S5_HALF = S5_WIDTH // 2
S5_HSTATE = (S5_GROUPS // 2) * S5_STATE
PREP_TM = 512
FLASH_T = 512
FLASH_HEADS = 2
FLASH_ROWS = 32
HG_CHUNK = 128
HG_LEVELS = (16, 32, 64, 128)
HG_SEQS = 2
OUT_TM = 512
FFN_TM = 512
FFN_TF = 512
FFN_ROWS = 64
NORM_ROWS = 16
CAST_TM = 256
HALO = SUBLANES


def _sigmoid(x):
    return 1.0 / (1.0 + jnp.exp(-x))


def _silu(x):
    return x * _sigmoid(x)


def _gelu_tanh(x):
    c = math.sqrt(2.0 / math.pi)
    return x * (0.5 * (1.0 + jnp.tanh(c * (x + 0.044715 * (x * x * x)))))


def _rms(x, gain):
    return x * lax.rsqrt(jnp.mean(x * x, axis=-1, keepdims=True) + EPS) * gain


def _dot(a, b):
    return jnp.dot(a, b, preferred_element_type=F32)


def _dot_nt(a, b):
    return lax.dot_general(a, b, (((1,), (1,)), ((), ())), preferred_element_type=F32)


def _dot_tn(a, b):
    return lax.dot_general(a, b, (((0,), (0,)), ((), ())), preferred_element_type=F32)


def _params(*sem):
    return pltpu.CompilerParams(dimension_semantics=sem, vmem_limit_bytes=VMEM_LIMIT_BYTES)


def _adaln_kernel(c_ref, w_ref, b_ref, o_ref):
    ca = _silu(c_ref[...]).astype(BF16)
    o_ref[0] = _dot(ca, w_ref[0].astype(BF16)) + b_ref[0]


def _adaln(c, w_ada, b_ada):
    nb, d = c.shape
    nl, _, n = w_ada.shape
    rows = -(-nb // SUBLANES) * SUBLANES
    c_pad = jnp.pad(c, ((0, rows - nb), (0, 0)))
    out = pl.pallas_call(
        _adaln_kernel,
        grid=(nl, n // ADA_TN),
        in_specs=[
            pl.BlockSpec((rows, d), lambda l, j: (0, 0)),
            pl.BlockSpec((1, d, ADA_TN), lambda l, j: (l, 0, j)),
            pl.BlockSpec((1, 1, ADA_TN), lambda l, j: (l, 0, j)),
        ],
        out_specs=pl.BlockSpec((1, rows, ADA_TN), lambda l, j: (l, 0, j)),
        out_shape=jax.ShapeDtypeStruct((nl, rows, n), F32),
        compiler_params=_params("parallel", "parallel"),
        name="adaln",
    )(c_pad, w_ada, b_ada.reshape(nl, 1, n))
    return out[:, :nb]


def _proj_in_kernel(x_ref, sc_ref, sh_ref, gain_ref, w_ref, o_ref, u_ref, h_ref, slab_ref):
    j = pl.program_id(2)

    @pl.when(j == 0)
    def _():
        scale = gain_ref[...] * (1.0 + sc_ref[0])
        shift = sh_ref[0]
        for r0 in range(0, PROJ_TM, NORM_ROWS):
            x = x_ref[0, r0:r0 + NORM_ROWS, :]
            inv = lax.rsqrt(jnp.mean(x * x, axis=-1, keepdims=True) + EPS)
            h_ref[r0:r0 + NORM_ROWS, :] = (x * inv * scale + shift).astype(BF16)

    res = _dot(h_ref[...], w_ref[...])
    o_ref[0] = res

    @pl.when(j == 0)
    def _():
        for c in range(S5_WIDTH // LANES):
            slab_ref[c] = res[:, c * LANES:(c + 1) * LANES]
        for t in range(S5_LS):
            for c in range(S5_WIDTH // LANES):
                col = t * S5_WIDTH + c * LANES
                u_ref[0, :, col:col + LANES] = slab_ref[c, pl.ds(t, PROJ_TM // S5_LS, stride=S5_LS), :]


def _proj_in(x, sc, sh, gain, w, l):
    nb, s, d = x.shape
    n = w.shape[2]
    rows = PROJ_TM // S5_LS
    return pl.pallas_call(
        _proj_in_kernel,
        grid=(nb, s // PROJ_TM, n // PROJ_TN),
        in_specs=[
            pl.BlockSpec((1, PROJ_TM, d), lambda b, i, j: (b, i, 0)),
            pl.BlockSpec((1, 1, d), lambda b, i, j: (b, 0, 0)),
            pl.BlockSpec((1, 1, d), lambda b, i, j: (b, 0, 0)),
            pl.BlockSpec((1, d), lambda b, i, j: (0, 0)),
            pl.BlockSpec((None, d, PROJ_TN), lambda b, i, j: (l, 0, j)),
        ],
        out_specs=[pl.BlockSpec((1, PROJ_TM, PROJ_TN), lambda b, i, j: (b, i, j)),
                   pl.BlockSpec((1, rows, S5_LS * S5_WIDTH), lambda b, i, j: (b, i, 0))],
        out_shape=[jax.ShapeDtypeStruct((nb, s, n), F32),
                   jax.ShapeDtypeStruct((nb, s // S5_LS, S5_LS * S5_WIDTH), F32)],
        scratch_shapes=[pltpu.VMEM((PROJ_TM, d), BF16), pltpu.VMEM((S5_WIDTH // LANES, PROJ_TM, LANES), F32)],
        compiler_params=_params("parallel", "parallel", "arbitrary"),
        name="proj_in",
    )(x, sc, sh, gain, w)


def _s5_kernel(u_ref, bm_ref, cm_ref, lam_ref, lam_chunk_ref, d_ref, wglu_ref, o_ref,
               sre, sim, hre, him, yloc):
    ph = pl.program_id(1)
    s = pl.program_id(2)
    kc = sre.shape[1]

    @pl.when(ph == 0)
    def _():
        @pl.when(s == 0)
        def _():
            sre[...] = jnp.zeros_like(sre)
            sim[...] = jnp.zeros_like(sim)

        u = u_ref[0]
        ub = u.astype(BF16)
        for h in range(2):
            cols = slice(h * S5_HALF, (h + 1) * S5_HALF)
            v = _dot(ub[:, cols], bm_ref[h])
            lr = lam_ref[h:h + 1, :]
            li = lam_ref[2 + h:3 + h, :]
            re = sre[h]
            im = sim[h]
            nre = lr * re - li * im + v[:, :S5_HSTATE]
            nim = lr * im + li * re + v[:, S5_HSTATE:]
            sre[h] = nre
            sim[h] = nim
            yl = _dot(nre.astype(BF16), cm_ref[h, 0]) + _dot(nim.astype(BF16), cm_ref[h, 1])
            yloc[s, :, cols] = yl + d_ref[:, cols] * u[:, cols]

        @pl.when(s == S5_LS - 1)
        def _():
            def body(k, carry):
                new = []
                for h in range(2):
                    cre, cim = carry[2 * h], carry[2 * h + 1]
                    hre[h, pl.ds(k, 1), :] = cre
                    him[h, pl.ds(k, 1), :] = cim
                    lr = lam_chunk_ref[h:h + 1, :]
                    li = lam_chunk_ref[2 + h:3 + h, :]
                    new.append(lr * cre - li * cim + sre[h, pl.ds(k, 1), :])
                    new.append(lr * cim + li * cre + sim[h, pl.ds(k, 1), :])
                return tuple(new)

            zero = jnp.zeros((1, S5_HSTATE), F32)
            lax.fori_loop(0, kc, body, (zero, zero, zero, zero))

    @pl.when(ph == 1)
    def _():
        ys = []
        for h in range(2):
            cols = slice(h * S5_HALF, (h + 1) * S5_HALF)
            lr = lam_ref[h:h + 1, :]
            li = lam_ref[2 + h:3 + h, :]
            pre = hre[h]
            pim = him[h]
            nre = lr * pre - li * pim
            nim = lr * pim + li * pre
            hre[h] = nre
            him[h] = nim
            yc = _dot(nre.astype(BF16), cm_ref[h, 0]) + _dot(nim.astype(BF16), cm_ref[h, 1])
            ys.append(yloc[s, :, cols] + yc)
        g = _gelu_tanh(jnp.concatenate(ys, axis=1))
        gate = _sigmoid(_dot(g.astype(BF16), wglu_ref[...]))
        o_ref[0] = (g * gate).astype(o_ref.dtype)


def _s5(u_rows, bm, cm, lam, lam_chunk, d, wglu):
    nb, kc, _ = u_rows.shape
    return pl.pallas_call(
        _s5_kernel,
        grid=(nb, 2, S5_LS),
        in_specs=[
            pl.BlockSpec((1, kc, S5_WIDTH),
                         lambda b, p, t: (b, 0, jnp.where(p == 0, t, S5_LS - 1))),
            pl.BlockSpec(bm.shape, lambda b, p, t: (0, 0, 0)),
            pl.BlockSpec(cm.shape, lambda b, p, t: (0, 0, 0, 0)),
            pl.BlockSpec(lam.shape, lambda b, p, t: (0, 0)),
            pl.BlockSpec(lam_chunk.shape, lambda b, p, t: (0, 0)),
            pl.BlockSpec(d.shape, lambda b, p, t: (0, 0)),
            pl.BlockSpec(wglu.shape, lambda b, p, t: (0, 0)),
        ],
        out_specs=pl.BlockSpec((1, kc, S5_WIDTH), lambda b, p, t: (b, 0, t * p)),
        out_shape=jax.ShapeDtypeStruct((nb, kc, S5_LS * S5_WIDTH), F32),
        scratch_shapes=[pltpu.VMEM((2, kc, S5_HSTATE), F32)] * 4
        + [pltpu.VMEM((S5_LS, kc, S5_WIDTH), F32)],
        compiler_params=_params("parallel", "arbitrary", "arbitrary"),
        name="s5",
    )(u_rows, bm, cm, lam, lam_chunk, d, wglu)


def _s5_prep(lam_re, lam_im, log_dt, b_re, b_im, c_re, c_im):
    lam_re, lam_im = lam_re.astype(F32), lam_im.astype(F32)
    dt = jnp.exp(log_dt.astype(F32))[:, None]
    mag = jnp.exp(lam_re * dt)
    bar_re, bar_im = mag * jnp.cos(lam_im * dt), mag * jnp.sin(lam_im * dt)
    den = lam_re * lam_re + lam_im * lam_im
    co_re = ((bar_re - 1.0) * lam_re + bar_im * lam_im) / den
    co_im = (bar_im * lam_re - (bar_re - 1.0) * lam_im) / den
    b_re, b_im = b_re.astype(F32), b_im.astype(F32)
    bb_re = co_re[..., None] * b_re - co_im[..., None] * b_im
    bb_im = co_re[..., None] * b_im + co_im[..., None] * b_re
    ch_re, ch_im = bar_re, bar_im
    for _ in range(int(math.log2(S5_LS))):
        ch_re, ch_im = ch_re * ch_re - ch_im * ch_im, 2.0 * ch_re * ch_im
    gh = S5_GROUPS // 2
    eye = jnp.eye(gh, dtype=F32)

    def b_tiles(part):
        t = part.reshape(2, gh, S5_STATE, S5_GROUP)
        return jnp.einsum('ab,hapc->hacbp', eye, t).reshape(2, S5_HALF, S5_HSTATE)

    def c_tiles(part):
        t = part.reshape(2, gh, S5_GROUP, S5_STATE)
        return jnp.einsum('ab,hbcp->hapbc', eye, t).reshape(2, S5_HSTATE, S5_HALF)

    bm = jnp.concatenate([b_tiles(bb_re), b_tiles(bb_im)], axis=-1).astype(BF16)
    cm = jnp.stack([c_tiles(c_re.astype(F32)), -c_tiles(c_im.astype(F32))], axis=1).astype(BF16)

    def rows(re, im):
        return jnp.concatenate([re.reshape(2, S5_HSTATE), im.reshape(2, S5_HSTATE)], 0)

    return bm, cm, rows(bar_re, bar_im), rows(ch_re, ch_im)


def _mla_prep_kernel(cq_ref, ckv_ref, qn_ref, kvn_ref, wq_ref, wkv_ref, cos_ref, s1_ref, s2_ref,
                     q_ref, k_ref, v_ref):
    cos_t = cos_ref[0]
    s1 = s1_ref[0]
    s2 = s2_ref[0]
    half = MLA_ROPE // 2

    def rope(x):
        return x * cos_t + pltpu.roll(x, half, 1) * s1 + pltpu.roll(x, LANES - half, 1) * s2

    scale = (MLA_NOPE + MLA_ROPE) ** -0.5 * math.log2(math.e)
    q = _dot(_rms(cq_ref[0], qn_ref[...]).astype(BF16), wq_ref[...])
    blk = ckv_ref[0]
    kv = _dot(_rms(blk[:, :MLA_KV_RANK], kvn_ref[...]).astype(BF16), wkv_ref[...])
    kr = rope(blk[:, MLA_KV_RANK:MLA_KV_RANK + LANES]).astype(BF16)
    for h in range(MLA_HEADS):
        qn = q[:, h * MLA_QK:h * MLA_QK + MLA_NOPE] * scale
        qr = rope(q[:, h * MLA_QK + MLA_NOPE:(h + 1) * MLA_QK]) * scale
        q_ref[0, h] = jnp.concatenate([qn, qr], axis=1).astype(BF16)
        k_ref[0, h] = jnp.concatenate([kv[:, h * MLA_NOPE:(h + 1) * MLA_NOPE].astype(BF16), kr], axis=1)
        vh = kv[:, MLA_WIDTH + h * MLA_V:MLA_WIDTH + (h + 1) * MLA_V]
        v_ref[0, h] = jnp.concatenate([vh, jnp.ones_like(vh)], axis=1).astype(BF16)


def _mla_prep(proj, qn, kvn, wq, wkv, cos_t, s1_t, s2_t):
    nb, s, _ = proj.shape
    tab = pl.BlockSpec((1, PREP_TM, LANES), lambda b, i: (b, i, 0))
    return pl.pallas_call(
        _mla_prep_kernel,
        grid=(nb, s // PREP_TM),
        in_specs=[
            pl.BlockSpec((1, PREP_TM, COL_BLK), lambda b, i: (b, i, 1)),
            pl.BlockSpec((1, PREP_TM, COL_BLK), lambda b, i: (b, i, 2)),
            pl.BlockSpec(qn.shape, lambda b, i: (0, 0)),
            pl.BlockSpec(kvn.shape, lambda b, i: (0, 0)),
            pl.BlockSpec(wq.shape, lambda b, i: (0, 0)),
            pl.BlockSpec(wkv.shape, lambda b, i: (0, 0)),
            tab, tab, tab,
        ],
        out_specs=[
            pl.BlockSpec((1, MLA_HEADS, PREP_TM, MLA_QK), lambda b, i: (b, 0, i, 0)),
            pl.BlockSpec((1, MLA_HEADS, PREP_TM, MLA_QK), lambda b, i: (b, 0, i, 0)),
            pl.BlockSpec((1, MLA_HEADS, PREP_TM, 2 * MLA_V), lambda b, i: (b, 0, i, 0)),
        ],
        out_shape=[
            jax.ShapeDtypeStruct((nb, MLA_HEADS, s, MLA_QK), BF16),
            jax.ShapeDtypeStruct((nb, MLA_HEADS, s, MLA_QK), BF16),
            jax.ShapeDtypeStruct((nb, MLA_HEADS, s, 2 * MLA_V), BF16),
        ],
        compiler_params=_params("parallel", "parallel"),
        name="mla_prep",
    )(proj, proj, qn, kvn, wq, wkv, cos_t, s1_t, s2_t)


def _flash_kernel(q_ref, k_ref, v_ref, o_ref, s_ref, p_ref, m_ref, a_ref, acc_ref):
    qi = pl.program_id(2)
    t = FLASH_T
    heads = range(FLASH_HEADS)
    nh = FLASH_HEADS
    m_ref[...] = jnp.full(m_ref.shape, -jnp.inf, F32)
    acc_ref[...] = jnp.zeros_like(acc_ref)
    p_ref[nh:] = jnp.zeros((nh,) + p_ref.shape[1:], BF16)
    a_ref[nh:] = jnp.ones((nh,) + a_ref.shape[1:], F32)

    def scores(j, slot):
        start = pl.multiple_of(j * t, t)
        for h in heads:
            s_ref[slot * nh + h] = _dot_nt(q_ref[0, h], k_ref[0, h, pl.ds(start, t), :])

    def softmax(slot, diagonal):
        for h in heads:
            for r0 in range(0, t, FLASH_ROWS):
                rows = slice(r0, r0 + FLASH_ROWS)
                sc = s_ref[slot * nh + h, rows, :]
                if diagonal:
                    row = lax.broadcasted_iota(jnp.int32, sc.shape, 0) + r0
                    col = lax.broadcasted_iota(jnp.int32, sc.shape, 1)
                    sc = jnp.where(col <= row, sc, MASK_VALUE)
                m_old = m_ref[h, rows, :]
                m_new = jnp.maximum(m_old, jnp.max(sc, axis=-1, keepdims=True))
                p_ref[slot * nh + h, rows, :] = jnp.exp2(sc - m_new[:, :1]).astype(BF16)
                a_ref[slot * nh + h, rows, :] = jnp.exp2(m_old - m_new)
                m_ref[h, rows, :] = m_new

    def accumulate(j, slot):
        start = pl.multiple_of(j * t, t)
        for h in heads:
            pv = _dot(p_ref[slot * nh + h], v_ref[0, h, pl.ds(start, t), :])
            alpha = a_ref[slot * nh + h]
            acc_ref[h] = jnp.concatenate([alpha, alpha], axis=1) * acc_ref[h] + pv

    def stage(j, slot, with_accumulate=True):
        softmax(slot, False)
        scores(j + 1, 1 - slot)
        if with_accumulate:
            accumulate(jnp.maximum(j - 1, 0), 1 - slot)

    odd = lax.rem(qi, 2)
    for slot in range(2):
        @pl.when(odd == slot)
        def _():
            scores(0, slot)

    @pl.when(odd == 1)
    def _():
        stage(0, 1, with_accumulate=False)

    def body(jj, carry):
        j = 2 * jj + odd
        stage(j, 0)
        stage(j + 1, 1)
        return carry

    lax.fori_loop(0, qi // 2, body, 0)
    softmax(0, True)
    accumulate(jnp.maximum(qi - 1, 0), 1)
    accumulate(qi, 0)

    for h in heads:
        acc = acc_ref[h]
        o_ref[0, :, h * MLA_V:(h + 1) * MLA_V] = (acc[:, :MLA_V] / acc[:, MLA_V:]).astype(o_ref.dtype)


def _flash(q, k, v):
    nb, nh, s, _ = q.shape
    hb = FLASH_HEADS
    return pl.pallas_call(
        _flash_kernel,
        grid=(nb, nh // hb, s // FLASH_T),
        in_specs=[
            pl.BlockSpec((1, hb, FLASH_T, MLA_QK), lambda b, h, i: (b, h, i, 0)),
            pl.BlockSpec((1, hb, s, MLA_QK), lambda b, h, i: (b, h, 0, 0)),
            pl.BlockSpec((1, hb, s, 2 * MLA_V), lambda b, h, i: (b, h, 0, 0)),
        ],
        out_specs=pl.BlockSpec((1, FLASH_T, hb * MLA_V), lambda b, h, i: (b, i, h)),
        out_shape=jax.ShapeDtypeStruct((nb, s, nh * MLA_V), BF16),
        scratch_shapes=[pltpu.VMEM((2 * hb, FLASH_T, FLASH_T), F32), pltpu.VMEM((2 * hb, FLASH_T, FLASH_T), BF16),
                        pltpu.VMEM((hb, FLASH_T, LANES), F32), pltpu.VMEM((2 * hb, FLASH_T, LANES), F32),
                        pltpu.VMEM((hb, FLASH_T, 2 * MLA_V), F32)],
        compiler_params=_params("parallel", "parallel", "arbitrary"),
        name="flash",
    )(q, k, v)


def _cumsum_rows(x, ltri):
    hi = x.astype(BF16)
    r1 = x - hi.astype(F32)
    mid = r1.astype(BF16)
    lo = (r1 - mid.astype(F32)).astype(BF16)
    return _dot(ltri, hi) + _dot(ltri, mid) + _dot(ltri, lo)


def _hgrn_kernel(q_ref, f_ref, i_ref, g_ref, lb_ref, on_ref, ones_ref, ltri_ref, o_ref, st_ref):
    @pl.when(pl.program_id(1) == 0)
    def _():
        st_ref[...] = jnp.zeros_like(st_ref)

    for n in range(q_ref.shape[0]):
        _hgrn_chunk(n, q_ref, f_ref, i_ref, g_ref, lb_ref, on_ref, ones_ref, ltri_ref, o_ref, st_ref)


def _hgrn_chunk(n, q_ref, f_ref, i_ref, g_ref, lb_ref, on_ref, ones_ref, ltri_ref, o_ref, st_ref):
    c, w = HG_CHUNK, HG_WIDTH
    z = f_ref[n]
    lb = lb_ref[...]
    e = jnp.exp(-jnp.abs(z))
    r = 1.0 / (1.0 + e)
    pos = z >= 0.0
    sig_z = jnp.where(pos, r, e * r)
    sig_mz = jnp.where(pos, e * r, r)
    kk = (1.0 - lb) * sig_mz
    q = _silu(q_ref[n])
    v = i_ref[n]
    gate = _silu(g_ref[n])
    f = lb + (1.0 - lb) * sig_z
    b = _cumsum_rows(jnp.log2(f), ltri_ref[...])
    row = lax.broadcasted_iota(jnp.int32, (c, w), 0)

    q_lvls, k_lvls = [], []
    for m in HG_LEVELS:
        half = m // 2
        b4 = b.reshape(c // m, 2, half, w)
        ref_b = b4[:, 0, half - 1:half, :]
        k_first = kk.reshape(c // m, 2, half, w)[:, 0] * jnp.exp2(ref_b - b4[:, 0])
        q_second = q.reshape(c // m, 2, half, w)[:, 1] * jnp.exp2(b4[:, 1] - ref_b)
        zero = jnp.zeros_like(k_first)
        q_lvls.append(jnp.stack([zero, q_second], axis=1).reshape(c, w).astype(BF16))
        k_lvls.append(jnp.stack([k_first, zero], axis=1).reshape(c, w).astype(BF16))

    pos8 = row & (SUBLANES - 1)

    def back1(x):
        x3 = x.reshape(c // SUBLANES, SUBLANES, w)
        return pltpu.roll(x3, 1, 1).reshape(c, w)

    f_in = jnp.where(pos8 >= 1, f, 0.0)
    kd_dl, vr = kk, v
    o_diag = jnp.zeros((c, w), F32)
    for dl in range(SUBLANES):
        if dl > 0:
            kd_dl = f_in * back1(kd_dl)
            vr = back1(vr)
        wb = (q * kd_dl).astype(BF16)
        a = jnp.concatenate([_dot(wb[:, :MXU_DIM], ones_ref[...]), _dot(wb[:, MXU_DIM:], ones_ref[...])], axis=1)
        o_diag = o_diag + a * vr

    tt = lax.broadcasted_iota(jnp.int32, (c, c), 0)
    ss = lax.broadcasted_iota(jnp.int32, (c, c), 1)
    same_blk = {m: (tt // m) == (ss // m) for m in HG_LEVELS}

    b_last = b[c - 1:c, :]
    qe = (q * jnp.exp2(b)).astype(BF16)
    kd = (kk * jnp.exp2(b_last - b)).astype(BF16)
    decay = jnp.exp2(b_last)
    vb = v.astype(BF16)
    for h in range(HG_HEADS):
        hs = slice(h * HG_DK, (h + 1) * HG_DK)
        st = st_ref[n, h]
        attn = jnp.zeros((c, c), F32)
        for m, ql, kl in zip(HG_LEVELS, q_lvls, k_lvls):
            part = _dot_nt(ql[:, hs], kl[:, hs])
            attn = attn + (part if m == c else jnp.where(same_blk[m], part, 0.0))
        o_h =_dot_nt(qe[:, hs], st.astype(BF16)) + _dot(attn.astype(BF16), vb[:, hs]) + o_diag[:, hs]
        st_ref[n, h] = st * decay[:, hs] + _dot_tn(vb[:, hs], kd[:, hs])
        o_ref[n, :, hs] = (_rms(o_h, on_ref[...]) * gate[:, hs]).astype(o_ref.dtype)


def _hgrn(proj, lb, onorm):
    nb, s, _ = proj.shape
    head_id = jnp.arange(MXU_DIM) // HG_DK
    ones_blk = (head_id[:, None] == head_id[None, :]).astype(BF16)
    t_id = jnp.arange(HG_CHUNK)
    ltri = (t_id[:, None] >= t_id[None, :]).astype(BF16)
    ns = HG_SEQS if nb % HG_SEQS == 0 else 1
    spec = lambda cb: pl.BlockSpec((ns, HG_CHUNK, COL_BLK), lambda b, i: (b, i, cb))
    return pl.pallas_call(
        _hgrn_kernel,
        grid=(nb // ns, s // HG_CHUNK),
        in_specs=[spec(3), spec(4), spec(5), spec(6),
                  pl.BlockSpec(lb.shape, lambda b, i: (0, 0)),
                  pl.BlockSpec(onorm.shape, lambda b, i: (0, 0)),
                  pl.BlockSpec(ones_blk.shape, lambda b, i: (0, 0)),
                  pl.BlockSpec(ltri.shape, lambda b, i: (0, 0))],
        out_specs=pl.BlockSpec((ns, HG_CHUNK, HG_WIDTH), lambda b, i: (b, i, 0)),
        out_shape=jax.ShapeDtypeStruct((nb, s, HG_WIDTH), BF16),
        scratch_shapes=[pltpu.VMEM((ns, HG_HEADS, HG_DV, HG_DK), F32)],
        compiler_params=_params("parallel", "arbitrary"),
        name="hgrn",
    )(proj, proj, proj, proj, lb, onorm, ones_blk, ltri)


def _out_proj_kernel(ys_ref, ym_ref, yh_ref, w_ref, x_ref, gain_ref, g_ref, o_ref, slab_ref):
    for t in range(S5_LS):
        for c in range(S5_WIDTH // LANES):
            col = t * S5_WIDTH + c * LANES
            slab_ref[c, pl.ds(t, OUT_TM // S5_LS, stride=S5_LS), :] = ys_ref[0, :, col:col + LANES]
    ys = jnp.concatenate([slab_ref[c] for c in range(S5_WIDTH // LANES)], axis=1).astype(BF16)
    mixed = (_dot(ys, w_ref[:S5_WIDTH, :])
             + _dot(ym_ref[0], w_ref[S5_WIDTH:S5_WIDTH + MLA_WIDTH, :])
             + _dot(yh_ref[0], w_ref[S5_WIDTH + MLA_WIDTH:, :]))
    o_ref[0] = x_ref[0] + g_ref[0] * _rms(mixed, gain_ref[...])


def _out_proj(y_s5_rows, y_mla, y_hg, w, l, x, gain, g):
    nb, s, d = x.shape
    row = lambda n: pl.BlockSpec((1, OUT_TM, n), lambda b, i: (b, i, 0))
    return pl.pallas_call(
        _out_proj_kernel,
        grid=(nb, s // OUT_TM),
        in_specs=[pl.BlockSpec((1, OUT_TM // S5_LS, S5_LS * S5_WIDTH), lambda b, i: (b, i, 0)),
                  row(MLA_WIDTH), row(HG_WIDTH),
                  pl.BlockSpec((None,) + w.shape[1:], lambda b, i: (l, 0, 0)),
                  row(d),
                  pl.BlockSpec((1, d), lambda b, i: (0, 0)),
                  pl.BlockSpec((1, 1, d), lambda b, i: (b, 0, 0))],
        out_specs=row(d),
        out_shape=jax.ShapeDtypeStruct((nb, s, d), F32),
        scratch_shapes=[pltpu.VMEM((S5_WIDTH // LANES, OUT_TM, LANES), F32)],
        compiler_params=_params("parallel", "parallel"),
        name="out_proj",
    )(y_s5_rows, y_mla, y_hg, w, x, gain, g)


def _ffn_kernel(x_ref, sc_ref, sh_ref, g_ref, pre_ref, post_ref,
                wg_ref, wv_ref, cwg_ref, cbg_ref, cwv_ref, cbv_ref, wd_ref, o_ref,
                h_ref, acc_ref, ug_ref, uv_ref, act_ref, tail_ref):
    i = pl.program_id(1)
    j = pl.program_id(2)

    @pl.when(j == 0)
    def _():
        scale = pre_ref[...] * (1.0 + sc_ref[0])
        shift = sh_ref[0]
        for r0 in range(0, FFN_TM, NORM_ROWS):
            x = x_ref[0, r0:r0 + NORM_ROWS, :]
            inv = lax.rsqrt(jnp.mean(x * x, axis=-1, keepdims=True) + EPS)
            h_ref[r0:r0 + NORM_ROWS, :] = (x * inv * scale + shift).astype(BF16)
        acc_ref[...] = jnp.zeros_like(acc_ref)

    def up():
        for u_ref, w_ref, t in ((ug_ref, wg_ref, 0), (uv_ref, wv_ref, 1)):
            u_ref[:HALO, :] = jnp.where(i == 0, 0.0, tail_ref[j, t])
            u = _dot(h_ref[...], w_ref[...])
            u_ref[HALO:, :] = u
            tail_ref[j, t] = u[FFN_TM - HALO:, :]

    def conv(u_ref, cw_ref, cb_ref, r0):
        return (cb_ref[...] + u_ref[r0 + HALO - 2:r0 + HALO - 2 + FFN_ROWS, :] * cw_ref[0:1, :]
                + u_ref[r0 + HALO - 1:r0 + HALO - 1 + FFN_ROWS, :] * cw_ref[1:2, :]
                + u_ref[r0 + HALO:r0 + HALO + FFN_ROWS, :] * cw_ref[2:3, :])

    up()
    for r0 in range(0, FFN_TM, FFN_ROWS):
        gate = conv(ug_ref, cwg_ref, cbg_ref, r0)
        val = conv(uv_ref, cwv_ref, cbv_ref, r0)
        act_ref[r0:r0 + FFN_ROWS, :] = (_gelu_tanh(gate) * val).astype(BF16)
    acc_ref[...] += _dot(act_ref[...], wd_ref[...])

    @pl.when(j == pl.num_programs(2) - 1)
    def _():
        scale = post_ref[...] * g_ref[0]
        for r0 in range(0, FFN_TM, NORM_ROWS):
            y = acc_ref[r0:r0 + NORM_ROWS, :]
            inv = lax.rsqrt(jnp.mean(y * y, axis=-1, keepdims=True) + EPS)
            o_ref[0, r0:r0 + NORM_ROWS, :] = x_ref[0, r0:r0 + NORM_ROWS, :] + y * inv * scale


def _ffn(x, sc, sh, g, pre, post, w_up, conv_w, conv_b, w_down, l):
    nb, s, d = x.shape
    ff = w_down.shape[1]
    nf = ff // FFN_TF
    vec = pl.BlockSpec((1, 1, d), lambda b, i, j: (b, 0, 0))
    par = pl.BlockSpec((1, d), lambda b, i, j: (0, 0))
    gate = lambda r: pl.BlockSpec((None, r, FFN_TF), lambda b, i, j: (l, 0, j))
    val = lambda r: pl.BlockSpec((None, r, FFN_TF), lambda b, i, j: (l, 0, j + nf))
    return pl.pallas_call(
        _ffn_kernel,
        grid=(nb, s // FFN_TM, nf),
        in_specs=[
            pl.BlockSpec((1, FFN_TM, d), lambda b, i, j: (b, i, 0)),
            vec, vec, vec, par, par,
            gate(d), val(d), gate(CONV_W), gate(1), val(CONV_W), val(1),
            pl.BlockSpec((None, FFN_TF, d), lambda b, i, j: (l, j, 0)),
        ],
        out_specs=pl.BlockSpec((1, FFN_TM, d), lambda b, i, j: (b, i, 0)),
        out_shape=jax.ShapeDtypeStruct((nb, s, d), F32),
        scratch_shapes=[pltpu.VMEM((FFN_TM, d), BF16), pltpu.VMEM((FFN_TM, d), F32),
                        pltpu.VMEM((FFN_TM + HALO, FFN_TF), F32), pltpu.VMEM((FFN_TM + HALO, FFN_TF), F32),
                        pltpu.VMEM((FFN_TM, FFN_TF), BF16), pltpu.VMEM((nf, 2, HALO, FFN_TF), F32)],
        compiler_params=_params("parallel", "arbitrary", "arbitrary"),
        name="ffn",
    )(x, sc, sh, g, pre, post, w_up, w_up, conv_w, conv_b, conv_w, conv_b, w_down)


def _pad_cols(a, n):
    return jnp.pad(a, [(0, 0)] * (a.ndim - 1) + [(0, n - a.shape[-1])])


def _cast_halves_kernel(w_ref, o_ref):
    o_ref[0, :, :D_FF] = w_ref[0].astype(BF16)
    o_ref[0, :, D_FF:] = jnp.zeros((o_ref.shape[1], D_FF_PAD - D_FF), BF16)


def _cast_w_up(w_up):
    nl, d, _ = w_up.shape
    return pl.pallas_call(
        _cast_halves_kernel,
        grid=(nl, d // CAST_TM, 2),
        in_specs=[pl.BlockSpec((1, CAST_TM, D_FF), lambda l, i, h: (l, i, h))],
        out_specs=pl.BlockSpec((1, CAST_TM, D_FF_PAD), lambda l, i, h: (l, i, h)),
        out_shape=jax.ShapeDtypeStruct((nl, d, 2 * D_FF_PAD), BF16),
        compiler_params=_params("parallel", "parallel", "parallel"),
        name="cast_w_up",
    )(w_up)


def _layout_w_in(w_in):
    o = 0
    parts = []
    for width in (S5_WIDTH, MLA_Q_RANK, MLA_KV_RANK, MLA_ROPE):
        parts.append(w_in[..., o:o + width])
        o += width
    parts.append(jnp.zeros(w_in.shape[:-1] + (COL_BLK - MLA_KV_RANK - MLA_ROPE,), w_in.dtype))
    parts.append(w_in[..., o:])
    return jnp.concatenate(parts, axis=-1).astype(BF16)


def _layout_w_uq(w):
    nl, r, _ = w.shape
    w = w.reshape(nl, r, MLA_HEADS, MLA_NOPE + MLA_ROPE)
    return _pad_cols(w, MLA_QK).reshape(nl, r, MLA_HEADS * MLA_QK).astype(BF16)


def _layout_w_ukv(w):
    nl, r, _ = w.shape
    w = w.reshape(nl, r, MLA_HEADS, MLA_NOPE + MLA_V)
    k = w[..., :MLA_NOPE].reshape(nl, r, MLA_HEADS * MLA_NOPE)
    v = w[..., MLA_NOPE:].reshape(nl, r, MLA_HEADS * MLA_V)
    return jnp.concatenate([k, v], axis=-1).astype(BF16)


def _rope_tables(positions):
    inv_freq = 1.0 / (ROPE_THETA ** (jnp.arange(0, MLA_ROPE, 2, dtype=F32) / MLA_ROPE))
    ang = positions.astype(F32)[..., None] * inv_freq
    cos, sin = jnp.cos(ang), jnp.sin(ang)
    z32 = jnp.zeros_like(cos)
    z64 = jnp.concatenate([z32, z32], axis=-1)
    return (jnp.concatenate([cos, cos, z64], axis=-1),
            jnp.concatenate([z32, sin, z64], axis=-1),
            jnp.concatenate([-sin, z32, z64], axis=-1))


def kernel(x, c, positions, w_in, s5_lambda_re, s5_lambda_im, s5_log_dt, s5_b_re, s5_b_im, s5_c_re, s5_c_im, s5_d, s5_w_glu, mla_q_norm, mla_w_uq, mla_kv_norm, mla_w_ukv, hg_lb_logits, hg_out_norm, w_out, mix_pre_norm, mix_post_norm, ffn_pre_norm, ffn_post_norm, ffn_w_up, ffn_conv_w, ffn_conv_b, ffn_w_down, w_ada, b_ada):
    nb, s, d = x.shape
    nl = w_in.shape[0]
    assert s % FLASH_T == 0 and s % FFN_TM == 0 and s % (S5_LS * SUBLANES) == 0

    mod = _adaln(c, w_ada, b_ada)
    cos_t, s1_t, s2_t = _rope_tables(positions)
    probs = jax.nn.softmax(hg_lb_logits.astype(F32), axis=0)
    lower_bounds = jnp.cumsum(probs, axis=0) - probs[0:1]

    w_in_p = _layout_w_in(w_in)
    w_uq_p = _layout_w_uq(mla_w_uq)
    w_ukv_p = _layout_w_ukv(mla_w_ukv)
    w_out_b = w_out.astype(BF16)
    w_glu_b = s5_w_glu.astype(BF16)
    def pad_halves(a):
        return jnp.concatenate([_pad_cols(a[..., :D_FF], D_FF_PAD), _pad_cols(a[..., D_FF:], D_FF_PAD)], axis=-1)

    w_up_p = _cast_w_up(ffn_w_up)
    conv_w_p = pad_halves(ffn_conv_w)
    conv_b_p = pad_halves(ffn_conv_b)[:, None, :]
    w_down = jnp.pad(ffn_w_down, ((0, 0), (0, D_FF_PAD - D_FF), (0, 0))).astype(BF16)

    for l in range(nl):
        sh1, sc1, g1, sh2, sc2, g2 = [m[:, None, :] for m in jnp.split(mod[l], 6, axis=-1)]
        proj, u_rows = _proj_in(x, sc1, sh1, mix_pre_norm[l][None, :], w_in_p, l)
        bm, cm, lam, lam_chunk = _s5_prep(s5_lambda_re[l], s5_lambda_im[l], s5_log_dt[l],
                                          s5_b_re[l], s5_b_im[l], s5_c_re[l], s5_c_im[l])
        y_s5 = _s5(u_rows, bm, cm, lam, lam_chunk, s5_d[l][None, :], w_glu_b[l])
        q, k, v = _mla_prep(proj, mla_q_norm[l][None, :], mla_kv_norm[l][None, :], w_uq_p[l], w_ukv_p[l],
                            cos_t, s1_t, s2_t)
        y_mla = _flash(q, k, v)
        y_hg = _hgrn(proj, lower_bounds[l][None, :], hg_out_norm[l][None, :])
        x = _out_proj(y_s5, y_mla, y_hg, w_out_b, l, x, mix_post_norm[l][None, :], g1)
        x = _ffn(x, sc2, sh2, g2, ffn_pre_norm[l][None, :], ffn_post_norm[l][None, :],
                 w_up_p, conv_w_p, conv_b_p, w_down, l)
    return x
```

```python
import functools
import math

import jax
import jax.numpy as jnp
from jax import lax
from jax.experimental import pallas as pl
from jax.experimental.pallas import tpu as pltpu

F32 = jnp.float32
BF16 = jnp.bfloat16

D_MODEL = 2048
DEPTH = 4
S5_WIDTH = 512
S5_GROUP = 16
S5_GROUPS = 32
S5_STATE = 64
MLA_HEADS = 8
MLA_NOPE = 128
MLA_ROPE = 64
MLA_V = 128
MLA_Q_RANK = 512
MLA_KV_RANK = 256
MLA_WIDTH = MLA_HEADS * MLA_V
ROPE_THETA = 10000.0
MASK_VALUE = -1e30
HG_HEADS = 4
HG_DK = 128
HG_DV = 128
HG_WIDTH = HG_HEADS * HG_DV
D_FF = 5504
CONV_W = 3
EPS = 1e-6

LANES = 128
SUBLANES = 8
MXU_DIM = 256
VMEM_LIMIT_BYTES = 56 * 1024 * 1024

COL_BLK = 512
N_COL_BLKS = 7
D_IN_PAD = N_COL_BLKS * COL_BLK
MLA_QK = 2 * LANES
D_FF_PAD = 5632

ADA_TN = 1024
PROJ_TM = 512
S5_LS = 16
S5_HALF = S5_WIDTH // 2
S5_HSTATE = (S5_GROUPS // 2) * S5_STATE
PREP_TM = 512
FLASH_T = 512
FLASH_HEADS = 2
FLASH_ROWS = 32
HG_CHUNK = 128
HG_LEVELS = (16, 32, 64, 128)
HG_SEQS = 2
OUT_TM = 512
FFN_TM = 512
FFN_TF = 512
FFN_ROWS = 64
FFN_PARTS = 2
NORM_ROWS = 16
CAST_TM = 256
HALO = SUBLANES


def _sigmoid(x):
    return 1.0 / (1.0 + jnp.exp(-x))


def _silu(x):
    return x * _sigmoid(x)


def _gelu_tanh(x):
    c = math.sqrt(2.0 / math.pi)
    return x * (0.5 * (1.0 + jnp.tanh(c * (x + 0.044715 * (x * x * x)))))


def _rms(x, gain):
    return x * lax.rsqrt(jnp.mean(x * x, axis=-1, keepdims=True) + EPS) * gain


def _dot(a, b):
    return jnp.dot(a, b, preferred_element_type=F32)


def _dot_nt(a, b):
    return lax.dot_general(a, b, (((1,), (1,)), ((), ())), preferred_element_type=F32)


def _dot_tn(a, b):
    return lax.dot_general(a, b, (((0,), (0,)), ((), ())), preferred_element_type=F32)


def _params(*sem):
    return pltpu.CompilerParams(dimension_semantics=sem, vmem_limit_bytes=VMEM_LIMIT_BYTES)


def _adaln_kernel(c_ref, w_ref, b_ref, o_ref):
    ca = _silu(c_ref[...]).astype(BF16)
    o_ref[0] = _dot(ca, w_ref[0].astype(BF16)) + b_ref[0]


def _adaln(c, w_ada, b_ada):
    nb, d = c.shape
    nl, _, n = w_ada.shape
    rows = -(-nb // SUBLANES) * SUBLANES
    c_pad = jnp.pad(c, ((0, rows - nb), (0, 0)))
    out = pl.pallas_call(
        _adaln_kernel,
        grid=(nl, n // ADA_TN),
        in_specs=[
            pl.BlockSpec((rows, d), lambda l, j: (0, 0)),
            pl.BlockSpec((1, d, ADA_TN), lambda l, j: (l, 0, j)),
            pl.BlockSpec((1, 1, ADA_TN), lambda l, j: (l, 0, j)),
        ],
        out_specs=pl.BlockSpec((1, rows, ADA_TN), lambda l, j: (l, 0, j)),
        out_shape=jax.ShapeDtypeStruct((nl, rows, n), F32),
        compiler_params=_params("parallel", "parallel"),
        name="adaln",
    )(c_pad, w_ada, b_ada.reshape(nl, 1, n))
    return out[:, :nb]


def _proj_in_kernel(x_ref, sc_ref, sh_ref, gain_ref, w_ref, o_ref, u_ref, h_ref, slab_ref):
    scale = gain_ref[...] * (1.0 + sc_ref[0])
    shift = sh_ref[0]
    part = PROJ_TM // 2
    for p0 in range(0, PROJ_TM, part):
        for r0 in range(p0, p0 + part, NORM_ROWS):
            x = x_ref[0, r0:r0 + NORM_ROWS, :]
            inv = lax.rsqrt(jnp.mean(x * x, axis=-1, keepdims=True) + EPS)
            h_ref[r0:r0 + NORM_ROWS, :] = (x * inv * scale + shift).astype(BF16)
        res = _dot(h_ref[p0:p0 + part, :], w_ref[...])
        o_ref[0, p0:p0 + part, :] = res
        for c in range(S5_WIDTH // LANES):
            slab_ref[c, p0:p0 + part, :] = res[:, c * LANES:(c + 1) * LANES]

    for t in range(S5_LS):
        for c in range(S5_WIDTH // LANES):
            col = t * S5_WIDTH + c * LANES
            u_ref[0, :, col:col + LANES] = slab_ref[c, pl.ds(t, PROJ_TM // S5_LS, stride=S5_LS), :]


def _proj_in(x, sc, sh, gain, w, l):
    nb, s, d = x.shape
    n = w.shape[2]
    rows = PROJ_TM // S5_LS
    return pl.pallas_call(
        _proj_in_kernel,
        grid=(nb, s // PROJ_TM),
        in_specs=[
            pl.BlockSpec((1, PROJ_TM, d), lambda b, i: (b, i, 0)),
            pl.BlockSpec((1, 1, d), lambda b, i: (b, 0, 0)),
            pl.BlockSpec((1, 1, d), lambda b, i: (b, 0, 0)),
            pl.BlockSpec((1, d), lambda b, i: (0, 0)),
            pl.BlockSpec((None, d, n), lambda b, i: (l, 0, 0), pipeline_mode=pl.Buffered(1)),
        ],
        out_specs=[pl.BlockSpec((1, PROJ_TM, n), lambda b, i: (b, i, 0)),
                   pl.BlockSpec((1, rows, S5_LS * S5_WIDTH), lambda b, i: (b, i, 0))],
        out_shape=[jax.ShapeDtypeStruct((nb, s, n), F32),
                   jax.ShapeDtypeStruct((nb, s // S5_LS, S5_LS * S5_WIDTH), F32)],
        scratch_shapes=[pltpu.VMEM((PROJ_TM, d), BF16), pltpu.VMEM((S5_WIDTH // LANES, PROJ_TM, LANES), F32)],
        compiler_params=_params("parallel", "parallel"),
        name="proj_in",
    )(x, sc, sh, gain, w)


def _s5_kernel(u_ref, bm_ref, cm_ref, lam_ref, lam_chunk_ref, d_ref, wglu_ref, o_ref,
               sre, sim, hre, him, yloc):
    ph = pl.program_id(1)
    s = pl.program_id(2)
    kc = sre.shape[1]

    @pl.when(ph == 0)
    def _():
        @pl.when(s == 0)
        def _():
            sre[...] = jnp.zeros_like(sre)
            sim[...] = jnp.zeros_like(sim)

        u = u_ref[0]
        ub = u.astype(BF16)
        for h in range(2):
            cols = slice(h * S5_HALF, (h + 1) * S5_HALF)
            v = _dot(ub[:, cols], bm_ref[h])
            lr = lam_ref[h:h + 1, :]
            li = lam_ref[2 + h:3 + h, :]
            re = sre[h]
            im = sim[h]
            nre = lr * re - li * im + v[:, :S5_HSTATE]
            nim = lr * im + li * re + v[:, S5_HSTATE:]
            sre[h] = nre
            sim[h] = nim
            yl = _dot(nre.astype(BF16), cm_ref[h, 0]) + _dot(nim.astype(BF16), cm_ref[h, 1])
            yloc[s, :, cols] = yl + d_ref[:, cols] * u[:, cols]

        @pl.when(s == S5_LS - 1)
        def _():
            def body(k, carry):
                new = []
                for h in range(2):
                    cre, cim = carry[2 * h], carry[2 * h + 1]
                    hre[h, pl.ds(k, 1), :] = cre
                    him[h, pl.ds(k, 1), :] = cim
                    lr = lam_chunk_ref[h:h + 1, :]
                    li = lam_chunk_ref[2 + h:3 + h, :]
                    new.append(lr * cre - li * cim + sre[h, pl.ds(k, 1), :])
                    new.append(lr * cim + li * cre + sim[h, pl.ds(k, 1), :])
                return tuple(new)

            zero = jnp.zeros((1, S5_HSTATE), F32)
            lax.fori_loop(0, kc, body, (zero, zero, zero, zero))

    @pl.when(ph == 1)
    def _():
        ys = []
        for h in range(2):
            cols = slice(h * S5_HALF, (h + 1) * S5_HALF)
            lr = lam_ref[h:h + 1, :]
            li = lam_ref[2 + h:3 + h, :]
            pre = hre[h]
            pim = him[h]
            nre = lr * pre - li * pim
            nim = lr * pim + li * pre
            hre[h] = nre
            him[h] = nim
            yc = _dot(nre.astype(BF16), cm_ref[h, 0]) + _dot(nim.astype(BF16), cm_ref[h, 1])
            ys.append(yloc[s, :, cols] + yc)
        g = _gelu_tanh(jnp.concatenate(ys, axis=1))
        gate = _sigmoid(_dot(g.astype(BF16), wglu_ref[...]))
        o_ref[0] = (g * gate).astype(o_ref.dtype)


def _s5(u_rows, bm, cm, lam, lam_chunk, d, wglu):
    nb, kc, _ = u_rows.shape
    return pl.pallas_call(
        _s5_kernel,
        grid=(nb, 2, S5_LS),
        in_specs=[
            pl.BlockSpec((1, kc, S5_WIDTH),
                         lambda b, p, t: (b, 0, jnp.where(p == 0, t, S5_LS - 1))),
            pl.BlockSpec(bm.shape, lambda b, p, t: (0, 0, 0)),
            pl.BlockSpec(cm.shape, lambda b, p, t: (0, 0, 0, 0)),
            pl.BlockSpec(lam.shape, lambda b, p, t: (0, 0)),
            pl.BlockSpec(lam_chunk.shape, lambda b, p, t: (0, 0)),
            pl.BlockSpec(d.shape, lambda b, p, t: (0, 0)),
            pl.BlockSpec(wglu.shape, lambda b, p, t: (0, 0)),
        ],
        out_specs=pl.BlockSpec((1, kc, S5_WIDTH), lambda b, p, t: (b, 0, t * p)),
        out_shape=jax.ShapeDtypeStruct((nb, kc, S5_LS * S5_WIDTH), F32),
        scratch_shapes=[pltpu.VMEM((2, kc, S5_HSTATE), F32)] * 4
        + [pltpu.VMEM((S5_LS, kc, S5_WIDTH), F32)],
        compiler_params=_params("parallel", "arbitrary", "arbitrary"),
        name="s5",
    )(u_rows, bm, cm, lam, lam_chunk, d, wglu)


def _s5_prep(lam_re, lam_im, log_dt, b_re, b_im, c_re, c_im):
    lam_re, lam_im = lam_re.astype(F32), lam_im.astype(F32)
    dt = jnp.exp(log_dt.astype(F32))[:, None]
    mag = jnp.exp(lam_re * dt)
    bar_re, bar_im = mag * jnp.cos(lam_im * dt), mag * jnp.sin(lam_im * dt)
    den = lam_re * lam_re + lam_im * lam_im
    co_re = ((bar_re - 1.0) * lam_re + bar_im * lam_im) / den
    co_im = (bar_im * lam_re - (bar_re - 1.0) * lam_im) / den
    b_re, b_im = b_re.astype(F32), b_im.astype(F32)
    bb_re = co_re[..., None] * b_re - co_im[..., None] * b_im
    bb_im = co_re[..., None] * b_im + co_im[..., None] * b_re
    ch_re, ch_im = bar_re, bar_im
    for _ in range(int(math.log2(S5_LS))):
        ch_re, ch_im = ch_re * ch_re - ch_im * ch_im, 2.0 * ch_re * ch_im
    gh = S5_GROUPS // 2
    eye = jnp.eye(gh, dtype=F32)

    def b_tiles(part):
        t = part.reshape(2, gh, S5_STATE, S5_GROUP)
        return jnp.einsum('ab,hapc->hacbp', eye, t).reshape(2, S5_HALF, S5_HSTATE)

    def c_tiles(part):
        t = part.reshape(2, gh, S5_GROUP, S5_STATE)
        return jnp.einsum('ab,hbcp->hapbc', eye, t).reshape(2, S5_HSTATE, S5_HALF)

    bm = jnp.concatenate([b_tiles(bb_re), b_tiles(bb_im)], axis=-1).astype(BF16)
    cm = jnp.stack([c_tiles(c_re.astype(F32)), -c_tiles(c_im.astype(F32))], axis=1).astype(BF16)

    def rows(re, im):
        return jnp.concatenate([re.reshape(2, S5_HSTATE), im.reshape(2, S5_HSTATE)], 0)

    return bm, cm, rows(bar_re, bar_im), rows(ch_re, ch_im)


def _mla_prep_kernel(cq_ref, ckv_ref, qn_ref, kvn_ref, wq_ref, wkv_ref, cos_ref, s1_ref, s2_ref,
                     q_ref, k_ref, v_ref):
    cos_t = cos_ref[0]
    s1 = s1_ref[0]
    s2 = s2_ref[0]
    half = MLA_ROPE // 2

    def rope(x):
        return x * cos_t + pltpu.roll(x, half, 1) * s1 + pltpu.roll(x, LANES - half, 1) * s2

    scale = (MLA_NOPE + MLA_ROPE) ** -0.5 * math.log2(math.e)
    q = _dot(_rms(cq_ref[0], qn_ref[...]).astype(BF16), wq_ref[...])
    blk = ckv_ref[0]
    kv = _dot(_rms(blk[:, :MLA_KV_RANK], kvn_ref[...]).astype(BF16), wkv_ref[...])
    kr = rope(blk[:, MLA_KV_RANK:MLA_KV_RANK + LANES]).astype(BF16)
    for h in range(MLA_HEADS):
        qn = q[:, h * MLA_QK:h * MLA_QK + MLA_NOPE] * scale
        qr = rope(q[:, h * MLA_QK + MLA_NOPE:(h + 1) * MLA_QK]) * scale
        q_ref[0, h] = jnp.concatenate([qn, qr], axis=1).astype(BF16)
        k_ref[0, h] = jnp.concatenate([kv[:, h * MLA_NOPE:(h + 1) * MLA_NOPE].astype(BF16), kr], axis=1)
        vh = kv[:, MLA_WIDTH + h * MLA_V:MLA_WIDTH + (h + 1) * MLA_V]
        v_ref[0, h] = jnp.concatenate([vh, jnp.ones_like(vh)], axis=1).astype(BF16)


def _mla_prep(proj, qn, kvn, wq, wkv, cos_t, s1_t, s2_t):
    nb, s, _ = proj.shape
    tab = pl.BlockSpec((1, PREP_TM, LANES), lambda b, i: (b, i, 0))
    return pl.pallas_call(
        _mla_prep_kernel,
        grid=(nb, s // PREP_TM),
        in_specs=[
            pl.BlockSpec((1, PREP_TM, COL_BLK), lambda b, i: (b, i, 1)),
            pl.BlockSpec((1, PREP_TM, COL_BLK), lambda b, i: (b, i, 2)),
            pl.BlockSpec(qn.shape, lambda b, i: (0, 0)),
            pl.BlockSpec(kvn.shape, lambda b, i: (0, 0)),
            pl.BlockSpec(wq.shape, lambda b, i: (0, 0)),
            pl.BlockSpec(wkv.shape, lambda b, i: (0, 0)),
            tab, tab, tab,
        ],
        out_specs=[
            pl.BlockSpec((1, MLA_HEADS, PREP_TM, MLA_QK), lambda b, i: (b, 0, i, 0)),
            pl.BlockSpec((1, MLA_HEADS, PREP_TM, MLA_QK), lambda b, i: (b, 0, i, 0)),
            pl.BlockSpec((1, MLA_HEADS, PREP_TM, 2 * MLA_V), lambda b, i: (b, 0, i, 0)),
        ],
        out_shape=[
            jax.ShapeDtypeStruct((nb, MLA_HEADS, s, MLA_QK), BF16),
            jax.ShapeDtypeStruct((nb, MLA_HEADS, s, MLA_QK), BF16),
            jax.ShapeDtypeStruct((nb, MLA_HEADS, s, 2 * MLA_V), BF16),
        ],
        compiler_params=_params("parallel", "parallel"),
        name="mla_prep",
    )(proj, proj, qn, kvn, wq, wkv, cos_t, s1_t, s2_t)


def _flash_kernel(q_ref, k_ref, v_ref, o_ref, s_ref, p_ref, m_ref, a_ref, acc_ref):
    qi = pl.program_id(2)
    t = FLASH_T
    heads = range(FLASH_HEADS)
    nh = FLASH_HEADS
    m_ref[...] = jnp.full(m_ref.shape, -jnp.inf, F32)
    acc_ref[...] = jnp.zeros_like(acc_ref)
    p_ref[nh:] = jnp.zeros((nh,) + p_ref.shape[1:], BF16)
    a_ref[nh:] = jnp.ones((nh,) + a_ref.shape[1:], F32)

    def scores(j, slot):
        start = pl.multiple_of(j * t, t)
        for h in heads:
            s_ref[slot * nh + h] = _dot_nt(q_ref[0, h], k_ref[0, h, pl.ds(start, t), :])

    def softmax(slot, diagonal):
        for h in heads:
            for r0 in range(0, t, FLASH_ROWS):
                rows = slice(r0, r0 + FLASH_ROWS)
                sc = s_ref[slot * nh + h, rows, :]
                if diagonal:
                    row = lax.broadcasted_iota(jnp.int32, sc.shape, 0) + r0
                    col = lax.broadcasted_iota(jnp.int32, sc.shape, 1)
                    sc = jnp.where(col <= row, sc, MASK_VALUE)
                m_old = m_ref[h, rows, :]
                m_new = jnp.maximum(m_old, jnp.max(sc, axis=-1, keepdims=True))
                p_ref[slot * nh + h, rows, :] = jnp.exp2(sc - m_new[:, :1]).astype(BF16)
                a_ref[slot * nh + h, rows, :] = jnp.exp2(m_old - m_new)
                m_ref[h, rows, :] = m_new

    def accumulate(j, slot):
        start = pl.multiple_of(j * t, t)
        for h in heads:
            pv = _dot(p_ref[slot * nh + h], v_ref[0, h, pl.ds(start, t), :])
            alpha = a_ref[slot * nh + h]
            acc_ref[h] = jnp.concatenate([alpha, alpha], axis=1) * acc_ref[h] + pv

    def stage(j, slot, with_accumulate=True):
        softmax(slot, False)
        scores(j + 1, 1 - slot)
        if with_accumulate:
            accumulate(jnp.maximum(j - 1, 0), 1 - slot)

    odd = lax.rem(qi, 2)
    for slot in range(2):
        @pl.when(odd == slot)
        def _():
            scores(0, slot)

    @pl.when(odd == 1)
    def _():
        stage(0, 1, with_accumulate=False)

    def body(jj, carry):
        j = 2 * jj + odd
        stage(j, 0)
        stage(j + 1, 1)
        return carry

    lax.fori_loop(0, qi // 2, body, 0)
    softmax(0, True)
    accumulate(jnp.maximum(qi - 1, 0), 1)
    accumulate(qi, 0)

    for h in heads:
        acc = acc_ref[h]
        o_ref[0, :, h * MLA_V:(h + 1) * MLA_V] = (acc[:, :MLA_V] / acc[:, MLA_V:]).astype(o_ref.dtype)


def _flash(q, k, v):
    nb, nh, s, _ = q.shape
    hb = FLASH_HEADS
    return pl.pallas_call(
        _flash_kernel,
        grid=(nb, nh // hb, s // FLASH_T),
        in_specs=[
            pl.BlockSpec((1, hb, FLASH_T, MLA_QK), lambda b, h, i: (b, h, i, 0)),
            pl.BlockSpec((1, hb, s, MLA_QK), lambda b, h, i: (b, h, 0, 0)),
            pl.BlockSpec((1, hb, s, 2 * MLA_V), lambda b, h, i: (b, h, 0, 0)),
        ],
        out_specs=pl.BlockSpec((1, FLASH_T, hb * MLA_V), lambda b, h, i: (b, i, h)),
        out_shape=jax.ShapeDtypeStruct((nb, s, nh * MLA_V), BF16),
        scratch_shapes=[pltpu.VMEM((2 * hb, FLASH_T, FLASH_T), F32), pltpu.VMEM((2 * hb, FLASH_T, FLASH_T), BF16),
                        pltpu.VMEM((hb, FLASH_T, LANES), F32), pltpu.VMEM((2 * hb, FLASH_T, LANES), F32),
                        pltpu.VMEM((hb, FLASH_T, 2 * MLA_V), F32)],
        compiler_params=_params("parallel", "parallel", "arbitrary"),
        name="flash",
    )(q, k, v)


def _cumsum_rows(x, ltri):
    hi = x.astype(BF16)
    r1 = x - hi.astype(F32)
    mid = r1.astype(BF16)
    lo = (r1 - mid.astype(F32)).astype(BF16)
    return _dot(ltri, hi) + _dot(ltri, mid) + _dot(ltri, lo)


def _hgrn_kernel(q_ref, f_ref, i_ref, g_ref, lb_ref, on_ref, ones_ref, ltri_ref, o_ref, st_ref):
    @pl.when(pl.program_id(1) == 0)
    def _():
        st_ref[...] = jnp.zeros_like(st_ref)

    for n in range(q_ref.shape[0]):
        _hgrn_chunk(n, q_ref, f_ref, i_ref, g_ref, lb_ref, on_ref, ones_ref, ltri_ref, o_ref, st_ref)


def _hgrn_chunk(n, q_ref, f_ref, i_ref, g_ref, lb_ref, on_ref, ones_ref, ltri_ref, o_ref, st_ref):
    c, w = HG_CHUNK, HG_WIDTH
    z = f_ref[n]
    lb = lb_ref[...]
    e = jnp.exp(-jnp.abs(z))
    r = 1.0 / (1.0 + e)
    pos = z >= 0.0
    sig_z = jnp.where(pos, r, e * r)
    sig_mz = jnp.where(pos, e * r, r)
    kk = (1.0 - lb) * sig_mz
    q = _silu(q_ref[n])
    v = i_ref[n]
    gate = _silu(g_ref[n])
    f = lb + (1.0 - lb) * sig_z
    b = _cumsum_rows(jnp.log2(f), ltri_ref[...])
    row = lax.broadcasted_iota(jnp.int32, (c, w), 0)

    q_lvls, k_lvls = [], []
    for m in HG_LEVELS:
        half = m // 2
        b4 = b.reshape(c // m, 2, half, w)
        ref_b = b4[:, 0, half - 1:half, :]
        k_first = kk.reshape(c // m, 2, half, w)[:, 0] * jnp.exp2(ref_b - b4[:, 0])
        q_second = q.reshape(c // m, 2, half, w)[:, 1] * jnp.exp2(b4[:, 1] - ref_b)
        zero = jnp.zeros_like(k_first)
        q_lvls.append(jnp.stack([zero, q_second], axis=1).reshape(c, w).astype(BF16))
        k_lvls.append(jnp.stack([k_first, zero], axis=1).reshape(c, w).astype(BF16))

    pos8 = row & (SUBLANES - 1)

    def back1(x):
        x3 = x.reshape(c // SUBLANES, SUBLANES, w)
        return pltpu.roll(x3, 1, 1).reshape(c, w)

    f_in = jnp.where(pos8 >= 1, f, 0.0)
    kd_dl, vr = kk, v
    o_diag = jnp.zeros((c, w), F32)
    for dl in range(SUBLANES):
        if dl > 0:
            kd_dl = f_in * back1(kd_dl)
            vr = back1(vr)
        wb = (q * kd_dl).astype(BF16)
        a = jnp.concatenate([_dot(wb[:, :MXU_DIM], ones_ref[...]), _dot(wb[:, MXU_DIM:], ones_ref[...])], axis=1)
        o_diag = o_diag + a * vr

    tt = lax.broadcasted_iota(jnp.int32, (c, c), 0)
    ss = lax.broadcasted_iota(jnp.int32, (c, c), 1)
    same_blk = {m: (tt // m) == (ss // m) for m in HG_LEVELS}

    b_last = b[c - 1:c, :]
    qe = (q * jnp.exp2(b)).astype(BF16)
    kd = (kk * jnp.exp2(b_last - b)).astype(BF16)
    decay = jnp.exp2(b_last)
    vb = v.astype(BF16)
    for h in range(HG_HEADS):
        hs = slice(h * HG_DK, (h + 1) * HG_DK)
        st = st_ref[n, h]
        attn = jnp.zeros((c, c), F32)
        for m, ql, kl in zip(HG_LEVELS, q_lvls, k_lvls):
            part = _dot_nt(ql[:, hs], kl[:, hs])
            attn = attn + (part if m == c else jnp.where(same_blk[m], part, 0.0))
        o_h =_dot_nt(qe[:, hs], st.astype(BF16)) + _dot(attn.astype(BF16), vb[:, hs]) + o_diag[:, hs]
        st_ref[n, h] = st * decay[:, hs] + _dot_tn(vb[:, hs], kd[:, hs])
        o_ref[n, :, hs] = (_rms(o_h, on_ref[...]) * gate[:, hs]).astype(o_ref.dtype)


def _hgrn(proj, lb, onorm):
    nb, s, _ = proj.shape
    head_id = jnp.arange(MXU_DIM) // HG_DK
    ones_blk = (head_id[:, None] == head_id[None, :]).astype(BF16)
    t_id = jnp.arange(HG_CHUNK)
    ltri = (t_id[:, None] >= t_id[None, :]).astype(BF16)
    ns = HG_SEQS if nb % HG_SEQS == 0 else 1
    spec = lambda cb: pl.BlockSpec((ns, HG_CHUNK, COL_BLK), lambda b, i: (b, i, cb))
    return pl.pallas_call(
        _hgrn_kernel,
        grid=(nb // ns, s // HG_CHUNK),
        in_specs=[spec(3), spec(4), spec(5), spec(6),
                  pl.BlockSpec(lb.shape, lambda b, i: (0, 0)),
                  pl.BlockSpec(onorm.shape, lambda b, i: (0, 0)),
                  pl.BlockSpec(ones_blk.shape, lambda b, i: (0, 0)),
                  pl.BlockSpec(ltri.shape, lambda b, i: (0, 0))],
        out_specs=pl.BlockSpec((ns, HG_CHUNK, HG_WIDTH), lambda b, i: (b, i, 0)),
        out_shape=jax.ShapeDtypeStruct((nb, s, HG_WIDTH), BF16),
        scratch_shapes=[pltpu.VMEM((ns, HG_HEADS, HG_DV, HG_DK), F32)],
        compiler_params=_params("parallel", "arbitrary"),
        name="hgrn",
    )(proj, proj, proj, proj, lb, onorm, ones_blk, ltri)


def _out_proj_kernel(ys_ref, ym_ref, yh_ref, w_ref, x_ref, gain_ref, g_ref, o_ref, slab_ref):
    for t in range(S5_LS):
        for c in range(S5_WIDTH // LANES):
            col = t * S5_WIDTH + c * LANES
            slab_ref[c, pl.ds(t, OUT_TM // S5_LS, stride=S5_LS), :] = ys_ref[0, :, col:col + LANES]
    scale = gain_ref[...] * g_ref[0]
    part = OUT_TM // 2
    for p0 in range(0, OUT_TM, part):
        rows = slice(p0, p0 + part)
        ys = jnp.concatenate([slab_ref[c, rows, :] for c in range(S5_WIDTH // LANES)], axis=1).astype(BF16)
        mixed = (_dot(ys, w_ref[:S5_WIDTH, :])
                 + _dot(ym_ref[0, rows, :], w_ref[S5_WIDTH:S5_WIDTH + MLA_WIDTH, :])
                 + _dot(yh_ref[0, rows, :], w_ref[S5_WIDTH + MLA_WIDTH:, :]))
        inv = lax.rsqrt(jnp.mean(mixed * mixed, axis=-1, keepdims=True) + EPS)
        o_ref[0, rows, :] = x_ref[0, rows, :] + mixed * inv * scale


def _out_proj(y_s5_rows, y_mla, y_hg, w, l, x, gain, g):
    nb, s, d = x.shape
    row = lambda n: pl.BlockSpec((1, OUT_TM, n), lambda b, i: (b, i, 0))
    return pl.pallas_call(
        _out_proj_kernel,
        grid=(nb, s // OUT_TM),
        in_specs=[pl.BlockSpec((1, OUT_TM // S5_LS, S5_LS * S5_WIDTH), lambda b, i: (b, i, 0)),
                  row(MLA_WIDTH), row(HG_WIDTH),
                  pl.BlockSpec((None,) + w.shape[1:], lambda b, i: (l, 0, 0), pipeline_mode=pl.Buffered(1)),
                  row(d),
                  pl.BlockSpec((1, d), lambda b, i: (0, 0)),
                  pl.BlockSpec((1, 1, d), lambda b, i: (b, 0, 0))],
        out_specs=row(d),
        out_shape=jax.ShapeDtypeStruct((nb, s, d), F32),
        scratch_shapes=[pltpu.VMEM((S5_WIDTH // LANES, OUT_TM, LANES), F32)],
        compiler_params=_params("parallel", "parallel"),
        name="out_proj",
    )(y_s5_rows, y_mla, y_hg, w, x, gain, g)


def _ffn_kernel(x_ref, sc_ref, sh_ref, g_ref, pre_ref, post_ref, wu_ref, cp_ref, wd_ref, o_ref,
                h_ref, acc_ref, ug_ref, uv_ref, act_ref, tail_ref):
    i = pl.program_id(1)
    j = pl.program_id(2)

    @pl.when(j == 0)
    def _():
        scale = pre_ref[...] * (1.0 + sc_ref[0])
        shift = sh_ref[0]
        for r0 in range(0, FFN_TM, NORM_ROWS):
            x = x_ref[0, r0:r0 + NORM_ROWS, :]
            inv = lax.rsqrt(jnp.mean(x * x, axis=-1, keepdims=True) + EPS)
            h_ref[r0:r0 + NORM_ROWS, :] = (x * inv * scale + shift).astype(BF16)
        acc_ref[...] = jnp.zeros_like(acc_ref)

    part = FFN_TM // FFN_PARTS
    for t, u_ref in enumerate((ug_ref, uv_ref)):
        u_ref[:HALO, :] = jnp.where(i == 0, 0.0, tail_ref[j, t])
    for p0 in range(0, FFN_TM, part):
        u = _dot(h_ref[p0:p0 + part, :], wu_ref[...])
        for t, u_ref in enumerate((ug_ref, uv_ref)):
            u_ref[HALO + p0:HALO + p0 + part, :] = u[:, t * FFN_TF:(t + 1) * FFN_TF]
            if p0 + part == FFN_TM:
                tail_ref[j, t] = u[part - HALO:, t * FFN_TF:(t + 1) * FFN_TF]

    def conv(u_ref, p0, r0):
        return (cp_ref[p0 + CONV_W:p0 + CONV_W + 1, :]
                + u_ref[r0 + HALO - 2:r0 + HALO - 2 + FFN_ROWS, :] * cp_ref[p0:p0 + 1, :]
                + u_ref[r0 + HALO - 1:r0 + HALO - 1 + FFN_ROWS, :] * cp_ref[p0 + 1:p0 + 2, :]
                + u_ref[r0 + HALO:r0 + HALO + FFN_ROWS, :] * cp_ref[p0 + 2:p0 + 3, :])

    for p0 in range(0, FFN_TM, part):
        for r0 in range(p0, p0 + part, FFN_ROWS):
            gate = conv(ug_ref, 0, r0)
            val = conv(uv_ref, CONV_W + 1, r0)
            act_ref[r0:r0 + FFN_ROWS, :] = (_gelu_tanh(gate) * val).astype(BF16)
        acc_ref[p0:p0 + part, :] += _dot(act_ref[p0:p0 + part, :], wd_ref[...])

    @pl.when(j == pl.num_programs(2) - 1)
    def _():
        scale = post_ref[...] * g_ref[0]
        for r0 in range(0, FFN_TM, NORM_ROWS):
            y = acc_ref[r0:r0 + NORM_ROWS, :]
            inv = lax.rsqrt(jnp.mean(y * y, axis=-1, keepdims=True) + EPS)
            o_ref[0, r0:r0 + NORM_ROWS, :] = x_ref[0, r0:r0 + NORM_ROWS, :] + y * inv * scale


def _ffn(x, sc, sh, g, pre, post, w_up, conv_p, w_down, l):
    nb, s, d = x.shape
    ff = w_down.shape[1]
    nf = ff // FFN_TF
    vec = pl.BlockSpec((1, 1, d), lambda b, i, j: (b, 0, 0))
    par = pl.BlockSpec((1, d), lambda b, i, j: (0, 0))
    return pl.pallas_call(
        _ffn_kernel,
        grid=(nb, s // FFN_TM, nf),
        in_specs=[
            pl.BlockSpec((1, FFN_TM, d), lambda b, i, j: (b, i, 0)),
            vec, vec, vec, par, par,
            pl.BlockSpec((None, d, 2 * FFN_TF), lambda b, i, j: (l, 0, j)),
            pl.BlockSpec((None, None, 2 * (CONV_W + 1), FFN_TF), lambda b, i, j: (l, j, 0, 0)),
            pl.BlockSpec((None, FFN_TF, d), lambda b, i, j: (l, j, 0)),
        ],
        out_specs=pl.BlockSpec((1, FFN_TM, d), lambda b, i, j: (b, i, 0)),
        out_shape=jax.ShapeDtypeStruct((nb, s, d), F32),
        scratch_shapes=[pltpu.VMEM((FFN_TM, d), BF16), pltpu.VMEM((FFN_TM, d), F32),
                        pltpu.VMEM((FFN_TM + HALO, FFN_TF), F32), pltpu.VMEM((FFN_TM + HALO, FFN_TF), F32),
                        pltpu.VMEM((FFN_TM, FFN_TF), BF16), pltpu.VMEM((nf, 2, HALO, FFN_TF), F32)],
        compiler_params=_params("parallel", "arbitrary", "arbitrary"),
        name="ffn",
    )(x, sc, sh, g, pre, post, w_up, conv_p, w_down)


def _pad_cols(a, n):
    return jnp.pad(a, [(0, 0)] * (a.ndim - 1) + [(0, n - a.shape[-1])])


def _cast_w_up_kernel(w_ref, o_ref):
    for h in range(2):
        @pl.when(pl.program_id(2) == h)
        def _():
            for j in range(D_FF_PAD // FFN_TF):
                width = min(FFN_TF, D_FF - j * FFN_TF)
                dst = (2 * j + h) * FFN_TF
                o_ref[0, :, dst:dst + width] = w_ref[0, :, j * FFN_TF:j * FFN_TF + width].astype(BF16)
                if width < FFN_TF:
                    o_ref[0, :, dst + width:dst + FFN_TF] = jnp.zeros((o_ref.shape[1], FFN_TF - width), BF16)


def _cast_w_up(w_up):
    nl, d, _ = w_up.shape
    return pl.pallas_call(
        _cast_w_up_kernel,
        grid=(nl, d // CAST_TM, 2),
        in_specs=[pl.BlockSpec((1, CAST_TM, D_FF), lambda l, i, h: (l, i, h))],
        out_specs=pl.BlockSpec((1, CAST_TM, 2 * D_FF_PAD), lambda l, i, h: (l, i, 0)),
        out_shape=jax.ShapeDtypeStruct((nl, d, 2 * D_FF_PAD), BF16),
        compiler_params=_params("parallel", "parallel", "arbitrary"),
        name="cast_w_up",
    )(w_up)


def _layout_w_in(w_in):
    o = 0
    parts = []
    for width in (S5_WIDTH, MLA_Q_RANK, MLA_KV_RANK, MLA_ROPE):
        parts.append(w_in[..., o:o + width])
        o += width
    parts.append(jnp.zeros(w_in.shape[:-1] + (COL_BLK - MLA_KV_RANK - MLA_ROPE,), w_in.dtype))
    parts.append(w_in[..., o:])
    return jnp.concatenate(parts, axis=-1).astype(BF16)


def _layout_w_uq(w):
    nl, r, _ = w.shape
    w = w.reshape(nl, r, MLA_HEADS, MLA_NOPE + MLA_ROPE)
    return _pad_cols(w, MLA_QK).reshape(nl, r, MLA_HEADS * MLA_QK).astype(BF16)


def _layout_w_ukv(w):
    nl, r, _ = w.shape
    w = w.reshape(nl, r, MLA_HEADS, MLA_NOPE + MLA_V)
    k = w[..., :MLA_NOPE].reshape(nl, r, MLA_HEADS * MLA_NOPE)
    v = w[..., MLA_NOPE:].reshape(nl, r, MLA_HEADS * MLA_V)
    return jnp.concatenate([k, v], axis=-1).astype(BF16)


def _rope_tables(positions):
    inv_freq = 1.0 / (ROPE_THETA ** (jnp.arange(0, MLA_ROPE, 2, dtype=F32) / MLA_ROPE))
    ang = positions.astype(F32)[..., None] * inv_freq
    cos, sin = jnp.cos(ang), jnp.sin(ang)
    z32 = jnp.zeros_like(cos)
    z64 = jnp.concatenate([z32, z32], axis=-1)
    return (jnp.concatenate([cos, cos, z64], axis=-1),
            jnp.concatenate([z32, sin, z64], axis=-1),
            jnp.concatenate([-sin, z32, z64], axis=-1))


def kernel(x, c, positions, w_in, s5_lambda_re, s5_lambda_im, s5_log_dt, s5_b_re, s5_b_im, s5_c_re, s5_c_im, s5_d, s5_w_glu, mla_q_norm, mla_w_uq, mla_kv_norm, mla_w_ukv, hg_lb_logits, hg_out_norm, w_out, mix_pre_norm, mix_post_norm, ffn_pre_norm, ffn_post_norm, ffn_w_up, ffn_conv_w, ffn_conv_b, ffn_w_down, w_ada, b_ada):
    nb, s, d = x.shape
    nl = w_in.shape[0]
    assert s % FLASH_T == 0 and s % FFN_TM == 0 and s % (S5_LS * SUBLANES) == 0

    mod = _adaln(c, w_ada, b_ada)
    cos_t, s1_t, s2_t = _rope_tables(positions)
    probs = jax.nn.softmax(hg_lb_logits.astype(F32), axis=0)
    lower_bounds = jnp.cumsum(probs, axis=0) - probs[0:1]

    w_in_p = _layout_w_in(w_in)
    w_uq_p = _layout_w_uq(mla_w_uq)
    w_ukv_p = _layout_w_ukv(mla_w_ukv)
    w_out_b = w_out.astype(BF16)
    w_glu_b = s5_w_glu.astype(BF16)
    w_up_p = _cast_w_up(ffn_w_up)
    conv_rows = jnp.concatenate([ffn_conv_w[..., :D_FF], ffn_conv_b[:, None, :D_FF],
                                 ffn_conv_w[..., D_FF:], ffn_conv_b[:, None, D_FF:]], axis=1)
    conv_p = _pad_cols(conv_rows, D_FF_PAD).reshape(nl, 2 * (CONV_W + 1), -1, FFN_TF).transpose(0, 2, 1, 3)
    w_down = jnp.pad(ffn_w_down, ((0, 0), (0, D_FF_PAD - D_FF), (0, 0))).astype(BF16)

    for l in range(nl):
        sh1, sc1, g1, sh2, sc2, g2 = [m[:, None, :] for m in jnp.split(mod[l], 6, axis=-1)]
        proj, u_rows = _proj_in(x, sc1, sh1, mix_pre_norm[l][None, :], w_in_p, l)
        bm, cm, lam, lam_chunk = _s5_prep(s5_lambda_re[l], s5_lambda_im[l], s5_log_dt[l],
                                          s5_b_re[l], s5_b_im[l], s5_c_re[l], s5_c_im[l])
        y_s5 = _s5(u_rows, bm, cm, lam, lam_chunk, s5_d[l][None, :], w_glu_b[l])
        q, k, v = _mla_prep(proj, mla_q_norm[l][None, :], mla_kv_norm[l][None, :], w_uq_p[l], w_ukv_p[l],
                            cos_t, s1_t, s2_t)
        y_mla = _flash(q, k, v)
        y_hg = _hgrn(proj, lower_bounds[l][None, :], hg_out_norm[l][None, :])
        x = _out_proj(y_s5, y_mla, y_hg, w_out_b, l, x, mix_post_norm[l][None, :], g1)
        x = _ffn(x, sc2, sh2, g2, ffn_pre_norm[l][None, :], ffn_post_norm[l][None, :],
                 w_up_p, conv_p, w_down, l)
    return x
```

```python
import functools
import math

import jax
import jax.numpy as jnp
from jax import lax
from jax.experimental import pallas as pl
from jax.experimental.pallas import tpu as pltpu

F32 = jnp.float32
BF16 = jnp.bfloat16

D_MODEL = 2048
DEPTH = 4
S5_WIDTH = 512
S5_GROUP = 16
S5_GROUPS = 32
S5_STATE = 64
MLA_HEADS = 8
MLA_NOPE = 128
MLA_ROPE = 64
MLA_V = 128
MLA_Q_RANK = 512
MLA_KV_RANK = 256
MLA_WIDTH = MLA_HEADS * MLA_V
ROPE_THETA = 10000.0
MASK_VALUE = -1e30
HG_HEADS = 4
HG_DK = 128
HG_DV = 128
HG_WIDTH = HG_HEADS * HG_DV
D_FF = 5504
CONV_W = 3
EPS = 1e-6

LANES = 128
SUBLANES = 8
MXU_DIM = 256
VMEM_LIMIT_BYTES = 56 * 1024 * 1024

COL_BLK = 512
N_COL_BLKS = 7
D_IN_PAD = N_COL_BLKS * COL_BLK
MLA_QK = 2 * LANES
D_FF_PAD = 5632

ADA_TN = 1024
PROJ_TM = 512
S5_LS = 16
S5_SEQS = 2
S5_HALF = S5_WIDTH // 2
S5_HSTATE = (S5_GROUPS // 2) * S5_STATE
PREP_TM = 512
FLASH_T = 512
FLASH_HEADS = 2
FLASH_ROWS = 32
HG_CHUNK = 128
HG_LEVELS = (16, 32, 64, 128)
HG_SEQS = 2
OUT_TM = 512
FFN_TM = 512
FFN_TF = 512
FFN_ROWS = 64
FFN_PARTS = 1
NORM_ROWS = 16
CAST_TM = 256
HALO = SUBLANES


def _sigmoid(x):
    return 1.0 / (1.0 + jnp.exp(-x))


def _silu(x):
    return x * _sigmoid(x)


def _gelu_tanh(x):
    c = math.sqrt(2.0 / math.pi)
    return x * (0.5 * (1.0 + jnp.tanh(c * (x + 0.044715 * (x * x * x)))))


def _rms(x, gain):
    return x * lax.rsqrt(jnp.mean(x * x, axis=-1, keepdims=True) + EPS) * gain


def _dot(a, b):
    return jnp.dot(a, b, preferred_element_type=F32)


def _dot_nt(a, b):
    return lax.dot_general(a, b, (((1,), (1,)), ((), ())), preferred_element_type=F32)


def _dot_tn(a, b):
    return lax.dot_general(a, b, (((0,), (0,)), ((), ())), preferred_element_type=F32)


def _params(*sem):
    return pltpu.CompilerParams(dimension_semantics=sem, vmem_limit_bytes=VMEM_LIMIT_BYTES)


def _adaln_kernel(c_ref, w_ref, b_ref, o_ref):
    ca = _silu(c_ref[...]).astype(BF16)
    o_ref[0] = _dot(ca, w_ref[0].astype(BF16)) + b_ref[0]


def _adaln(c, w_ada, b_ada):
    nb, d = c.shape
    nl, _, n = w_ada.shape
    rows = -(-nb // SUBLANES) * SUBLANES
    c_pad = jnp.pad(c, ((0, rows - nb), (0, 0)))
    out = pl.pallas_call(
        _adaln_kernel,
        grid=(nl, n // ADA_TN),
        in_specs=[
            pl.BlockSpec((rows, d), lambda l, j: (0, 0)),
            pl.BlockSpec((1, d, ADA_TN), lambda l, j: (l, 0, j)),
            pl.BlockSpec((1, 1, ADA_TN), lambda l, j: (l, 0, j)),
        ],
        out_specs=pl.BlockSpec((1, rows, ADA_TN), lambda l, j: (l, 0, j)),
        out_shape=jax.ShapeDtypeStruct((nl, rows, n), F32),
        compiler_params=_params("parallel", "parallel"),
        name="adaln",
    )(c_pad, w_ada, b_ada.reshape(nl, 1, n))
    return out[:, :nb]


def _proj_in_kernel(x_ref, sc_ref, sh_ref, gain_ref, w_ref, o_ref, u_ref, h_ref, slab_ref):
    scale = gain_ref[...] * (1.0 + sc_ref[0])
    shift = sh_ref[0]
    part = PROJ_TM // 2
    for p0 in range(0, PROJ_TM, part):
        for r0 in range(p0, p0 + part, NORM_ROWS):
            x = x_ref[0, r0:r0 + NORM_ROWS, :]
            inv = lax.rsqrt(jnp.mean(x * x, axis=-1, keepdims=True) + EPS)
            h_ref[r0:r0 + NORM_ROWS, :] = (x * inv * scale + shift).astype(BF16)
        res = _dot(h_ref[p0:p0 + part, :], w_ref[...])
        o_ref[0, p0:p0 + part, :] = res
        for c in range(S5_WIDTH // LANES):
            slab_ref[c, p0:p0 + part, :] = res[:, c * LANES:(c + 1) * LANES]

    for t in range(S5_LS):
        for c in range(S5_WIDTH // LANES):
            col = t * S5_WIDTH + c * LANES
            u_ref[0, :, col:col + LANES] = slab_ref[c, pl.ds(t, PROJ_TM // S5_LS, stride=S5_LS), :]


def _proj_in(x, sc, sh, gain, w, l):
    nb, s, d = x.shape
    n = w.shape[2]
    rows = PROJ_TM // S5_LS
    return pl.pallas_call(
        _proj_in_kernel,
        grid=(nb, s // PROJ_TM),
        in_specs=[
            pl.BlockSpec((1, PROJ_TM, d), lambda b, i: (b, i, 0)),
            pl.BlockSpec((1, 1, d), lambda b, i: (b, 0, 0)),
            pl.BlockSpec((1, 1, d), lambda b, i: (b, 0, 0)),
            pl.BlockSpec((1, d), lambda b, i: (0, 0)),
            pl.BlockSpec((None, d, n), lambda b, i: (l, 0, 0), pipeline_mode=pl.Buffered(1)),
        ],
        out_specs=[pl.BlockSpec((1, PROJ_TM, n), lambda b, i: (b, i, 0)),
                   pl.BlockSpec((1, rows, S5_LS * S5_WIDTH), lambda b, i: (b, i, 0))],
        out_shape=[jax.ShapeDtypeStruct((nb, s, n), F32),
                   jax.ShapeDtypeStruct((nb, s // S5_LS, S5_LS * S5_WIDTH), F32)],
        scratch_shapes=[pltpu.VMEM((PROJ_TM, d), BF16), pltpu.VMEM((S5_WIDTH // LANES, PROJ_TM, LANES), F32)],
        compiler_params=_params("parallel", "parallel"),
        name="proj_in",
    )(x, sc, sh, gain, w)


def _s5_kernel(u_ref, bm_ref, cm_ref, lam_ref, lam_chunk_ref, d_ref, wglu_ref, o_ref,
               sre, sim, hre, him, yloc):
    ph = pl.program_id(1)
    s = pl.program_id(2)
    nseq, kc, _ = u_ref.shape
    rows = nseq * kc

    @pl.when(ph == 0)
    def _():
        @pl.when(s == 0)
        def _():
            sre[...] = jnp.zeros_like(sre)
            sim[...] = jnp.zeros_like(sim)

        u = u_ref[...].reshape(rows, S5_WIDTH)
        ub = u.astype(BF16)
        for h in range(2):
            cols = slice(h * S5_HALF, (h + 1) * S5_HALF)
            v = _dot(ub[:, cols], bm_ref[h])
            lr = lam_ref[h:h + 1, :]
            li = lam_ref[2 + h:3 + h, :]
            re = sre[h]
            im = sim[h]
            nre = lr * re - li * im + v[:, :S5_HSTATE]
            nim = lr * im + li * re + v[:, S5_HSTATE:]
            sre[h] = nre
            sim[h] = nim
            yl = _dot(nre.astype(BF16), cm_ref[h, 0]) + _dot(nim.astype(BF16), cm_ref[h, 1])
            yloc[s, :, cols] = yl + d_ref[:, cols] * u[:, cols]

        @pl.when(s == S5_LS - 1)
        def _():
            def body(k, carry):
                new = []
                for n in range(nseq):
                    row = pl.ds(n * kc + k, 1)
                    for h in range(2):
                        cre, cim = carry[4 * n + 2 * h], carry[4 * n + 2 * h + 1]
                        hre[h, row, :] = cre
                        him[h, row, :] = cim
                        lr = lam_chunk_ref[h:h + 1, :]
                        li = lam_chunk_ref[2 + h:3 + h, :]
                        new.append(lr * cre - li * cim + sre[h, row, :])
                        new.append(lr * cim + li * cre + sim[h, row, :])
                return tuple(new)

            zero = jnp.zeros((1, S5_HSTATE), F32)
            lax.fori_loop(0, kc, body, (zero,) * (4 * nseq))

    @pl.when(ph == 1)
    def _():
        ys = []
        for h in range(2):
            cols = slice(h * S5_HALF, (h + 1) * S5_HALF)
            lr = lam_ref[h:h + 1, :]
            li = lam_ref[2 + h:3 + h, :]
            pre = hre[h]
            pim = him[h]
            nre = lr * pre - li * pim
            nim = lr * pim + li * pre
            hre[h] = nre
            him[h] = nim
            yc = _dot(nre.astype(BF16), cm_ref[h, 0]) + _dot(nim.astype(BF16), cm_ref[h, 1])
            ys.append(yloc[s, :, cols] + yc)
        g = _gelu_tanh(jnp.concatenate(ys, axis=1))
        gate = _sigmoid(_dot(g.astype(BF16), wglu_ref[...]))
        o_ref[...] = (g * gate).reshape(o_ref.shape).astype(o_ref.dtype)


def _s5(u_rows, bm, cm, lam, lam_chunk, d, wglu):
    nb, kc, _ = u_rows.shape
    ns = S5_SEQS if nb % S5_SEQS == 0 else 1
    rows = ns * kc
    return pl.pallas_call(
        _s5_kernel,
        grid=(nb // ns, 2, S5_LS),
        in_specs=[
            pl.BlockSpec((ns, kc, S5_WIDTH),
                         lambda b, p, t: (b, 0, jnp.where(p == 0, t, S5_LS - 1))),
            pl.BlockSpec(bm.shape, lambda b, p, t: (0, 0, 0)),
            pl.BlockSpec(cm.shape, lambda b, p, t: (0, 0, 0, 0)),
            pl.BlockSpec(lam.shape, lambda b, p, t: (0, 0)),
            pl.BlockSpec(lam_chunk.shape, lambda b, p, t: (0, 0)),
            pl.BlockSpec(d.shape, lambda b, p, t: (0, 0)),
            pl.BlockSpec(wglu.shape, lambda b, p, t: (0, 0)),
        ],
        out_specs=pl.BlockSpec((ns, kc, S5_WIDTH), lambda b, p, t: (b, 0, t * p)),
        out_shape=jax.ShapeDtypeStruct((nb, kc, S5_LS * S5_WIDTH), F32),
        scratch_shapes=[pltpu.VMEM((2, rows, S5_HSTATE), F32)] * 4
        + [pltpu.VMEM((S5_LS, rows, S5_WIDTH), F32)],
        compiler_params=_params("parallel", "arbitrary", "arbitrary"),
        name="s5",
    )(u_rows, bm, cm, lam, lam_chunk, d, wglu)


def _s5_prep(lam_re, lam_im, log_dt, b_re, b_im, c_re, c_im):
    lam_re, lam_im = lam_re.astype(F32), lam_im.astype(F32)
    dt = jnp.exp(log_dt.astype(F32))[:, None]
    mag = jnp.exp(lam_re * dt)
    bar_re, bar_im = mag * jnp.cos(lam_im * dt), mag * jnp.sin(lam_im * dt)
    den = lam_re * lam_re + lam_im * lam_im
    co_re = ((bar_re - 1.0) * lam_re + bar_im * lam_im) / den
    co_im = (bar_im * lam_re - (bar_re - 1.0) * lam_im) / den
    b_re, b_im = b_re.astype(F32), b_im.astype(F32)
    bb_re = co_re[..., None] * b_re - co_im[..., None] * b_im
    bb_im = co_re[..., None] * b_im + co_im[..., None] * b_re
    ch_re, ch_im = bar_re, bar_im
    for _ in range(int(math.log2(S5_LS))):
        ch_re, ch_im = ch_re * ch_re - ch_im * ch_im, 2.0 * ch_re * ch_im
    gh = S5_GROUPS // 2
    eye = jnp.eye(gh, dtype=F32)

    def b_tiles(part):
        t = part.reshape(2, gh, S5_STATE, S5_GROUP)
        return jnp.einsum('ab,hapc->hacbp', eye, t).reshape(2, S5_HALF, S5_HSTATE)

    def c_tiles(part):
        t = part.reshape(2, gh, S5_GROUP, S5_STATE)
        return jnp.einsum('ab,hbcp->hapbc', eye, t).reshape(2, S5_HSTATE, S5_HALF)

    bm = jnp.concatenate([b_tiles(bb_re), b_tiles(bb_im)], axis=-1).astype(BF16)
    cm = jnp.stack([c_tiles(c_re.astype(F32)), -c_tiles(c_im.astype(F32))], axis=1).astype(BF16)

    def rows(re, im):
        return jnp.concatenate([re.reshape(2, S5_HSTATE), im.reshape(2, S5_HSTATE)], 0)

    return bm, cm, rows(bar_re, bar_im), rows(ch_re, ch_im)


def _mla_prep_kernel(cq_ref, ckv_ref, qn_ref, kvn_ref, wq_ref, wkv_ref, cos_ref, s1_ref, s2_ref,
                     q_ref, k_ref, v_ref):
    cos_t = cos_ref[0]
    s1 = s1_ref[0]
    s2 = s2_ref[0]
    half = MLA_ROPE // 2

    def rope(x):
        return x * cos_t + pltpu.roll(x, half, 1) * s1 + pltpu.roll(x, LANES - half, 1) * s2

    scale = (MLA_NOPE + MLA_ROPE) ** -0.5 * math.log2(math.e)
    q = _dot(_rms(cq_ref[0], qn_ref[...]).astype(BF16), wq_ref[...])
    blk = ckv_ref[0]
    kv = _dot(_rms(blk[:, :MLA_KV_RANK], kvn_ref[...]).astype(BF16), wkv_ref[...])
    kr = rope(blk[:, MLA_KV_RANK:MLA_KV_RANK + LANES]).astype(BF16)
    for h in range(MLA_HEADS):
        qn = q[:, h * MLA_QK:h * MLA_QK + MLA_NOPE] * scale
        qr = rope(q[:, h * MLA_QK + MLA_NOPE:(h + 1) * MLA_QK]) * scale
        q_ref[0, h] = jnp.concatenate([qn, qr], axis=1).astype(BF16)
        k_ref[0, h] = jnp.concatenate([kv[:, h * MLA_NOPE:(h + 1) * MLA_NOPE].astype(BF16), kr], axis=1)
        vh = kv[:, MLA_WIDTH + h * MLA_V:MLA_WIDTH + (h + 1) * MLA_V]
        v_ref[0, h] = jnp.concatenate([vh, jnp.ones_like(vh)], axis=1).astype(BF16)


def _mla_prep(proj, qn, kvn, wq, wkv, cos_t, s1_t, s2_t):
    nb, s, _ = proj.shape
    tab = pl.BlockSpec((1, PREP_TM, LANES), lambda b, i: (b, i, 0))
    return pl.pallas_call(
        _mla_prep_kernel,
        grid=(nb, s // PREP_TM),
        in_specs=[
            pl.BlockSpec((1, PREP_TM, COL_BLK), lambda b, i: (b, i, 1)),
            pl.BlockSpec((1, PREP_TM, COL_BLK), lambda b, i: (b, i, 2)),
            pl.BlockSpec(qn.shape, lambda b, i: (0, 0)),
            pl.BlockSpec(kvn.shape, lambda b, i: (0, 0)),
            pl.BlockSpec(wq.shape, lambda b, i: (0, 0)),
            pl.BlockSpec(wkv.shape, lambda b, i: (0, 0)),
            tab, tab, tab,
        ],
        out_specs=[
            pl.BlockSpec((1, MLA_HEADS, PREP_TM, MLA_QK), lambda b, i: (b, 0, i, 0)),
            pl.BlockSpec((1, MLA_HEADS, PREP_TM, MLA_QK), lambda b, i: (b, 0, i, 0)),
            pl.BlockSpec((1, MLA_HEADS, PREP_TM, 2 * MLA_V), lambda b, i: (b, 0, i, 0)),
        ],
        out_shape=[
            jax.ShapeDtypeStruct((nb, MLA_HEADS, s, MLA_QK), BF16),
            jax.ShapeDtypeStruct((nb, MLA_HEADS, s, MLA_QK), BF16),
            jax.ShapeDtypeStruct((nb, MLA_HEADS, s, 2 * MLA_V), BF16),
        ],
        compiler_params=_params("parallel", "parallel"),
        name="mla_prep",
    )(proj, proj, qn, kvn, wq, wkv, cos_t, s1_t, s2_t)


def _flash_kernel(q_ref, k_ref, v_ref, o_ref, s_ref, p_ref, m_ref, a_ref, acc_ref):
    qi = pl.program_id(2)
    t = FLASH_T
    heads = range(FLASH_HEADS)
    nh = FLASH_HEADS
    m_ref[...] = jnp.full(m_ref.shape, -jnp.inf, F32)
    acc_ref[...] = jnp.zeros_like(acc_ref)
    p_ref[nh:] = jnp.zeros((nh,) + p_ref.shape[1:], BF16)
    a_ref[nh:] = jnp.ones((nh,) + a_ref.shape[1:], F32)

    def scores(j, slot):
        start = pl.multiple_of(j * t, t)
        for h in heads:
            s_ref[slot * nh + h] = _dot_nt(q_ref[0, h], k_ref[0, h, pl.ds(start, t), :])

    def softmax(slot, diagonal):
        for h in heads:
            for r0 in range(0, t, FLASH_ROWS):
                rows = slice(r0, r0 + FLASH_ROWS)
                sc = s_ref[slot * nh + h, rows, :]
                if diagonal:
                    row = lax.broadcasted_iota(jnp.int32, sc.shape, 0) + r0
                    col = lax.broadcasted_iota(jnp.int32, sc.shape, 1)
                    sc = jnp.where(col <= row, sc, MASK_VALUE)
                m_old = m_ref[h, rows, :]
                m_new = jnp.maximum(m_old, jnp.max(sc, axis=-1, keepdims=True))
                p_ref[slot * nh + h, rows, :] = jnp.exp2(sc - m_new[:, :1]).astype(BF16)
                a_ref[slot * nh + h, rows, :] = jnp.exp2(m_old - m_new)
                m_ref[h, rows, :] = m_new

    def accumulate(j, slot):
        start = pl.multiple_of(j * t, t)
        for h in heads:
            pv = _dot(p_ref[slot * nh + h], v_ref[0, h, pl.ds(start, t), :])
            alpha = a_ref[slot * nh + h]
            acc_ref[h] = jnp.concatenate([alpha, alpha], axis=1) * acc_ref[h] + pv

    def stage(j, slot, with_accumulate=True):
        softmax(slot, False)
        scores(j + 1, 1 - slot)
        if with_accumulate:
            accumulate(jnp.maximum(j - 1, 0), 1 - slot)

    odd = lax.rem(qi, 2)
    for slot in range(2):
        @pl.when(odd == slot)
        def _():
            scores(0, slot)

    @pl.when(odd == 1)
    def _():
        stage(0, 1, with_accumulate=False)

    def body(jj, carry):
        j = 2 * jj + odd
        stage(j, 0)
        stage(j + 1, 1)
        return carry

    lax.fori_loop(0, qi // 2, body, 0)
    softmax(0, True)
    accumulate(jnp.maximum(qi - 1, 0), 1)
    accumulate(qi, 0)

    for h in heads:
        acc = acc_ref[h]
        o_ref[0, :, h * MLA_V:(h + 1) * MLA_V] = (acc[:, :MLA_V] / acc[:, MLA_V:]).astype(o_ref.dtype)


def _flash(q, k, v):
    nb, nh, s, _ = q.shape
    hb = FLASH_HEADS
    return pl.pallas_call(
        _flash_kernel,
        grid=(nb, nh // hb, s // FLASH_T),
        in_specs=[
            pl.BlockSpec((1, hb, FLASH_T, MLA_QK), lambda b, h, i: (b, h, i, 0)),
            pl.BlockSpec((1, hb, s, MLA_QK), lambda b, h, i: (b, h, 0, 0)),
            pl.BlockSpec((1, hb, s, 2 * MLA_V), lambda b, h, i: (b, h, 0, 0)),
        ],
        out_specs=pl.BlockSpec((1, FLASH_T, hb * MLA_V), lambda b, h, i: (b, i, h)),
        out_shape=jax.ShapeDtypeStruct((nb, s, nh * MLA_V), BF16),
        scratch_shapes=[pltpu.VMEM((2 * hb, FLASH_T, FLASH_T), F32), pltpu.VMEM((2 * hb, FLASH_T, FLASH_T), BF16),
                        pltpu.VMEM((hb, FLASH_T, LANES), F32), pltpu.VMEM((2 * hb, FLASH_T, LANES), F32),
                        pltpu.VMEM((hb, FLASH_T, 2 * MLA_V), F32)],
        compiler_params=_params("parallel", "parallel", "arbitrary"),
        name="flash",
    )(q, k, v)


def _cumsum_rows(x, ltri):
    hi = x.astype(BF16)
    r1 = x - hi.astype(F32)
    mid = r1.astype(BF16)
    lo = (r1 - mid.astype(F32)).astype(BF16)
    return _dot(ltri, hi) + _dot(ltri, mid) + _dot(ltri, lo)


def _hgrn_kernel(q_ref, f_ref, i_ref, g_ref, lb_ref, on_ref, ones_ref, ltri_ref, o_ref, st_ref):
    @pl.when(pl.program_id(1) == 0)
    def _():
        st_ref[...] = jnp.zeros_like(st_ref)

    for n in range(q_ref.shape[0]):
        _hgrn_chunk(n, q_ref, f_ref, i_ref, g_ref, lb_ref, on_ref, ones_ref, ltri_ref, o_ref, st_ref)


def _hgrn_chunk(n, q_ref, f_ref, i_ref, g_ref, lb_ref, on_ref, ones_ref, ltri_ref, o_ref, st_ref):
    c, w = HG_CHUNK, HG_WIDTH
    z = f_ref[n]
    lb = lb_ref[...]
    e = jnp.exp(-jnp.abs(z))
    r = 1.0 / (1.0 + e)
    pos = z >= 0.0
    sig_z = jnp.where(pos, r, e * r)
    sig_mz = jnp.where(pos, e * r, r)
    kk = (1.0 - lb) * sig_mz
    q = _silu(q_ref[n])
    v = i_ref[n]
    gate = _silu(g_ref[n])
    f = lb + (1.0 - lb) * sig_z
    b = _cumsum_rows(jnp.log2(f), ltri_ref[...])
    row = lax.broadcasted_iota(jnp.int32, (c, w), 0)

    q_lvls, k_lvls = [], []
    for m in HG_LEVELS:
        half = m // 2
        b4 = b.reshape(c // m, 2, half, w)
        ref_b = b4[:, 0, half - 1:half, :]
        k_first = kk.reshape(c // m, 2, half, w)[:, 0] * jnp.exp2(ref_b - b4[:, 0])
        q_second = q.reshape(c // m, 2, half, w)[:, 1] * jnp.exp2(b4[:, 1] - ref_b)
        zero = jnp.zeros_like(k_first)
        q_lvls.append(jnp.stack([zero, q_second], axis=1).reshape(c, w).astype(BF16))
        k_lvls.append(jnp.stack([k_first, zero], axis=1).reshape(c, w).astype(BF16))

    pos8 = row & (SUBLANES - 1)

    def back1(x):
        x3 = x.reshape(c // SUBLANES, SUBLANES, w)
        return pltpu.roll(x3, 1, 1).reshape(c, w)

    f_in = jnp.where(pos8 >= 1, f, 0.0)
    kd_dl, vr = kk, v
    o_diag = jnp.zeros((c, w), F32)
    for dl in range(SUBLANES):
        if dl > 0:
            kd_dl = f_in * back1(kd_dl)
            vr = back1(vr)
        wb = (q * kd_dl).astype(BF16)
        a = jnp.concatenate([_dot(wb[:, :MXU_DIM], ones_ref[...]), _dot(wb[:, MXU_DIM:], ones_ref[...])], axis=1)
        o_diag = o_diag + a * vr

    tt = lax.broadcasted_iota(jnp.int32, (c, c), 0)
    ss = lax.broadcasted_iota(jnp.int32, (c, c), 1)
    same_blk = {m: (tt // m) == (ss // m) for m in HG_LEVELS}

    b_last = b[c - 1:c, :]
    qe = (q * jnp.exp2(b)).astype(BF16)
    kd = (kk * jnp.exp2(b_last - b)).astype(BF16)
    decay = jnp.exp2(b_last)
    vb = v.astype(BF16)
    for h in range(HG_HEADS):
        hs = slice(h * HG_DK, (h + 1) * HG_DK)
        st = st_ref[n, h]
        attn = jnp.zeros((c, c), F32)
        for m, ql, kl in zip(HG_LEVELS, q_lvls, k_lvls):
            part = _dot_nt(ql[:, hs], kl[:, hs])
            attn = attn + (part if m == c else jnp.where(same_blk[m], part, 0.0))
        o_h =_dot_nt(qe[:, hs], st.astype(BF16)) + _dot(attn.astype(BF16), vb[:, hs]) + o_diag[:, hs]
        st_ref[n, h] = st * decay[:, hs] + _dot_tn(vb[:, hs], kd[:, hs])
        o_ref[n, :, hs] = (_rms(o_h, on_ref[...]) * gate[:, hs]).astype(o_ref.dtype)


def _hgrn(proj, lb, onorm):
    nb, s, _ = proj.shape
    head_id = jnp.arange(MXU_DIM) // HG_DK
    ones_blk = (head_id[:, None] == head_id[None, :]).astype(BF16)
    t_id = jnp.arange(HG_CHUNK)
    ltri = (t_id[:, None] >= t_id[None, :]).astype(BF16)
    ns = HG_SEQS if nb % HG_SEQS == 0 else 1
    spec = lambda cb: pl.BlockSpec((ns, HG_CHUNK, COL_BLK), lambda b, i: (b, i, cb))
    return pl.pallas_call(
        _hgrn_kernel,
        grid=(nb // ns, s // HG_CHUNK),
        in_specs=[spec(3), spec(4), spec(5), spec(6),
                  pl.BlockSpec(lb.shape, lambda b, i: (0, 0)),
                  pl.BlockSpec(onorm.shape, lambda b, i: (0, 0)),
                  pl.BlockSpec(ones_blk.shape, lambda b, i: (0, 0)),
                  pl.BlockSpec(ltri.shape, lambda b, i: (0, 0))],
        out_specs=pl.BlockSpec((ns, HG_CHUNK, HG_WIDTH), lambda b, i: (b, i, 0)),
        out_shape=jax.ShapeDtypeStruct((nb, s, HG_WIDTH), BF16),
        scratch_shapes=[pltpu.VMEM((ns, HG_HEADS, HG_DV, HG_DK), F32)],
        compiler_params=_params("parallel", "arbitrary"),
        name="hgrn",
    )(proj, proj, proj, proj, lb, onorm, ones_blk, ltri)


def _out_proj_kernel(ys_ref, ym_ref, yh_ref, w_ref, x_ref, gain_ref, g_ref, o_ref, slab_ref):
    for t in range(S5_LS):
        for c in range(S5_WIDTH // LANES):
            col = t * S5_WIDTH + c * LANES
            slab_ref[c, pl.ds(t, OUT_TM // S5_LS, stride=S5_LS), :] = ys_ref[0, :, col:col + LANES]
    scale = gain_ref[...] * g_ref[0]
    part = OUT_TM // 2
    for p0 in range(0, OUT_TM, part):
        rows = slice(p0, p0 + part)
        ys = jnp.concatenate([slab_ref[c, rows, :] for c in range(S5_WIDTH // LANES)], axis=1).astype(BF16)
        mixed = (_dot(ys, w_ref[:S5_WIDTH, :])
                 + _dot(ym_ref[0, rows, :], w_ref[S5_WIDTH:S5_WIDTH + MLA_WIDTH, :])
                 + _dot(yh_ref[0, rows, :], w_ref[S5_WIDTH + MLA_WIDTH:, :]))
        inv = lax.rsqrt(jnp.mean(mixed * mixed, axis=-1, keepdims=True) + EPS)
        o_ref[0, rows, :] = x_ref[0, rows, :] + mixed * inv * scale


def _out_proj(y_s5_rows, y_mla, y_hg, w, l, x, gain, g):
    nb, s, d = x.shape
    row = lambda n: pl.BlockSpec((1, OUT_TM, n), lambda b, i: (b, i, 0))
    return pl.pallas_call(
        _out_proj_kernel,
        grid=(nb, s // OUT_TM),
        in_specs=[pl.BlockSpec((1, OUT_TM // S5_LS, S5_LS * S5_WIDTH), lambda b, i: (b, i, 0)),
                  row(MLA_WIDTH), row(HG_WIDTH),
                  pl.BlockSpec((None,) + w.shape[1:], lambda b, i: (l, 0, 0), pipeline_mode=pl.Buffered(1)),
                  row(d),
                  pl.BlockSpec((1, d), lambda b, i: (0, 0)),
                  pl.BlockSpec((1, 1, d), lambda b, i: (b, 0, 0))],
        out_specs=row(d),
        out_shape=jax.ShapeDtypeStruct((nb, s, d), F32),
        scratch_shapes=[pltpu.VMEM((S5_WIDTH // LANES, OUT_TM, LANES), F32)],
        compiler_params=_params("parallel", "parallel"),
        name="out_proj",
    )(y_s5_rows, y_mla, y_hg, w, x, gain, g)


def _ffn_kernel(x_ref, sc_ref, sh_ref, g_ref, pre_ref, post_ref, wu_ref, cp_ref, wd_ref, o_ref,
                h_ref, acc_ref, ug_ref, uv_ref, act_ref, tail_ref):
    i = pl.program_id(1)
    j = pl.program_id(2)

    @pl.when(j == 0)
    def _():
        scale = pre_ref[...] * (1.0 + sc_ref[0])
        shift = sh_ref[0]
        for r0 in range(0, FFN_TM, NORM_ROWS):
            x = x_ref[0, r0:r0 + NORM_ROWS, :]
            inv = lax.rsqrt(jnp.mean(x * x, axis=-1, keepdims=True) + EPS)
            h_ref[r0:r0 + NORM_ROWS, :] = (x * inv * scale + shift).astype(BF16)
        acc_ref[...] = jnp.zeros_like(acc_ref)

    part = FFN_TM // FFN_PARTS
    for t, u_ref in enumerate((ug_ref, uv_ref)):
        u_ref[:HALO, :] = jnp.where(i == 0, 0.0, tail_ref[j, t])
    for p0 in range(0, FFN_TM, part):
        u = _dot(h_ref[p0:p0 + part, :], wu_ref[...])
        for t, u_ref in enumerate((ug_ref, uv_ref)):
            u_ref[HALO + p0:HALO + p0 + part, :] = u[:, t * FFN_TF:(t + 1) * FFN_TF]
            if p0 + part == FFN_TM:
                tail_ref[j, t] = u[part - HALO:, t * FFN_TF:(t + 1) * FFN_TF]

    def conv(u_ref, p0, r0):
        return (cp_ref[p0 + CONV_W:p0 + CONV_W + 1, :]
                + u_ref[r0 + HALO - 2:r0 + HALO - 2 + FFN_ROWS, :] * cp_ref[p0:p0 + 1, :]
                + u_ref[r0 + HALO - 1:r0 + HALO - 1 + FFN_ROWS, :] * cp_ref[p0 + 1:p0 + 2, :]
                + u_ref[r0 + HALO:r0 + HALO + FFN_ROWS, :] * cp_ref[p0 + 2:p0 + 3, :])

    for p0 in range(0, FFN_TM, part):
        for r0 in range(p0, p0 + part, FFN_ROWS):
            gate = conv(ug_ref, 0, r0)
            val = conv(uv_ref, CONV_W + 1, r0)
            act_ref[r0:r0 + FFN_ROWS, :] = (_gelu_tanh(gate) * val).astype(BF16)
        acc_ref[p0:p0 + part, :] += _dot(act_ref[p0:p0 + part, :], wd_ref[...])

    @pl.when(j == pl.num_programs(2) - 1)
    def _():
        scale = post_ref[...] * g_ref[0]
        for r0 in range(0, FFN_TM, NORM_ROWS):
            y = acc_ref[r0:r0 + NORM_ROWS, :]
            inv = lax.rsqrt(jnp.mean(y * y, axis=-1, keepdims=True) + EPS)
            o_ref[0, r0:r0 + NORM_ROWS, :] = x_ref[0, r0:r0 + NORM_ROWS, :] + y * inv * scale


def _ffn(x, sc, sh, g, pre, post, w_up, conv_p, w_down, l):
    nb, s, d = x.shape
    ff = w_down.shape[1]
    nf = ff // FFN_TF
    vec = pl.BlockSpec((1, 1, d), lambda b, i, j: (b, 0, 0))
    par = pl.BlockSpec((1, d), lambda b, i, j: (0, 0))
    return pl.pallas_call(
        _ffn_kernel,
        grid=(nb, s // FFN_TM, nf),
        in_specs=[
            pl.BlockSpec((1, FFN_TM, d), lambda b, i, j: (b, i, 0)),
            vec, vec, vec, par, par,
            pl.BlockSpec((None, d, 2 * FFN_TF), lambda b, i, j: (l, 0, j)),
            pl.BlockSpec((None, None, 2 * (CONV_W + 1), FFN_TF), lambda b, i, j: (l, j, 0, 0)),
            pl.BlockSpec((None, FFN_TF, d), lambda b, i, j: (l, j, 0)),
        ],
        out_specs=pl.BlockSpec((1, FFN_TM, d), lambda b, i, j: (b, i, 0)),
        out_shape=jax.ShapeDtypeStruct((nb, s, d), F32),
        scratch_shapes=[pltpu.VMEM((FFN_TM, d), BF16), pltpu.VMEM((FFN_TM, d), F32),
                        pltpu.VMEM((FFN_TM + HALO, FFN_TF), F32), pltpu.VMEM((FFN_TM + HALO, FFN_TF), F32),
                        pltpu.VMEM((FFN_TM, FFN_TF), BF16), pltpu.VMEM((nf, 2, HALO, FFN_TF), F32)],
        compiler_params=_params("parallel", "arbitrary", "arbitrary"),
        name="ffn",
    )(x, sc, sh, g, pre, post, w_up, conv_p, w_down)


def _pad_cols(a, n):
    return jnp.pad(a, [(0, 0)] * (a.ndim - 1) + [(0, n - a.shape[-1])])


def _cast_w_up_kernel(w_ref, o_ref):
    for h in range(2):
        @pl.when(pl.program_id(2) == h)
        def _():
            for j in range(D_FF_PAD // FFN_TF):
                width = min(FFN_TF, D_FF - j * FFN_TF)
                dst = (2 * j + h) * FFN_TF
                o_ref[0, :, dst:dst + width] = w_ref[0, :, j * FFN_TF:j * FFN_TF + width].astype(BF16)
                if width < FFN_TF:
                    o_ref[0, :, dst + width:dst + FFN_TF] = jnp.zeros((o_ref.shape[1], FFN_TF - width), BF16)


def _cast_w_up(w_up):
    nl, d, _ = w_up.shape
    return pl.pallas_call(
        _cast_w_up_kernel,
        grid=(nl, d // CAST_TM, 2),
        in_specs=[pl.BlockSpec((1, CAST_TM, D_FF), lambda l, i, h: (l, i, h))],
        out_specs=pl.BlockSpec((1, CAST_TM, 2 * D_FF_PAD), lambda l, i, h: (l, i, 0)),
        out_shape=jax.ShapeDtypeStruct((nl, d, 2 * D_FF_PAD), BF16),
        compiler_params=_params("parallel", "parallel", "arbitrary"),
        name="cast_w_up",
    )(w_up)


def _layout_w_in(w_in):
    o = 0
    parts = []
    for width in (S5_WIDTH, MLA_Q_RANK, MLA_KV_RANK, MLA_ROPE):
        parts.append(w_in[..., o:o + width])
        o += width
    parts.append(jnp.zeros(w_in.shape[:-1] + (COL_BLK - MLA_KV_RANK - MLA_ROPE,), w_in.dtype))
    parts.append(w_in[..., o:])
    return jnp.concatenate(parts, axis=-1).astype(BF16)


def _layout_w_uq(w):
    nl, r, _ = w.shape
    w = w.reshape(nl, r, MLA_HEADS, MLA_NOPE + MLA_ROPE)
    return _pad_cols(w, MLA_QK).reshape(nl, r, MLA_HEADS * MLA_QK).astype(BF16)


def _layout_w_ukv(w):
    nl, r, _ = w.shape
    w = w.reshape(nl, r, MLA_HEADS, MLA_NOPE + MLA_V)
    k = w[..., :MLA_NOPE].reshape(nl, r, MLA_HEADS * MLA_NOPE)
    v = w[..., MLA_NOPE:].reshape(nl, r, MLA_HEADS * MLA_V)
    return jnp.concatenate([k, v], axis=-1).astype(BF16)


def _rope_tables(positions):
    inv_freq = 1.0 / (ROPE_THETA ** (jnp.arange(0, MLA_ROPE, 2, dtype=F32) / MLA_ROPE))
    ang = positions.astype(F32)[..., None] * inv_freq
    cos, sin = jnp.cos(ang), jnp.sin(ang)
    z32 = jnp.zeros_like(cos)
    z64 = jnp.concatenate([z32, z32], axis=-1)
    return (jnp.concatenate([cos, cos, z64], axis=-1),
            jnp.concatenate([z32, sin, z64], axis=-1),
            jnp.concatenate([-sin, z32, z64], axis=-1))


def kernel(x, c, positions, w_in, s5_lambda_re, s5_lambda_im, s5_log_dt, s5_b_re, s5_b_im, s5_c_re, s5_c_im, s5_d, s5_w_glu, mla_q_norm, mla_w_uq, mla_kv_norm, mla_w_ukv, hg_lb_logits, hg_out_norm, w_out, mix_pre_norm, mix_post_norm, ffn_pre_norm, ffn_post_norm, ffn_w_up, ffn_conv_w, ffn_conv_b, ffn_w_down, w_ada, b_ada):
    nb, s, d = x.shape
    nl = w_in.shape[0]
    assert s % FLASH_T == 0 and s % FFN_TM == 0 and s % (S5_LS * SUBLANES) == 0

    mod = _adaln(c, w_ada, b_ada)
    cos_t, s1_t, s2_t = _rope_tables(positions)
    probs = jax.nn.softmax(hg_lb_logits.astype(F32), axis=0)
    lower_bounds = jnp.cumsum(probs, axis=0) - probs[0:1]

    w_in_p = _layout_w_in(w_in)
    w_uq_p = _layout_w_uq(mla_w_uq)
    w_ukv_p = _layout_w_ukv(mla_w_ukv)
    w_out_b = w_out.astype(BF16)
    w_glu_b = s5_w_glu.astype(BF16)
    w_up_p = _cast_w_up(ffn_w_up)
    conv_rows = jnp.concatenate([ffn_conv_w[..., :D_FF], ffn_conv_b[:, None, :D_FF],
                                 ffn_conv_w[..., D_FF:], ffn_conv_b[:, None, D_FF:]], axis=1)
    conv_p = _pad_cols(conv_rows, D_FF_PAD).reshape(nl, 2 * (CONV_W + 1), -1, FFN_TF).transpose(0, 2, 1, 3)
    w_down = jnp.pad(ffn_w_down, ((0, 0), (0, D_FF_PAD - D_FF), (0, 0))).astype(BF16)

    for l in range(nl):
        sh1, sc1, g1, sh2, sc2, g2 = [m[:, None, :] for m in jnp.split(mod[l], 6, axis=-1)]
        proj, u_rows = _proj_in(x, sc1, sh1, mix_pre_norm[l][None, :], w_in_p, l)
        bm, cm, lam, lam_chunk = _s5_prep(s5_lambda_re[l], s5_lambda_im[l], s5_log_dt[l],
                                          s5_b_re[l], s5_b_im[l], s5_c_re[l], s5_c_im[l])
        y_s5 = _s5(u_rows, bm, cm, lam, lam_chunk, s5_d[l][None, :], w_glu_b[l])
        q, k, v = _mla_prep(proj, mla_q_norm[l][None, :], mla_kv_norm[l][None, :], w_uq_p[l], w_ukv_p[l],
                            cos_t, s1_t, s2_t)
        y_mla = _flash(q, k, v)
        y_hg = _hgrn(proj, lower_bounds[l][None, :], hg_out_norm[l][None, :])
        x = _out_proj(y_s5, y_mla, y_hg, w_out_b, l, x, mix_post_norm[l][None, :], g1)
        x = _ffn(x, sc2, sh2, g2, ffn_pre_norm[l][None, :], ffn_post_norm[l][None, :],
                 w_up_p, conv_p, w_down, l)
    return x
```

```python
import functools
import math

import jax
import jax.numpy as jnp
from jax import lax
from jax.experimental import pallas as pl
from jax.experimental.pallas import tpu as pltpu

F32 = jnp.float32
BF16 = jnp.bfloat16

D_MODEL = 2048
DEPTH = 4
S5_WIDTH = 512
S5_GROUP = 16
S5_GROUPS = 32
S5_STATE = 64
MLA_HEADS = 8
MLA_NOPE = 128
MLA_ROPE = 64
MLA_V = 128
MLA_Q_RANK = 512
MLA_KV_RANK = 256
MLA_WIDTH = MLA_HEADS * MLA_V
ROPE_THETA = 10000.0
MASK_VALUE = -1e30
HG_HEADS = 4
HG_DK = 128
HG_DV = 128
HG_WIDTH = HG_HEADS * HG_DV
D_FF = 5504
CONV_W = 3
EPS = 1e-6

LANES = 128
SUBLANES = 8
MXU_DIM = 256
VMEM_LIMIT_BYTES = 56 * 1024 * 1024

COL_BLK = 512
N_COL_BLKS = 7
D_IN_PAD = N_COL_BLKS * COL_BLK
MLA_QK = 2 * LANES
D_FF_PAD = 5632

ADA_TN = 1024
PROJ_TM = 512
S5_LS = 16
S5_SEQS = 2
S5_HALF = S5_WIDTH // 2
S5_HSTATE = (S5_GROUPS // 2) * S5_STATE
PREP_TM = 512
FLASH_T = 512
FLASH_HEADS = 2
FLASH_ROWS = 32
HG_CHUNK = 128
HG_LEVELS = (16, 32, 64, 128)
HG_SEQS = 2
OUT_TM = 512
FFN_TM = 1024
FFN_TF = 512
FFN_ROWS = 64
FFN_PARTS = 1
NORM_ROWS = 16
CAST_TM = 256
HALO = SUBLANES


def _sigmoid(x):
    return 1.0 / (1.0 + jnp.exp(-x))


def _silu(x):
    return x * _sigmoid(x)


def _gelu_tanh(x):
    c = math.sqrt(2.0 / math.pi)
    return x * (0.5 * (1.0 + jnp.tanh(c * (x + 0.044715 * (x * x * x)))))


def _rms(x, gain):
    return x * lax.rsqrt(jnp.mean(x * x, axis=-1, keepdims=True) + EPS) * gain


def _dot(a, b):
    return jnp.dot(a, b, preferred_element_type=F32)


def _dot_nt(a, b):
    return lax.dot_general(a, b, (((1,), (1,)), ((), ())), preferred_element_type=F32)


def _dot_tn(a, b):
    return lax.dot_general(a, b, (((0,), (0,)), ((), ())), preferred_element_type=F32)


def _params(*sem):
    return pltpu.CompilerParams(dimension_semantics=sem, vmem_limit_bytes=VMEM_LIMIT_BYTES)


def _adaln_kernel(c_ref, w_ref, b_ref, o_ref):
    ca = _silu(c_ref[...]).astype(BF16)
    o_ref[0] = _dot(ca, w_ref[0].astype(BF16)) + b_ref[0]


def _adaln(c, w_ada, b_ada):
    nb, d = c.shape
    nl, _, n = w_ada.shape
    rows = -(-nb // SUBLANES) * SUBLANES
    c_pad = jnp.pad(c, ((0, rows - nb), (0, 0)))
    out = pl.pallas_call(
        _adaln_kernel,
        grid=(nl, n // ADA_TN),
        in_specs=[
            pl.BlockSpec((rows, d), lambda l, j: (0, 0)),
            pl.BlockSpec((1, d, ADA_TN), lambda l, j: (l, 0, j)),
            pl.BlockSpec((1, 1, ADA_TN), lambda l, j: (l, 0, j)),
        ],
        out_specs=pl.BlockSpec((1, rows, ADA_TN), lambda l, j: (l, 0, j)),
        out_shape=jax.ShapeDtypeStruct((nl, rows, n), F32),
        compiler_params=_params("parallel", "parallel"),
        name="adaln",
    )(c_pad, w_ada, b_ada.reshape(nl, 1, n))
    return out[:, :nb]


def _proj_in_kernel(x_ref, sc_ref, sh_ref, gain_ref, w_ref, o_ref, u_ref, h_ref, slab_ref):
    scale = gain_ref[...] * (1.0 + sc_ref[0])
    shift = sh_ref[0]
    part = PROJ_TM // 2
    for p0 in range(0, PROJ_TM, part):
        for r0 in range(p0, p0 + part, NORM_ROWS):
            x = x_ref[0, r0:r0 + NORM_ROWS, :]
            inv = lax.rsqrt(jnp.mean(x * x, axis=-1, keepdims=True) + EPS)
            h_ref[r0:r0 + NORM_ROWS, :] = (x * inv * scale + shift).astype(BF16)
        res = _dot(h_ref[p0:p0 + part, :], w_ref[...])
        o_ref[0, p0:p0 + part, :] = res
        for c in range(S5_WIDTH // LANES):
            slab_ref[c, p0:p0 + part, :] = res[:, c * LANES:(c + 1) * LANES]

    for t in range(S5_LS):
        for c in range(S5_WIDTH // LANES):
            col = t * S5_WIDTH + c * LANES
            u_ref[0, :, col:col + LANES] = slab_ref[c, pl.ds(t, PROJ_TM // S5_LS, stride=S5_LS), :]


def _proj_in(x, sc, sh, gain, w, l):
    nb, s, d = x.shape
    n = w.shape[2]
    rows = PROJ_TM // S5_LS
    return pl.pallas_call(
        _proj_in_kernel,
        grid=(nb, s // PROJ_TM),
        in_specs=[
            pl.BlockSpec((1, PROJ_TM, d), lambda b, i: (b, i, 0)),
            pl.BlockSpec((1, 1, d), lambda b, i: (b, 0, 0)),
            pl.BlockSpec((1, 1, d), lambda b, i: (b, 0, 0)),
            pl.BlockSpec((1, d), lambda b, i: (0, 0)),
            pl.BlockSpec((None, d, n), lambda b, i: (l, 0, 0), pipeline_mode=pl.Buffered(1)),
        ],
        out_specs=[pl.BlockSpec((1, PROJ_TM, n), lambda b, i: (b, i, 0)),
                   pl.BlockSpec((1, rows, S5_LS * S5_WIDTH), lambda b, i: (b, i, 0))],
        out_shape=[jax.ShapeDtypeStruct((nb, s, n), F32),
                   jax.ShapeDtypeStruct((nb, s // S5_LS, S5_LS * S5_WIDTH), F32)],
        scratch_shapes=[pltpu.VMEM((PROJ_TM, d), BF16), pltpu.VMEM((S5_WIDTH // LANES, PROJ_TM, LANES), F32)],
        compiler_params=_params("parallel", "parallel"),
        name="proj_in",
    )(x, sc, sh, gain, w)


def _s5_kernel(u_ref, bm_ref, cm_ref, lam_ref, lam_chunk_ref, d_ref, wglu_ref, o_ref,
               sre, sim, hre, him, yloc):
    ph = pl.program_id(1)
    s = pl.program_id(2)
    nseq, kc, _ = u_ref.shape
    rows = nseq * kc

    @pl.when(ph == 0)
    def _():
        @pl.when(s == 0)
        def _():
            sre[...] = jnp.zeros_like(sre)
            sim[...] = jnp.zeros_like(sim)

        u = u_ref[...].reshape(rows, S5_WIDTH)
        ub = u.astype(BF16)
        for h in range(2):
            cols = slice(h * S5_HALF, (h + 1) * S5_HALF)
            v = _dot(ub[:, cols], bm_ref[h])
            lr = lam_ref[h:h + 1, :]
            li = lam_ref[2 + h:3 + h, :]
            re = sre[h]
            im = sim[h]
            nre = lr * re - li * im + v[:, :S5_HSTATE]
            nim = lr * im + li * re + v[:, S5_HSTATE:]
            sre[h] = nre
            sim[h] = nim
            yl = _dot(nre.astype(BF16), cm_ref[h, 0]) + _dot(nim.astype(BF16), cm_ref[h, 1])
            yloc[s, :, cols] = yl + d_ref[:, cols] * u[:, cols]

        @pl.when(s == S5_LS - 1)
        def _():
            def body(k, carry):
                new = []
                for n in range(nseq):
                    row = pl.ds(n * kc + k, 1)
                    for h in range(2):
                        cre, cim = carry[4 * n + 2 * h], carry[4 * n + 2 * h + 1]
                        hre[h, row, :] = cre
                        him[h, row, :] = cim
                        lr = lam_chunk_ref[h:h + 1, :]
                        li = lam_chunk_ref[2 + h:3 + h, :]
                        new.append(lr * cre - li * cim + sre[h, row, :])
                        new.append(lr * cim + li * cre + sim[h, row, :])
                return tuple(new)

            zero = jnp.zeros((1, S5_HSTATE), F32)
            lax.fori_loop(0, kc, body, (zero,) * (4 * nseq))

    @pl.when(ph == 1)
    def _():
        ys = []
        for h in range(2):
            cols = slice(h * S5_HALF, (h + 1) * S5_HALF)
            lr = lam_ref[h:h + 1, :]
            li = lam_ref[2 + h:3 + h, :]
            pre = hre[h]
            pim = him[h]
            nre = lr * pre - li * pim
            nim = lr * pim + li * pre
            hre[h] = nre
            him[h] = nim
            yc = _dot(nre.astype(BF16), cm_ref[h, 0]) + _dot(nim.astype(BF16), cm_ref[h, 1])
            ys.append(yloc[s, :, cols] + yc)
        g = _gelu_tanh(jnp.concatenate(ys, axis=1))
        gate = _sigmoid(_dot(g.astype(BF16), wglu_ref[...]))
        o_ref[...] = (g * gate).reshape(o_ref.shape).astype(o_ref.dtype)


def _s5(u_rows, bm, cm, lam, lam_chunk, d, wglu):
    nb, kc, _ = u_rows.shape
    ns = S5_SEQS if nb % S5_SEQS == 0 else 1
    rows = ns * kc
    return pl.pallas_call(
        _s5_kernel,
        grid=(nb // ns, 2, S5_LS),
        in_specs=[
            pl.BlockSpec((ns, kc, S5_WIDTH),
                         lambda b, p, t: (b, 0, jnp.where(p == 0, t, S5_LS - 1))),
            pl.BlockSpec(bm.shape, lambda b, p, t: (0, 0, 0)),
            pl.BlockSpec(cm.shape, lambda b, p, t: (0, 0, 0, 0)),
            pl.BlockSpec(lam.shape, lambda b, p, t: (0, 0)),
            pl.BlockSpec(lam_chunk.shape, lambda b, p, t: (0, 0)),
            pl.BlockSpec(d.shape, lambda b, p, t: (0, 0)),
            pl.BlockSpec(wglu.shape, lambda b, p, t: (0, 0)),
        ],
        out_specs=pl.BlockSpec((ns, kc, S5_WIDTH), lambda b, p, t: (b, 0, t * p)),
        out_shape=jax.ShapeDtypeStruct((nb, kc, S5_LS * S5_WIDTH), F32),
        scratch_shapes=[pltpu.VMEM((2, rows, S5_HSTATE), F32)] * 4
        + [pltpu.VMEM((S5_LS, rows, S5_WIDTH), F32)],
        compiler_params=_params("parallel", "arbitrary", "arbitrary"),
        name="s5",
    )(u_rows, bm, cm, lam, lam_chunk, d, wglu)


def _s5_prep(lam_re, lam_im, log_dt, b_re, b_im, c_re, c_im):
    lam_re, lam_im = lam_re.astype(F32), lam_im.astype(F32)
    dt = jnp.exp(log_dt.astype(F32))[:, None]
    mag = jnp.exp(lam_re * dt)
    bar_re, bar_im = mag * jnp.cos(lam_im * dt), mag * jnp.sin(lam_im * dt)
    den = lam_re * lam_re + lam_im * lam_im
    co_re = ((bar_re - 1.0) * lam_re + bar_im * lam_im) / den
    co_im = (bar_im * lam_re - (bar_re - 1.0) * lam_im) / den
    b_re, b_im = b_re.astype(F32), b_im.astype(F32)
    bb_re = co_re[..., None] * b_re - co_im[..., None] * b_im
    bb_im = co_re[..., None] * b_im + co_im[..., None] * b_re
    ch_re, ch_im = bar_re, bar_im
    for _ in range(int(math.log2(S5_LS))):
        ch_re, ch_im = ch_re * ch_re - ch_im * ch_im, 2.0 * ch_re * ch_im
    gh = S5_GROUPS // 2
    eye = jnp.eye(gh, dtype=F32)

    def b_tiles(part):
        t = part.reshape(2, gh, S5_STATE, S5_GROUP)
        return jnp.einsum('ab,hapc->hacbp', eye, t).reshape(2, S5_HALF, S5_HSTATE)

    def c_tiles(part):
        t = part.reshape(2, gh, S5_GROUP, S5_STATE)
        return jnp.einsum('ab,hbcp->hapbc', eye, t).reshape(2, S5_HSTATE, S5_HALF)

    bm = jnp.concatenate([b_tiles(bb_re), b_tiles(bb_im)], axis=-1).astype(BF16)
    cm = jnp.stack([c_tiles(c_re.astype(F32)), -c_tiles(c_im.astype(F32))], axis=1).astype(BF16)

    def rows(re, im):
        return jnp.concatenate([re.reshape(2, S5_HSTATE), im.reshape(2, S5_HSTATE)], 0)

    return bm, cm, rows(bar_re, bar_im), rows(ch_re, ch_im)


def _mla_prep_kernel(cq_ref, ckv_ref, qn_ref, kvn_ref, wq_ref, wkv_ref, cos_ref, s1_ref, s2_ref,
                     q_ref, k_ref, v_ref):
    cos_t = cos_ref[0]
    s1 = s1_ref[0]
    s2 = s2_ref[0]
    half = MLA_ROPE // 2

    def rope(x):
        return x * cos_t + pltpu.roll(x, half, 1) * s1 + pltpu.roll(x, LANES - half, 1) * s2

    scale = (MLA_NOPE + MLA_ROPE) ** -0.5 * math.log2(math.e)
    q = _dot(_rms(cq_ref[0], qn_ref[...]).astype(BF16), wq_ref[...])
    blk = ckv_ref[0]
    kv = _dot(_rms(blk[:, :MLA_KV_RANK], kvn_ref[...]).astype(BF16), wkv_ref[...])
    kr = rope(blk[:, MLA_KV_RANK:MLA_KV_RANK + LANES]).astype(BF16)
    for h in range(MLA_HEADS):
        qn = q[:, h * MLA_QK:h * MLA_QK + MLA_NOPE] * scale
        qr = rope(q[:, h * MLA_QK + MLA_NOPE:(h + 1) * MLA_QK]) * scale
        q_ref[0, h] = jnp.concatenate([qn, qr], axis=1).astype(BF16)
        k_ref[0, h] = jnp.concatenate([kv[:, h * MLA_NOPE:(h + 1) * MLA_NOPE].astype(BF16), kr], axis=1)
        vh = kv[:, MLA_WIDTH + h * MLA_V:MLA_WIDTH + (h + 1) * MLA_V]
        v_ref[0, h] = jnp.concatenate([vh, jnp.ones_like(vh)], axis=1).astype(BF16)


def _mla_prep(proj, qn, kvn, wq, wkv, cos_t, s1_t, s2_t):
    nb, s, _ = proj.shape
    tab = pl.BlockSpec((1, PREP_TM, LANES), lambda b, i: (b, i, 0))
    return pl.pallas_call(
        _mla_prep_kernel,
        grid=(nb, s // PREP_TM),
        in_specs=[
            pl.BlockSpec((1, PREP_TM, COL_BLK), lambda b, i: (b, i, 1)),
            pl.BlockSpec((1, PREP_TM, COL_BLK), lambda b, i: (b, i, 2)),
            pl.BlockSpec(qn.shape, lambda b, i: (0, 0)),
            pl.BlockSpec(kvn.shape, lambda b, i: (0, 0)),
            pl.BlockSpec(wq.shape, lambda b, i: (0, 0)),
            pl.BlockSpec(wkv.shape, lambda b, i: (0, 0)),
            tab, tab, tab,
        ],
        out_specs=[
            pl.BlockSpec((1, MLA_HEADS, PREP_TM, MLA_QK), lambda b, i: (b, 0, i, 0)),
            pl.BlockSpec((1, MLA_HEADS, PREP_TM, MLA_QK), lambda b, i: (b, 0, i, 0)),
            pl.BlockSpec((1, MLA_HEADS, PREP_TM, 2 * MLA_V), lambda b, i: (b, 0, i, 0)),
        ],
        out_shape=[
            jax.ShapeDtypeStruct((nb, MLA_HEADS, s, MLA_QK), BF16),
            jax.ShapeDtypeStruct((nb, MLA_HEADS, s, MLA_QK), BF16),
            jax.ShapeDtypeStruct((nb, MLA_HEADS, s, 2 * MLA_V), BF16),
        ],
        compiler_params=_params("parallel", "parallel"),
        name="mla_prep",
    )(proj, proj, qn, kvn, wq, wkv, cos_t, s1_t, s2_t)


def _flash_kernel(q_ref, k_ref, v_ref, o_ref, s_ref, p_ref, m_ref, a_ref, acc_ref):
    qi = pl.program_id(2)
    t = FLASH_T
    heads = range(FLASH_HEADS)
    nh = FLASH_HEADS
    m_ref[...] = jnp.full(m_ref.shape, -jnp.inf, F32)
    acc_ref[...] = jnp.zeros_like(acc_ref)
    p_ref[nh:] = jnp.zeros((nh,) + p_ref.shape[1:], BF16)
    a_ref[nh:] = jnp.ones((nh,) + a_ref.shape[1:], F32)

    def scores(j, slot):
        start = pl.multiple_of(j * t, t)
        for h in heads:
            s_ref[slot * nh + h] = _dot_nt(q_ref[0, h], k_ref[0, h, pl.ds(start, t), :])

    def softmax(slot, diagonal):
        for h in heads:
            for r0 in range(0, t, FLASH_ROWS):
                rows = slice(r0, r0 + FLASH_ROWS)
                sc = s_ref[slot * nh + h, rows, :]
                if diagonal:
                    row = lax.broadcasted_iota(jnp.int32, sc.shape, 0) + r0
                    col = lax.broadcasted_iota(jnp.int32, sc.shape, 1)
                    sc = jnp.where(col <= row, sc, MASK_VALUE)
                m_old = m_ref[h, rows, :]
                m_new = jnp.maximum(m_old, jnp.max(sc, axis=-1, keepdims=True))
                p_ref[slot * nh + h, rows, :] = jnp.exp2(sc - m_new[:, :1]).astype(BF16)
                a_ref[slot * nh + h, rows, :] = jnp.exp2(m_old - m_new)
                m_ref[h, rows, :] = m_new

    def accumulate(j, slot):
        start = pl.multiple_of(j * t, t)
        for h in heads:
            pv = _dot(p_ref[slot * nh + h], v_ref[0, h, pl.ds(start, t), :])
            alpha = a_ref[slot * nh + h]
            acc_ref[h] = jnp.concatenate([alpha, alpha], axis=1) * acc_ref[h] + pv

    def stage(j, slot, with_accumulate=True):
        softmax(slot, False)
        scores(j + 1, 1 - slot)
        if with_accumulate:
            accumulate(jnp.maximum(j - 1, 0), 1 - slot)

    odd = lax.rem(qi, 2)
    for slot in range(2):
        @pl.when(odd == slot)
        def _():
            scores(0, slot)

    @pl.when(odd == 1)
    def _():
        stage(0, 1, with_accumulate=False)

    def body(jj, carry):
        j = 2 * jj + odd
        stage(j, 0)
        stage(j + 1, 1)
        return carry

    lax.fori_loop(0, qi // 2, body, 0)
    softmax(0, True)
    accumulate(jnp.maximum(qi - 1, 0), 1)
    accumulate(qi, 0)

    for h in heads:
        acc = acc_ref[h]
        o_ref[0, :, h * MLA_V:(h + 1) * MLA_V] = (acc[:, :MLA_V] / acc[:, MLA_V:]).astype(o_ref.dtype)


def _flash(q, k, v):
    nb, nh, s, _ = q.shape
    hb = FLASH_HEADS
    return pl.pallas_call(
        _flash_kernel,
        grid=(nb, nh // hb, s // FLASH_T),
        in_specs=[
            pl.BlockSpec((1, hb, FLASH_T, MLA_QK), lambda b, h, i: (b, h, i, 0)),
            pl.BlockSpec((1, hb, s, MLA_QK), lambda b, h, i: (b, h, 0, 0)),
            pl.BlockSpec((1, hb, s, 2 * MLA_V), lambda b, h, i: (b, h, 0, 0)),
        ],
        out_specs=pl.BlockSpec((1, FLASH_T, hb * MLA_V), lambda b, h, i: (b, i, h)),
        out_shape=jax.ShapeDtypeStruct((nb, s, nh * MLA_V), BF16),
        scratch_shapes=[pltpu.VMEM((2 * hb, FLASH_T, FLASH_T), F32), pltpu.VMEM((2 * hb, FLASH_T, FLASH_T), BF16),
                        pltpu.VMEM((hb, FLASH_T, LANES), F32), pltpu.VMEM((2 * hb, FLASH_T, LANES), F32),
                        pltpu.VMEM((hb, FLASH_T, 2 * MLA_V), F32)],
        compiler_params=_params("parallel", "parallel", "arbitrary"),
        name="flash",
    )(q, k, v)


def _cumsum_rows(x, ltri):
    hi = x.astype(BF16)
    r1 = x - hi.astype(F32)
    mid = r1.astype(BF16)
    lo = (r1 - mid.astype(F32)).astype(BF16)
    return _dot(ltri, hi) + _dot(ltri, mid) + _dot(ltri, lo)


def _hgrn_kernel(q_ref, f_ref, i_ref, g_ref, lb_ref, on_ref, ones_ref, ltri_ref, o_ref, st_ref):
    @pl.when(pl.program_id(1) == 0)
    def _():
        st_ref[...] = jnp.zeros_like(st_ref)

    for n in range(q_ref.shape[0]):
        _hgrn_chunk(n, q_ref, f_ref, i_ref, g_ref, lb_ref, on_ref, ones_ref, ltri_ref, o_ref, st_ref)


def _hgrn_chunk(n, q_ref, f_ref, i_ref, g_ref, lb_ref, on_ref, ones_ref, ltri_ref, o_ref, st_ref):
    c, w = HG_CHUNK, HG_WIDTH
    z = f_ref[n]
    lb = lb_ref[...]
    e = jnp.exp(-jnp.abs(z))
    r = 1.0 / (1.0 + e)
    pos = z >= 0.0
    sig_z = jnp.where(pos, r, e * r)
    sig_mz = jnp.where(pos, e * r, r)
    kk = (1.0 - lb) * sig_mz
    q = _silu(q_ref[n])
    v = i_ref[n]
    gate = _silu(g_ref[n])
    f = lb + (1.0 - lb) * sig_z
    b = _cumsum_rows(jnp.log2(f), ltri_ref[...])
    row = lax.broadcasted_iota(jnp.int32, (c, w), 0)

    q_lvls, k_lvls = [], []
    for m in HG_LEVELS:
        half = m // 2
        b4 = b.reshape(c // m, 2, half, w)
        ref_b = b4[:, 0, half - 1:half, :]
        k_first = kk.reshape(c // m, 2, half, w)[:, 0] * jnp.exp2(ref_b - b4[:, 0])
        q_second = q.reshape(c // m, 2, half, w)[:, 1] * jnp.exp2(b4[:, 1] - ref_b)
        zero = jnp.zeros_like(k_first)
        q_lvls.append(jnp.stack([zero, q_second], axis=1).reshape(c, w).astype(BF16))
        k_lvls.append(jnp.stack([k_first, zero], axis=1).reshape(c, w).astype(BF16))

    pos8 = row & (SUBLANES - 1)

    def back1(x):
        x3 = x.reshape(c // SUBLANES, SUBLANES, w)
        return pltpu.roll(x3, 1, 1).reshape(c, w)

    f_in = jnp.where(pos8 >= 1, f, 0.0)
    kd_dl, vr = kk, v
    o_diag = jnp.zeros((c, w), F32)
    for dl in range(SUBLANES):
        if dl > 0:
            kd_dl = f_in * back1(kd_dl)
            vr = back1(vr)
        wb = (q * kd_dl).astype(BF16)
        a = jnp.concatenate([_dot(wb[:, :MXU_DIM], ones_ref[...]), _dot(wb[:, MXU_DIM:], ones_ref[...])], axis=1)
        o_diag = o_diag + a * vr

    tt = lax.broadcasted_iota(jnp.int32, (c, c), 0)
    ss = lax.broadcasted_iota(jnp.int32, (c, c), 1)
    same_blk = {m: (tt // m) == (ss // m) for m in HG_LEVELS}

    b_last = b[c - 1:c, :]
    qe = (q * jnp.exp2(b)).astype(BF16)
    kd = (kk * jnp.exp2(b_last - b)).astype(BF16)
    decay = jnp.exp2(b_last)
    vb = v.astype(BF16)
    for h in range(HG_HEADS):
        hs = slice(h * HG_DK, (h + 1) * HG_DK)
        st = st_ref[n, h]
        attn = jnp.zeros((c, c), F32)
        for m, ql, kl in zip(HG_LEVELS, q_lvls, k_lvls):
            part = _dot_nt(ql[:, hs], kl[:, hs])
            attn = attn + (part if m == c else jnp.where(same_blk[m], part, 0.0))
        o_h =_dot_nt(qe[:, hs], st.astype(BF16)) + _dot(attn.astype(BF16), vb[:, hs]) + o_diag[:, hs]
        st_ref[n, h] = st * decay[:, hs] + _dot_tn(vb[:, hs], kd[:, hs])
        o_ref[n, :, hs] = (_rms(o_h, on_ref[...]) * gate[:, hs]).astype(o_ref.dtype)


def _hgrn(proj, lb, onorm):
    nb, s, _ = proj.shape
    head_id = jnp.arange(MXU_DIM) // HG_DK
    ones_blk = (head_id[:, None] == head_id[None, :]).astype(BF16)
    t_id = jnp.arange(HG_CHUNK)
    ltri = (t_id[:, None] >= t_id[None, :]).astype(BF16)
    ns = HG_SEQS if nb % HG_SEQS == 0 else 1
    spec = lambda cb: pl.BlockSpec((ns, HG_CHUNK, COL_BLK), lambda b, i: (b, i, cb))
    return pl.pallas_call(
        _hgrn_kernel,
        grid=(nb // ns, s // HG_CHUNK),
        in_specs=[spec(3), spec(4), spec(5), spec(6),
                  pl.BlockSpec(lb.shape, lambda b, i: (0, 0)),
                  pl.BlockSpec(onorm.shape, lambda b, i: (0, 0)),
                  pl.BlockSpec(ones_blk.shape, lambda b, i: (0, 0)),
                  pl.BlockSpec(ltri.shape, lambda b, i: (0, 0))],
        out_specs=pl.BlockSpec((ns, HG_CHUNK, HG_WIDTH), lambda b, i: (b, i, 0)),
        out_shape=jax.ShapeDtypeStruct((nb, s, HG_WIDTH), BF16),
        scratch_shapes=[pltpu.VMEM((ns, HG_HEADS, HG_DV, HG_DK), F32)],
        compiler_params=_params("parallel", "arbitrary"),
        name="hgrn",
    )(proj, proj, proj, proj, lb, onorm, ones_blk, ltri)


def _out_proj_kernel(ys_ref, ym_ref, yh_ref, w_ref, x_ref, gain_ref, g_ref, o_ref, slab_ref):
    for t in range(S5_LS):
        for c in range(S5_WIDTH // LANES):
            col = t * S5_WIDTH + c * LANES
            slab_ref[c, pl.ds(t, OUT_TM // S5_LS, stride=S5_LS), :] = ys_ref[0, :, col:col + LANES]
    scale = gain_ref[...] * g_ref[0]
    part = OUT_TM // 2
    for p0 in range(0, OUT_TM, part):
        rows = slice(p0, p0 + part)
        ys = jnp.concatenate([slab_ref[c, rows, :] for c in range(S5_WIDTH // LANES)], axis=1).astype(BF16)
        mixed = (_dot(ys, w_ref[:S5_WIDTH, :])
                 + _dot(ym_ref[0, rows, :], w_ref[S5_WIDTH:S5_WIDTH + MLA_WIDTH, :])
                 + _dot(yh_ref[0, rows, :], w_ref[S5_WIDTH + MLA_WIDTH:, :]))
        inv = lax.rsqrt(jnp.mean(mixed * mixed, axis=-1, keepdims=True) + EPS)
        o_ref[0, rows, :] = x_ref[0, rows, :] + mixed * inv * scale


def _out_proj(y_s5_rows, y_mla, y_hg, w, l, x, gain, g):
    nb, s, d = x.shape
    row = lambda n: pl.BlockSpec((1, OUT_TM, n), lambda b, i: (b, i, 0))
    return pl.pallas_call(
        _out_proj_kernel,
        grid=(nb, s // OUT_TM),
        in_specs=[pl.BlockSpec((1, OUT_TM // S5_LS, S5_LS * S5_WIDTH), lambda b, i: (b, i, 0)),
                  row(MLA_WIDTH), row(HG_WIDTH),
                  pl.BlockSpec((None,) + w.shape[1:], lambda b, i: (l, 0, 0), pipeline_mode=pl.Buffered(1)),
                  row(d),
                  pl.BlockSpec((1, d), lambda b, i: (0, 0)),
                  pl.BlockSpec((1, 1, d), lambda b, i: (b, 0, 0))],
        out_specs=row(d),
        out_shape=jax.ShapeDtypeStruct((nb, s, d), F32),
        scratch_shapes=[pltpu.VMEM((S5_WIDTH // LANES, OUT_TM, LANES), F32)],
        compiler_params=_params("parallel", "parallel"),
        name="out_proj",
    )(y_s5_rows, y_mla, y_hg, w, x, gain, g)


def _ffn_kernel(x_ref, sc_ref, sh_ref, g_ref, pre_ref, post_ref, wu_ref, cp_ref, wd_ref, o_ref,
                h_ref, acc_ref, ug_ref, uv_ref, act_ref, tail_ref):
    i = pl.program_id(1)
    j = pl.program_id(2)

    @pl.when(j == 0)
    def _():
        scale = pre_ref[...] * (1.0 + sc_ref[0])
        shift = sh_ref[0]
        for r0 in range(0, FFN_TM, NORM_ROWS):
            x = x_ref[0, r0:r0 + NORM_ROWS, :]
            inv = lax.rsqrt(jnp.mean(x * x, axis=-1, keepdims=True) + EPS)
            h_ref[r0:r0 + NORM_ROWS, :] = (x * inv * scale + shift).astype(BF16)
        acc_ref[...] = jnp.zeros_like(acc_ref)

    part = FFN_TM // FFN_PARTS
    for t, u_ref in enumerate((ug_ref, uv_ref)):
        u_ref[:HALO, :] = jnp.where(i == 0, 0.0, tail_ref[j, t])
    for p0 in range(0, FFN_TM, part):
        u = _dot(h_ref[p0:p0 + part, :], wu_ref[...])
        for t, u_ref in enumerate((ug_ref, uv_ref)):
            u_ref[HALO + p0:HALO + p0 + part, :] = u[:, t * FFN_TF:(t + 1) * FFN_TF]
            if p0 + part == FFN_TM:
                tail_ref[j, t] = u[part - HALO:, t * FFN_TF:(t + 1) * FFN_TF]

    def conv(u_ref, p0, r0):
        return (cp_ref[p0 + CONV_W:p0 + CONV_W + 1, :]
                + u_ref[r0 + HALO - 2:r0 + HALO - 2 + FFN_ROWS, :] * cp_ref[p0:p0 + 1, :]
                + u_ref[r0 + HALO - 1:r0 + HALO - 1 + FFN_ROWS, :] * cp_ref[p0 + 1:p0 + 2, :]
                + u_ref[r0 + HALO:r0 + HALO + FFN_ROWS, :] * cp_ref[p0 + 2:p0 + 3, :])

    for p0 in range(0, FFN_TM, part):
        for r0 in range(p0, p0 + part, FFN_ROWS):
            gate = conv(ug_ref, 0, r0)
            val = conv(uv_ref, CONV_W + 1, r0)
            act_ref[r0:r0 + FFN_ROWS, :] = (_gelu_tanh(gate) * val).astype(BF16)
        acc_ref[p0:p0 + part, :] += _dot(act_ref[p0:p0 + part, :], wd_ref[...])

    @pl.when(j == pl.num_programs(2) - 1)
    def _():
        scale = post_ref[...] * g_ref[0]
        for r0 in range(0, FFN_TM, NORM_ROWS):
            y = acc_ref[r0:r0 + NORM_ROWS, :]
            inv = lax.rsqrt(jnp.mean(y * y, axis=-1, keepdims=True) + EPS)
            o_ref[0, r0:r0 + NORM_ROWS, :] = x_ref[0, r0:r0 + NORM_ROWS, :] + y * inv * scale


def _ffn(x, sc, sh, g, pre, post, w_up, conv_p, w_down, l):
    nb, s, d = x.shape
    ff = w_down.shape[1]
    nf = ff // FFN_TF
    vec = pl.BlockSpec((1, 1, d), lambda b, i, j: (b, 0, 0))
    par = pl.BlockSpec((1, d), lambda b, i, j: (0, 0))
    return pl.pallas_call(
        _ffn_kernel,
        grid=(nb, s // FFN_TM, nf),
        in_specs=[
            pl.BlockSpec((1, FFN_TM, d), lambda b, i, j: (b, i, 0), pipeline_mode=pl.Buffered(1)),
            vec, vec, vec, par, par,
            pl.BlockSpec((None, d, 2 * FFN_TF), lambda b, i, j: (l, 0, j)),
            pl.BlockSpec((None, None, 2 * (CONV_W + 1), FFN_TF), lambda b, i, j: (l, j, 0, 0)),
            pl.BlockSpec((None, FFN_TF, d), lambda b, i, j: (l, j, 0)),
        ],
        out_specs=pl.BlockSpec((1, FFN_TM, d), lambda b, i, j: (b, i, 0), pipeline_mode=pl.Buffered(1)),
        out_shape=jax.ShapeDtypeStruct((nb, s, d), F32),
        scratch_shapes=[pltpu.VMEM((FFN_TM, d), BF16), pltpu.VMEM((FFN_TM, d), F32),
                        pltpu.VMEM((FFN_TM + HALO, FFN_TF), F32), pltpu.VMEM((FFN_TM + HALO, FFN_TF), F32),
                        pltpu.VMEM((FFN_TM, FFN_TF), BF16), pltpu.VMEM((nf, 2, HALO, FFN_TF), F32)],
        compiler_params=_params("parallel", "arbitrary", "arbitrary"),
        name="ffn",
    )(x, sc, sh, g, pre, post, w_up, conv_p, w_down)


def _pad_cols(a, n):
    return jnp.pad(a, [(0, 0)] * (a.ndim - 1) + [(0, n - a.shape[-1])])


def _cast_w_up_kernel(w_ref, o_ref):
    for h in range(2):
        @pl.when(pl.program_id(2) == h)
        def _():
            for j in range(D_FF_PAD // FFN_TF):
                width = min(FFN_TF, D_FF - j * FFN_TF)
                dst = (2 * j + h) * FFN_TF
                o_ref[0, :, dst:dst + width] = w_ref[0, :, j * FFN_TF:j * FFN_TF + width].astype(BF16)
                if width < FFN_TF:
                    o_ref[0, :, dst + width:dst + FFN_TF] = jnp.zeros((o_ref.shape[1], FFN_TF - width), BF16)


def _cast_w_up(w_up):
    nl, d, _ = w_up.shape
    return pl.pallas_call(
        _cast_w_up_kernel,
        grid=(nl, d // CAST_TM, 2),
        in_specs=[pl.BlockSpec((1, CAST_TM, D_FF), lambda l, i, h: (l, i, h))],
        out_specs=pl.BlockSpec((1, CAST_TM, 2 * D_FF_PAD), lambda l, i, h: (l, i, 0)),
        out_shape=jax.ShapeDtypeStruct((nl, d, 2 * D_FF_PAD), BF16),
        compiler_params=_params("parallel", "parallel", "arbitrary"),
        name="cast_w_up",
    )(w_up)


def _layout_w_in(w_in):
    o = 0
    parts = []
    for width in (S5_WIDTH, MLA_Q_RANK, MLA_KV_RANK, MLA_ROPE):
        parts.append(w_in[..., o:o + width])
        o += width
    parts.append(jnp.zeros(w_in.shape[:-1] + (COL_BLK - MLA_KV_RANK - MLA_ROPE,), w_in.dtype))
    parts.append(w_in[..., o:])
    return jnp.concatenate(parts, axis=-1).astype(BF16)


def _layout_w_uq(w):
    nl, r, _ = w.shape
    w = w.reshape(nl, r, MLA_HEADS, MLA_NOPE + MLA_ROPE)
    return _pad_cols(w, MLA_QK).reshape(nl, r, MLA_HEADS * MLA_QK).astype(BF16)


def _layout_w_ukv(w):
    nl, r, _ = w.shape
    w = w.reshape(nl, r, MLA_HEADS, MLA_NOPE + MLA_V)
    k = w[..., :MLA_NOPE].reshape(nl, r, MLA_HEADS * MLA_NOPE)
    v = w[..., MLA_NOPE:].reshape(nl, r, MLA_HEADS * MLA_V)
    return jnp.concatenate([k, v], axis=-1).astype(BF16)


def _rope_tables(positions):
    inv_freq = 1.0 / (ROPE_THETA ** (jnp.arange(0, MLA_ROPE, 2, dtype=F32) / MLA_ROPE))
    ang = positions.astype(F32)[..., None] * inv_freq
    cos, sin = jnp.cos(ang), jnp.sin(ang)
    z32 = jnp.zeros_like(cos)
    z64 = jnp.concatenate([z32, z32], axis=-1)
    return (jnp.concatenate([cos, cos, z64], axis=-1),
            jnp.concatenate([z32, sin, z64], axis=-1),
            jnp.concatenate([-sin, z32, z64], axis=-1))


def kernel(x, c, positions, w_in, s5_lambda_re, s5_lambda_im, s5_log_dt, s5_b_re, s5_b_im, s5_c_re, s5_c_im, s5_d, s5_w_glu, mla_q_norm, mla_w_uq, mla_kv_norm, mla_w_ukv, hg_lb_logits, hg_out_norm, w_out, mix_pre_norm, mix_post_norm, ffn_pre_norm, ffn_post_norm, ffn_w_up, ffn_conv_w, ffn_conv_b, ffn_w_down, w_ada, b_ada):
    nb, s, d = x.shape
    nl = w_in.shape[0]
    assert s % FLASH_T == 0 and s % FFN_TM == 0 and s % (S5_LS * SUBLANES) == 0

    mod = _adaln(c, w_ada, b_ada)
    cos_t, s1_t, s2_t = _rope_tables(positions)
    probs = jax.nn.softmax(hg_lb_logits.astype(F32), axis=0)
    lower_bounds = jnp.cumsum(probs, axis=0) - probs[0:1]

    w_in_p = _layout_w_in(w_in)
    w_uq_p = _layout_w_uq(mla_w_uq)
    w_ukv_p = _layout_w_ukv(mla_w_ukv)
    w_out_b = w_out.astype(BF16)
    w_glu_b = s5_w_glu.astype(BF16)
    w_up_p = _cast_w_up(ffn_w_up)
    conv_rows = jnp.concatenate([ffn_conv_w[..., :D_FF], ffn_conv_b[:, None, :D_FF],
                                 ffn_conv_w[..., D_FF:], ffn_conv_b[:, None, D_FF:]], axis=1)
    conv_p = _pad_cols(conv_rows, D_FF_PAD).reshape(nl, 2 * (CONV_W + 1), -1, FFN_TF).transpose(0, 2, 1, 3)
    w_down = jnp.pad(ffn_w_down, ((0, 0), (0, D_FF_PAD - D_FF), (0, 0))).astype(BF16)

    for l in range(nl):
        sh1, sc1, g1, sh2, sc2, g2 = [m[:, None, :] for m in jnp.split(mod[l], 6, axis=-1)]
        proj, u_rows = _proj_in(x, sc1, sh1, mix_pre_norm[l][None, :], w_in_p, l)
        bm, cm, lam, lam_chunk = _s5_prep(s5_lambda_re[l], s5_lambda_im[l], s5_log_dt[l],
                                          s5_b_re[l], s5_b_im[l], s5_c_re[l], s5_c_im[l])
        y_s5 = _s5(u_rows, bm, cm, lam, lam_chunk, s5_d[l][None, :], w_glu_b[l])
        q, k, v = _mla_prep(proj, mla_q_norm[l][None, :], mla_kv_norm[l][None, :], w_uq_p[l], w_ukv_p[l],
                            cos_t, s1_t, s2_t)
        y_mla = _flash(q, k, v)
        y_hg = _hgrn(proj, lower_bounds[l][None, :], hg_out_norm[l][None, :])
        x = _out_proj(y_s5, y_mla, y_hg, w_out_b, l, x, mix_post_norm[l][None, :], g1)
        x = _ffn(x, sc2, sh2, g2, ffn_pre_norm[l][None, :], ffn_post_norm[l][None, :],
                 w_up_p, conv_p, w_down, l)
    return x
```

```python
import functools
import math

import jax
import jax.numpy as jnp
from jax import lax
from jax.experimental import pallas as pl
from jax.experimental.pallas import tpu as pltpu

F32 = jnp.float32
BF16 = jnp.bfloat16

D_MODEL = 2048
DEPTH = 4
S5_WIDTH = 512
S5_GROUP = 16
S5_GROUPS = 32
S5_STATE = 64
MLA_HEADS = 8
MLA_NOPE = 128
MLA_ROPE = 64
MLA_V = 128
MLA_Q_RANK = 512
MLA_KV_RANK = 256
MLA_WIDTH = MLA_HEADS * MLA_V
ROPE_THETA = 10000.0
MASK_VALUE = -1e30
HG_HEADS = 4
HG_DK = 128
HG_DV = 128
HG_WIDTH = HG_HEADS * HG_DV
D_FF = 5504
CONV_W = 3
EPS = 1e-6

LANES = 128
SUBLANES = 8
MXU_DIM = 256
VMEM_LIMIT_BYTES = 56 * 1024 * 1024

COL_BLK = 512
N_COL_BLKS = 7
D_IN_PAD = N_COL_BLKS * COL_BLK
MLA_QK = 2 * LANES
D_FF_PAD = 5632

ADA_TN = 1024
PROJ_TM = 512
S5_LS = 16
S5_SEQS = 2
S5_HALF = S5_WIDTH // 2
S5_HSTATE = (S5_GROUPS // 2) * S5_STATE
PREP_TM = 512
FLASH_T = 512
FLASH_HEADS = 2
FLASH_ROWS = 32
HG_CHUNK = 128
HG_LEVELS = (16, 32, 64, 128)
HG_SEQS = 4
OUT_TM = 512
FFN_TM = 1024
FFN_TF = 512
FFN_ROWS = 64
FFN_PARTS = 1
NORM_ROWS = 16
CAST_TM = 256
HALO = SUBLANES


def _sigmoid(x):
    return 1.0 / (1.0 + jnp.exp(-x))


def _silu(x):
    return x * _sigmoid(x)


def _gelu_tanh(x):
    c = math.sqrt(2.0 / math.pi)
    return x * (0.5 * (1.0 + jnp.tanh(c * (x + 0.044715 * (x * x * x)))))


def _rms(x, gain):
    return x * lax.rsqrt(jnp.mean(x * x, axis=-1, keepdims=True) + EPS) * gain


def _dot(a, b):
    return jnp.dot(a, b, preferred_element_type=F32)


def _dot_nt(a, b):
    return lax.dot_general(a, b, (((1,), (1,)), ((), ())), preferred_element_type=F32)


def _dot_tn(a, b):
    return lax.dot_general(a, b, (((0,), (0,)), ((), ())), preferred_element_type=F32)


def _params(*sem):
    return pltpu.CompilerParams(dimension_semantics=sem, vmem_limit_bytes=VMEM_LIMIT_BYTES)


def _adaln_kernel(c_ref, w_ref, b_ref, o_ref):
    ca = _silu(c_ref[...]).astype(BF16)
    o_ref[0] = _dot(ca, w_ref[0].astype(BF16)) + b_ref[0]


def _adaln(c, w_ada, b_ada):
    nb, d = c.shape
    nl, _, n = w_ada.shape
    rows = -(-nb // SUBLANES) * SUBLANES
    c_pad = jnp.pad(c, ((0, rows - nb), (0, 0)))
    out = pl.pallas_call(
        _adaln_kernel,
        grid=(nl, n // ADA_TN),
        in_specs=[
            pl.BlockSpec((rows, d), lambda l, j: (0, 0)),
            pl.BlockSpec((1, d, ADA_TN), lambda l, j: (l, 0, j)),
            pl.BlockSpec((1, 1, ADA_TN), lambda l, j: (l, 0, j)),
        ],
        out_specs=pl.BlockSpec((1, rows, ADA_TN), lambda l, j: (l, 0, j)),
        out_shape=jax.ShapeDtypeStruct((nl, rows, n), F32),
        compiler_params=_params("parallel", "parallel"),
        name="adaln",
    )(c_pad, w_ada, b_ada.reshape(nl, 1, n))
    return out[:, :nb]


def _proj_in_kernel(x_ref, sc_ref, sh_ref, gain_ref, w_ref, o_ref, u_ref, h_ref, slab_ref):
    scale = gain_ref[...] * (1.0 + sc_ref[0])
    shift = sh_ref[0]
    part = PROJ_TM // 2
    for p0 in range(0, PROJ_TM, part):
        for r0 in range(p0, p0 + part, NORM_ROWS):
            x = x_ref[0, r0:r0 + NORM_ROWS, :]
            inv = lax.rsqrt(jnp.mean(x * x, axis=-1, keepdims=True) + EPS)
            h_ref[r0:r0 + NORM_ROWS, :] = (x * inv * scale + shift).astype(BF16)
        res = _dot(h_ref[p0:p0 + part, :], w_ref[...])
        o_ref[0, p0:p0 + part, :] = res
        for c in range(S5_WIDTH // LANES):
            slab_ref[c, p0:p0 + part, :] = res[:, c * LANES:(c + 1) * LANES]

    for t in range(S5_LS):
        for c in range(S5_WIDTH // LANES):
            col = t * S5_WIDTH + c * LANES
            u_ref[0, :, col:col + LANES] = slab_ref[c, pl.ds(t, PROJ_TM // S5_LS, stride=S5_LS), :]


def _proj_in(x, sc, sh, gain, w, l):
    nb, s, d = x.shape
    n = w.shape[2]
    rows = PROJ_TM // S5_LS
    return pl.pallas_call(
        _proj_in_kernel,
        grid=(nb, s // PROJ_TM),
        in_specs=[
            pl.BlockSpec((1, PROJ_TM, d), lambda b, i: (b, i, 0)),
            pl.BlockSpec((1, 1, d), lambda b, i: (b, 0, 0)),
            pl.BlockSpec((1, 1, d), lambda b, i: (b, 0, 0)),
            pl.BlockSpec((1, d), lambda b, i: (0, 0)),
            pl.BlockSpec((None, d, n), lambda b, i: (l, 0, 0), pipeline_mode=pl.Buffered(1)),
        ],
        out_specs=[pl.BlockSpec((1, PROJ_TM, n), lambda b, i: (b, i, 0)),
                   pl.BlockSpec((1, rows, S5_LS * S5_WIDTH), lambda b, i: (b, i, 0))],
        out_shape=[jax.ShapeDtypeStruct((nb, s, n), F32),
                   jax.ShapeDtypeStruct((nb, s // S5_LS, S5_LS * S5_WIDTH), F32)],
        scratch_shapes=[pltpu.VMEM((PROJ_TM, d), BF16), pltpu.VMEM((S5_WIDTH // LANES, PROJ_TM, LANES), F32)],
        compiler_params=_params("parallel", "parallel"),
        name="proj_in",
    )(x, sc, sh, gain, w)


def _s5_kernel(u_ref, bm_ref, cm_ref, lam_ref, lam_chunk_ref, d_ref, wglu_ref, o_ref,
               sre, sim, hre, him, yloc):
    ph = pl.program_id(1)
    s = pl.program_id(2)
    nseq, kc, _ = u_ref.shape
    rows = nseq * kc

    @pl.when(ph == 0)
    def _():
        @pl.when(s == 0)
        def _():
            sre[...] = jnp.zeros_like(sre)
            sim[...] = jnp.zeros_like(sim)

        u = u_ref[...].reshape(rows, S5_WIDTH)
        ub = u.astype(BF16)
        for h in range(2):
            cols = slice(h * S5_HALF, (h + 1) * S5_HALF)
            v = _dot(ub[:, cols], bm_ref[h])
            lr = lam_ref[h:h + 1, :]
            li = lam_ref[2 + h:3 + h, :]
            re = sre[h]
            im = sim[h]
            nre = lr * re - li * im + v[:, :S5_HSTATE]
            nim = lr * im + li * re + v[:, S5_HSTATE:]
            sre[h] = nre
            sim[h] = nim
            yl = _dot(nre.astype(BF16), cm_ref[h, 0]) + _dot(nim.astype(BF16), cm_ref[h, 1])
            yloc[s, :, cols] = yl + d_ref[:, cols] * u[:, cols]

        @pl.when(s == S5_LS - 1)
        def _():
            def body(k, carry):
                new = []
                for n in range(nseq):
                    row = pl.ds(n * kc + k, 1)
                    for h in range(2):
                        cre, cim = carry[4 * n + 2 * h], carry[4 * n + 2 * h + 1]
                        hre[h, row, :] = cre
                        him[h, row, :] = cim
                        lr = lam_chunk_ref[h:h + 1, :]
                        li = lam_chunk_ref[2 + h:3 + h, :]
                        new.append(lr * cre - li * cim + sre[h, row, :])
                        new.append(lr * cim + li * cre + sim[h, row, :])
                return tuple(new)

            zero = jnp.zeros((1, S5_HSTATE), F32)
            lax.fori_loop(0, kc, body, (zero,) * (4 * nseq))

    @pl.when(ph == 1)
    def _():
        ys = []
        for h in range(2):
            cols = slice(h * S5_HALF, (h + 1) * S5_HALF)
            lr = lam_ref[h:h + 1, :]
            li = lam_ref[2 + h:3 + h, :]
            pre = hre[h]
            pim = him[h]
            nre = lr * pre - li * pim
            nim = lr * pim + li * pre
            hre[h] = nre
            him[h] = nim
            yc = _dot(nre.astype(BF16), cm_ref[h, 0]) + _dot(nim.astype(BF16), cm_ref[h, 1])
            ys.append(yloc[s, :, cols] + yc)
        g = _gelu_tanh(jnp.concatenate(ys, axis=1))
        gate = _sigmoid(_dot(g.astype(BF16), wglu_ref[...]))
        o_ref[...] = (g * gate).reshape(o_ref.shape).astype(o_ref.dtype)


def _s5(u_rows, bm, cm, lam, lam_chunk, d, wglu):
    nb, kc, _ = u_rows.shape
    ns = S5_SEQS if nb % S5_SEQS == 0 else 1
    rows = ns * kc
    return pl.pallas_call(
        _s5_kernel,
        grid=(nb // ns, 2, S5_LS),
        in_specs=[
            pl.BlockSpec((ns, kc, S5_WIDTH),
                         lambda b, p, t: (b, 0, jnp.where(p == 0, t, S5_LS - 1))),
            pl.BlockSpec(bm.shape, lambda b, p, t: (0, 0, 0)),
            pl.BlockSpec(cm.shape, lambda b, p, t: (0, 0, 0, 0)),
            pl.BlockSpec(lam.shape, lambda b, p, t: (0, 0)),
            pl.BlockSpec(lam_chunk.shape, lambda b, p, t: (0, 0)),
            pl.BlockSpec(d.shape, lambda b, p, t: (0, 0)),
            pl.BlockSpec(wglu.shape, lambda b, p, t: (0, 0)),
        ],
        out_specs=pl.BlockSpec((ns, kc, S5_WIDTH), lambda b, p, t: (b, 0, t * p)),
        out_shape=jax.ShapeDtypeStruct((nb, kc, S5_LS * S5_WIDTH), F32),
        scratch_shapes=[pltpu.VMEM((2, rows, S5_HSTATE), F32)] * 4
        + [pltpu.VMEM((S5_LS, rows, S5_WIDTH), F32)],
        compiler_params=_params("parallel", "arbitrary", "arbitrary"),
        name="s5",
    )(u_rows, bm, cm, lam, lam_chunk, d, wglu)


def _s5_prep(lam_re, lam_im, log_dt, b_re, b_im, c_re, c_im):
    lam_re, lam_im = lam_re.astype(F32), lam_im.astype(F32)
    dt = jnp.exp(log_dt.astype(F32))[:, None]
    mag = jnp.exp(lam_re * dt)
    bar_re, bar_im = mag * jnp.cos(lam_im * dt), mag * jnp.sin(lam_im * dt)
    den = lam_re * lam_re + lam_im * lam_im
    co_re = ((bar_re - 1.0) * lam_re + bar_im * lam_im) / den
    co_im = (bar_im * lam_re - (bar_re - 1.0) * lam_im) / den
    b_re, b_im = b_re.astype(F32), b_im.astype(F32)
    bb_re = co_re[..., None] * b_re - co_im[..., None] * b_im
    bb_im = co_re[..., None] * b_im + co_im[..., None] * b_re
    ch_re, ch_im = bar_re, bar_im
    for _ in range(int(math.log2(S5_LS))):
        ch_re, ch_im = ch_re * ch_re - ch_im * ch_im, 2.0 * ch_re * ch_im
    gh = S5_GROUPS // 2
    eye = jnp.eye(gh, dtype=F32)

    def b_tiles(part):
        t = part.reshape(2, gh, S5_STATE, S5_GROUP)
        return jnp.einsum('ab,hapc->hacbp', eye, t).reshape(2, S5_HALF, S5_HSTATE)

    def c_tiles(part):
        t = part.reshape(2, gh, S5_GROUP, S5_STATE)
        return jnp.einsum('ab,hbcp->hapbc', eye, t).reshape(2, S5_HSTATE, S5_HALF)

    bm = jnp.concatenate([b_tiles(bb_re), b_tiles(bb_im)], axis=-1).astype(BF16)
    cm = jnp.stack([c_tiles(c_re.astype(F32)), -c_tiles(c_im.astype(F32))], axis=1).astype(BF16)

    def rows(re, im):
        return jnp.concatenate([re.reshape(2, S5_HSTATE), im.reshape(2, S5_HSTATE)], 0)

    return bm, cm, rows(bar_re, bar_im), rows(ch_re, ch_im)


def _mla_prep_kernel(cq_ref, ckv_ref, qn_ref, kvn_ref, wq_ref, wkv_ref, cos_ref, s1_ref, s2_ref,
                     q_ref, k_ref, v_ref):
    cos_t = cos_ref[0]
    s1 = s1_ref[0]
    s2 = s2_ref[0]
    half = MLA_ROPE // 2

    def rope(x):
        return x * cos_t + pltpu.roll(x, half, 1) * s1 + pltpu.roll(x, LANES - half, 1) * s2

    scale = (MLA_NOPE + MLA_ROPE) ** -0.5 * math.log2(math.e)
    q = _dot(_rms(cq_ref[0], qn_ref[...]).astype(BF16), wq_ref[...])
    blk = ckv_ref[0]
    kv = _dot(_rms(blk[:, :MLA_KV_RANK], kvn_ref[...]).astype(BF16), wkv_ref[...])
    kr = rope(blk[:, MLA_KV_RANK:MLA_KV_RANK + LANES]).astype(BF16)
    for h in range(MLA_HEADS):
        qn = q[:, h * MLA_QK:h * MLA_QK + MLA_NOPE] * scale
        qr = rope(q[:, h * MLA_QK + MLA_NOPE:(h + 1) * MLA_QK]) * scale
        q_ref[0, h] = jnp.concatenate([qn, qr], axis=1).astype(BF16)
        k_ref[0, h] = jnp.concatenate([kv[:, h * MLA_NOPE:(h + 1) * MLA_NOPE].astype(BF16), kr], axis=1)
        vh = kv[:, MLA_WIDTH + h * MLA_V:MLA_WIDTH + (h + 1) * MLA_V]
        v_ref[0, h] = jnp.concatenate([vh, jnp.ones_like(vh)], axis=1).astype(BF16)


def _mla_prep(proj, qn, kvn, wq, wkv, cos_t, s1_t, s2_t):
    nb, s, _ = proj.shape
    tab = pl.BlockSpec((1, PREP_TM, LANES), lambda b, i: (b, i, 0))
    return pl.pallas_call(
        _mla_prep_kernel,
        grid=(nb, s // PREP_TM),
        in_specs=[
            pl.BlockSpec((1, PREP_TM, COL_BLK), lambda b, i: (b, i, 1)),
            pl.BlockSpec((1, PREP_TM, COL_BLK), lambda b, i: (b, i, 2)),
            pl.BlockSpec(qn.shape, lambda b, i: (0, 0)),
            pl.BlockSpec(kvn.shape, lambda b, i: (0, 0)),
            pl.BlockSpec(wq.shape, lambda b, i: (0, 0)),
            pl.BlockSpec(wkv.shape, lambda b, i: (0, 0)),
            tab, tab, tab,
        ],
        out_specs=[
            pl.BlockSpec((1, MLA_HEADS, PREP_TM, MLA_QK), lambda b, i: (b, 0, i, 0)),
            pl.BlockSpec((1, MLA_HEADS, PREP_TM, MLA_QK), lambda b, i: (b, 0, i, 0)),
            pl.BlockSpec((1, MLA_HEADS, PREP_TM, 2 * MLA_V), lambda b, i: (b, 0, i, 0)),
        ],
        out_shape=[
            jax.ShapeDtypeStruct((nb, MLA_HEADS, s, MLA_QK), BF16),
            jax.ShapeDtypeStruct((nb, MLA_HEADS, s, MLA_QK), BF16),
            jax.ShapeDtypeStruct((nb, MLA_HEADS, s, 2 * MLA_V), BF16),
        ],
        compiler_params=_params("parallel", "parallel"),
        name="mla_prep",
    )(proj, proj, qn, kvn, wq, wkv, cos_t, s1_t, s2_t)


def _flash_kernel(q_ref, k_ref, v_ref, o_ref, s_ref, p_ref, m_ref, a_ref, acc_ref):
    qi = pl.program_id(2)
    t = FLASH_T
    heads = range(FLASH_HEADS)
    nh = FLASH_HEADS
    m_ref[...] = jnp.full(m_ref.shape, -jnp.inf, F32)
    acc_ref[...] = jnp.zeros_like(acc_ref)
    p_ref[nh:] = jnp.zeros((nh,) + p_ref.shape[1:], BF16)
    a_ref[nh:] = jnp.ones((nh,) + a_ref.shape[1:], F32)

    def scores(j, slot):
        start = pl.multiple_of(j * t, t)
        for h in heads:
            s_ref[slot * nh + h] = _dot_nt(q_ref[0, h], k_ref[0, h, pl.ds(start, t), :])

    def softmax(slot, diagonal):
        for h in heads:
            for r0 in range(0, t, FLASH_ROWS):
                rows = slice(r0, r0 + FLASH_ROWS)
                sc = s_ref[slot * nh + h, rows, :]
                if diagonal:
                    row = lax.broadcasted_iota(jnp.int32, sc.shape, 0) + r0
                    col = lax.broadcasted_iota(jnp.int32, sc.shape, 1)
                    sc = jnp.where(col <= row, sc, MASK_VALUE)
                m_old = m_ref[h, rows, :]
                m_new = jnp.maximum(m_old, jnp.max(sc, axis=-1, keepdims=True))
                p_ref[slot * nh + h, rows, :] = jnp.exp2(sc - m_new[:, :1]).astype(BF16)
                a_ref[slot * nh + h, rows, :] = jnp.exp2(m_old - m_new)
                m_ref[h, rows, :] = m_new

    def accumulate(j, slot):
        start = pl.multiple_of(j * t, t)
        for h in heads:
            pv = _dot(p_ref[slot * nh + h], v_ref[0, h, pl.ds(start, t), :])
            alpha = a_ref[slot * nh + h]
            acc_ref[h] = jnp.concatenate([alpha, alpha], axis=1) * acc_ref[h] + pv

    def stage(j, slot, with_accumulate=True):
        softmax(slot, False)
        scores(j + 1, 1 - slot)
        if with_accumulate:
            accumulate(jnp.maximum(j - 1, 0), 1 - slot)

    odd = lax.rem(qi, 2)
    for slot in range(2):
        @pl.when(odd == slot)
        def _():
            scores(0, slot)

    @pl.when(odd == 1)
    def _():
        stage(0, 1, with_accumulate=False)

    def body(jj, carry):
        j = 2 * jj + odd
        stage(j, 0)
        stage(j + 1, 1)
        return carry

    lax.fori_loop(0, qi // 2, body, 0)
    softmax(0, True)
    accumulate(jnp.maximum(qi - 1, 0), 1)
    accumulate(qi, 0)

    for h in heads:
        acc = acc_ref[h]
        o_ref[0, :, h * MLA_V:(h + 1) * MLA_V] = (acc[:, :MLA_V] / acc[:, MLA_V:]).astype(o_ref.dtype)


def _flash(q, k, v):
    nb, nh, s, _ = q.shape
    hb = FLASH_HEADS
    return pl.pallas_call(
        _flash_kernel,
        grid=(nb, nh // hb, s // FLASH_T),
        in_specs=[
            pl.BlockSpec((1, hb, FLASH_T, MLA_QK), lambda b, h, i: (b, h, i, 0)),
            pl.BlockSpec((1, hb, s, MLA_QK), lambda b, h, i: (b, h, 0, 0)),
            pl.BlockSpec((1, hb, s, 2 * MLA_V), lambda b, h, i: (b, h, 0, 0)),
        ],
        out_specs=pl.BlockSpec((1, FLASH_T, hb * MLA_V), lambda b, h, i: (b, i, h)),
        out_shape=jax.ShapeDtypeStruct((nb, s, nh * MLA_V), BF16),
        scratch_shapes=[pltpu.VMEM((2 * hb, FLASH_T, FLASH_T), F32), pltpu.VMEM((2 * hb, FLASH_T, FLASH_T), BF16),
                        pltpu.VMEM((hb, FLASH_T, LANES), F32), pltpu.VMEM((2 * hb, FLASH_T, LANES), F32),
                        pltpu.VMEM((hb, FLASH_T, 2 * MLA_V), F32)],
        compiler_params=_params("parallel", "parallel", "arbitrary"),
        name="flash",
    )(q, k, v)


def _cumsum_rows(x, ltri):
    hi = x.astype(BF16)
    r1 = x - hi.astype(F32)
    mid = r1.astype(BF16)
    lo = (r1 - mid.astype(F32)).astype(BF16)
    return _dot(ltri, hi) + _dot(ltri, mid) + _dot(ltri, lo)


def _hgrn_kernel(q_ref, f_ref, i_ref, g_ref, lb_ref, on_ref, ones_ref, ltri_ref, o_ref, st_ref):
    @pl.when(pl.program_id(1) == 0)
    def _():
        st_ref[...] = jnp.zeros_like(st_ref)

    for n in range(q_ref.shape[0]):
        _hgrn_chunk(n, q_ref, f_ref, i_ref, g_ref, lb_ref, on_ref, ones_ref, ltri_ref, o_ref, st_ref)


def _hgrn_chunk(n, q_ref, f_ref, i_ref, g_ref, lb_ref, on_ref, ones_ref, ltri_ref, o_ref, st_ref):
    c, w = HG_CHUNK, HG_WIDTH
    z = f_ref[n]
    lb = lb_ref[...]
    e = jnp.exp(-jnp.abs(z))
    r = 1.0 / (1.0 + e)
    pos = z >= 0.0
    sig_z = jnp.where(pos, r, e * r)
    sig_mz = jnp.where(pos, e * r, r)
    kk = (1.0 - lb) * sig_mz
    q = _silu(q_ref[n])
    v = i_ref[n]
    gate = _silu(g_ref[n])
    f = lb + (1.0 - lb) * sig_z
    b = _cumsum_rows(jnp.log2(f), ltri_ref[...])
    row = lax.broadcasted_iota(jnp.int32, (c, w), 0)

    q_lvls, k_lvls = [], []
    for m in HG_LEVELS:
        half = m // 2
        b4 = b.reshape(c // m, 2, half, w)
        ref_b = b4[:, 0, half - 1:half, :]
        k_first = kk.reshape(c // m, 2, half, w)[:, 0] * jnp.exp2(ref_b - b4[:, 0])
        q_second = q.reshape(c // m, 2, half, w)[:, 1] * jnp.exp2(b4[:, 1] - ref_b)
        zero = jnp.zeros_like(k_first)
        q_lvls.append(jnp.stack([zero, q_second], axis=1).reshape(c, w).astype(BF16))
        k_lvls.append(jnp.stack([k_first, zero], axis=1).reshape(c, w).astype(BF16))

    pos8 = row & (SUBLANES - 1)

    def back1(x):
        x3 = x.reshape(c // SUBLANES, SUBLANES, w)
        return pltpu.roll(x3, 1, 1).reshape(c, w)

    f_in = jnp.where(pos8 >= 1, f, 0.0)
    kd_dl, vr = kk, v
    o_diag = jnp.zeros((c, w), F32)
    for dl in range(SUBLANES):
        if dl > 0:
            kd_dl = f_in * back1(kd_dl)
            vr = back1(vr)
        wb = (q * kd_dl).astype(BF16)
        a = jnp.concatenate([_dot(wb[:, :MXU_DIM], ones_ref[...]), _dot(wb[:, MXU_DIM:], ones_ref[...])], axis=1)
        o_diag = o_diag + a * vr

    tt = lax.broadcasted_iota(jnp.int32, (c, c), 0)
    ss = lax.broadcasted_iota(jnp.int32, (c, c), 1)
    same_blk = {m: (tt // m) == (ss // m) for m in HG_LEVELS}

    b_last = b[c - 1:c, :]
    qe = (q * jnp.exp2(b)).astype(BF16)
    kd = (kk * jnp.exp2(b_last - b)).astype(BF16)
    decay = jnp.exp2(b_last)
    vb = v.astype(BF16)
    for h in range(HG_HEADS):
        hs = slice(h * HG_DK, (h + 1) * HG_DK)
        st = st_ref[n, h]
        attn = jnp.zeros((c, c), F32)
        for m, ql, kl in zip(HG_LEVELS, q_lvls, k_lvls):
            part = _dot_nt(ql[:, hs], kl[:, hs])
            attn = attn + (part if m == c else jnp.where(same_blk[m], part, 0.0))
        o_h =_dot_nt(qe[:, hs], st.astype(BF16)) + _dot(attn.astype(BF16), vb[:, hs]) + o_diag[:, hs]
        st_ref[n, h] = st * decay[:, hs] + _dot_tn(vb[:, hs], kd[:, hs])
        o_ref[n, :, hs] = (_rms(o_h, on_ref[...]) * gate[:, hs]).astype(o_ref.dtype)


def _hgrn(proj, lb, onorm):
    nb, s, _ = proj.shape
    head_id = jnp.arange(MXU_DIM) // HG_DK
    ones_blk = (head_id[:, None] == head_id[None, :]).astype(BF16)
    t_id = jnp.arange(HG_CHUNK)
    ltri = (t_id[:, None] >= t_id[None, :]).astype(BF16)
    ns = HG_SEQS if nb % HG_SEQS == 0 else 1
    spec = lambda cb: pl.BlockSpec((ns, HG_CHUNK, COL_BLK), lambda b, i: (b, i, cb))
    return pl.pallas_call(
        _hgrn_kernel,
        grid=(nb // ns, s // HG_CHUNK),
        in_specs=[spec(3), spec(4), spec(5), spec(6),
                  pl.BlockSpec(lb.shape, lambda b, i: (0, 0)),
                  pl.BlockSpec(onorm.shape, lambda b, i: (0, 0)),
                  pl.BlockSpec(ones_blk.shape, lambda b, i: (0, 0)),
                  pl.BlockSpec(ltri.shape, lambda b, i: (0, 0))],
        out_specs=pl.BlockSpec((ns, HG_CHUNK, HG_WIDTH), lambda b, i: (b, i, 0)),
        out_shape=jax.ShapeDtypeStruct((nb, s, HG_WIDTH), BF16),
        scratch_shapes=[pltpu.VMEM((ns, HG_HEADS, HG_DV, HG_DK), F32)],
        compiler_params=_params("parallel", "arbitrary"),
        name="hgrn",
    )(proj, proj, proj, proj, lb, onorm, ones_blk, ltri)


def _out_proj_kernel(ys_ref, ym_ref, yh_ref, w_ref, x_ref, gain_ref, g_ref, o_ref, slab_ref):
    for t in range(S5_LS):
        for c in range(S5_WIDTH // LANES):
            col = t * S5_WIDTH + c * LANES
            slab_ref[c, pl.ds(t, OUT_TM // S5_LS, stride=S5_LS), :] = ys_ref[0, :, col:col + LANES]
    scale = gain_ref[...] * g_ref[0]
    part = OUT_TM // 2
    for p0 in range(0, OUT_TM, part):
        rows = slice(p0, p0 + part)
        ys = jnp.concatenate([slab_ref[c, rows, :] for c in range(S5_WIDTH // LANES)], axis=1).astype(BF16)
        mixed = (_dot(ys, w_ref[:S5_WIDTH, :])
                 + _dot(ym_ref[0, rows, :], w_ref[S5_WIDTH:S5_WIDTH + MLA_WIDTH, :])
                 + _dot(yh_ref[0, rows, :], w_ref[S5_WIDTH + MLA_WIDTH:, :]))
        inv = lax.rsqrt(jnp.mean(mixed * mixed, axis=-1, keepdims=True) + EPS)
        o_ref[0, rows, :] = x_ref[0, rows, :] + mixed * inv * scale


def _out_proj(y_s5_rows, y_mla, y_hg, w, l, x, gain, g):
    nb, s, d = x.shape
    row = lambda n: pl.BlockSpec((1, OUT_TM, n), lambda b, i: (b, i, 0))
    return pl.pallas_call(
        _out_proj_kernel,
        grid=(nb, s // OUT_TM),
        in_specs=[pl.BlockSpec((1, OUT_TM // S5_LS, S5_LS * S5_WIDTH), lambda b, i: (b, i, 0)),
                  row(MLA_WIDTH), row(HG_WIDTH),
                  pl.BlockSpec((None,) + w.shape[1:], lambda b, i: (l, 0, 0), pipeline_mode=pl.Buffered(1)),
                  row(d),
                  pl.BlockSpec((1, d), lambda b, i: (0, 0)),
                  pl.BlockSpec((1, 1, d), lambda b, i: (b, 0, 0))],
        out_specs=row(d),
        out_shape=jax.ShapeDtypeStruct((nb, s, d), F32),
        scratch_shapes=[pltpu.VMEM((S5_WIDTH // LANES, OUT_TM, LANES), F32)],
        compiler_params=_params("parallel", "parallel"),
        name="out_proj",
    )(y_s5_rows, y_mla, y_hg, w, x, gain, g)


def _ffn_kernel(x_ref, sc_ref, sh_ref, g_ref, pre_ref, post_ref, wu_ref, cp_ref, wd_ref, o_ref,
                h_ref, ug_ref, uv_ref, act_ref, tail_ref):
    i = pl.program_id(1)
    j = pl.program_id(2)

    @pl.when(j == 0)
    def _():
        scale = pre_ref[...] * (1.0 + sc_ref[0])
        shift = sh_ref[0]
        for r0 in range(0, FFN_TM, NORM_ROWS):
            x = x_ref[0, r0:r0 + NORM_ROWS, :]
            inv = lax.rsqrt(jnp.mean(x * x, axis=-1, keepdims=True) + EPS)
            h_ref[r0:r0 + NORM_ROWS, :] = (x * inv * scale + shift).astype(BF16)
        o_ref[...] = jnp.zeros_like(o_ref)

    part = FFN_TM // FFN_PARTS
    for t, u_ref in enumerate((ug_ref, uv_ref)):
        u_ref[:HALO, :] = jnp.where(i == 0, 0.0, tail_ref[j, t])
    for p0 in range(0, FFN_TM, part):
        u = _dot(h_ref[p0:p0 + part, :], wu_ref[...])
        for t, u_ref in enumerate((ug_ref, uv_ref)):
            u_ref[HALO + p0:HALO + p0 + part, :] = u[:, t * FFN_TF:(t + 1) * FFN_TF]
            if p0 + part == FFN_TM:
                tail_ref[j, t] = u[part - HALO:, t * FFN_TF:(t + 1) * FFN_TF]

    def conv(u_ref, p0, r0):
        return (cp_ref[p0 + CONV_W:p0 + CONV_W + 1, :]
                + u_ref[r0 + HALO - 2:r0 + HALO - 2 + FFN_ROWS, :] * cp_ref[p0:p0 + 1, :]
                + u_ref[r0 + HALO - 1:r0 + HALO - 1 + FFN_ROWS, :] * cp_ref[p0 + 1:p0 + 2, :]
                + u_ref[r0 + HALO:r0 + HALO + FFN_ROWS, :] * cp_ref[p0 + 2:p0 + 3, :])

    for p0 in range(0, FFN_TM, part):
        for r0 in range(p0, p0 + part, FFN_ROWS):
            gate = conv(ug_ref, 0, r0)
            val = conv(uv_ref, CONV_W + 1, r0)
            act_ref[r0:r0 + FFN_ROWS, :] = (_gelu_tanh(gate) * val).astype(BF16)
        o_ref[0, p0:p0 + part, :] += _dot(act_ref[p0:p0 + part, :], wd_ref[...])

    @pl.when(j == pl.num_programs(2) - 1)
    def _():
        scale = post_ref[...] * g_ref[0]
        for r0 in range(0, FFN_TM, NORM_ROWS):
            y = o_ref[0, r0:r0 + NORM_ROWS, :]
            inv = lax.rsqrt(jnp.mean(y * y, axis=-1, keepdims=True) + EPS)
            o_ref[0, r0:r0 + NORM_ROWS, :] = x_ref[0, r0:r0 + NORM_ROWS, :] + y * inv * scale


def _ffn(x, sc, sh, g, pre, post, w_up, conv_p, w_down, l):
    nb, s, d = x.shape
    ff = w_down.shape[1]
    nf = ff // FFN_TF
    vec = pl.BlockSpec((1, 1, d), lambda b, i, j: (b, 0, 0))
    par = pl.BlockSpec((1, d), lambda b, i, j: (0, 0))
    return pl.pallas_call(
        _ffn_kernel,
        grid=(nb, s // FFN_TM, nf),
        in_specs=[
            pl.BlockSpec((1, FFN_TM, d), lambda b, i, j: (b, i, 0)),
            vec, vec, vec, par, par,
            pl.BlockSpec((None, d, 2 * FFN_TF), lambda b, i, j: (l, 0, j)),
            pl.BlockSpec((None, None, 2 * (CONV_W + 1), FFN_TF), lambda b, i, j: (l, j, 0, 0)),
            pl.BlockSpec((None, FFN_TF, d), lambda b, i, j: (l, j, 0)),
        ],
        out_specs=pl.BlockSpec((1, FFN_TM, d), lambda b, i, j: (b, i, 0), pipeline_mode=pl.Buffered(1)),
        out_shape=jax.ShapeDtypeStruct((nb, s, d), F32),
        scratch_shapes=[pltpu.VMEM((FFN_TM, d), BF16),
                        pltpu.VMEM((FFN_TM + HALO, FFN_TF), F32), pltpu.VMEM((FFN_TM + HALO, FFN_TF), F32),
                        pltpu.VMEM((FFN_TM, FFN_TF), BF16), pltpu.VMEM((nf, 2, HALO, FFN_TF), F32)],
        compiler_params=_params("parallel", "arbitrary", "arbitrary"),
        name="ffn",
    )(x, sc, sh, g, pre, post, w_up, conv_p, w_down)


def _pad_cols(a, n):
    return jnp.pad(a, [(0, 0)] * (a.ndim - 1) + [(0, n - a.shape[-1])])


def _cast_w_up_kernel(w_ref, o_ref):
    for h in range(2):
        @pl.when(pl.program_id(2) == h)
        def _():
            for j in range(D_FF_PAD // FFN_TF):
                width = min(FFN_TF, D_FF - j * FFN_TF)
                dst = (2 * j + h) * FFN_TF
                o_ref[0, :, dst:dst + width] = w_ref[0, :, j * FFN_TF:j * FFN_TF + width].astype(BF16)
                if width < FFN_TF:
                    o_ref[0, :, dst + width:dst + FFN_TF] = jnp.zeros((o_ref.shape[1], FFN_TF - width), BF16)


def _cast_w_up(w_up):
    nl, d, _ = w_up.shape
    return pl.pallas_call(
        _cast_w_up_kernel,
        grid=(nl, d // CAST_TM, 2),
        in_specs=[pl.BlockSpec((1, CAST_TM, D_FF), lambda l, i, h: (l, i, h))],
        out_specs=pl.BlockSpec((1, CAST_TM, 2 * D_FF_PAD), lambda l, i, h: (l, i, 0)),
        out_shape=jax.ShapeDtypeStruct((nl, d, 2 * D_FF_PAD), BF16),
        compiler_params=_params("parallel", "parallel", "arbitrary"),
        name="cast_w_up",
    )(w_up)


W_IN_HEAD = S5_WIDTH + MLA_Q_RANK + MLA_KV_RANK + MLA_ROPE


def _cast_w_in_kernel(w_ref, o_ref):
    w = w_ref[0]
    gap = jnp.zeros((w.shape[0], 3 * COL_BLK - W_IN_HEAD), w.dtype)
    o_ref[0] = jnp.concatenate([w[:, :W_IN_HEAD], gap, w[:, W_IN_HEAD:]], axis=1).astype(BF16)


def _layout_w_in(w_in):
    nl, d, n = w_in.shape
    return pl.pallas_call(
        _cast_w_in_kernel,
        grid=(nl, d // CAST_TM),
        in_specs=[pl.BlockSpec((1, CAST_TM, n), lambda l, i: (l, i, 0))],
        out_specs=pl.BlockSpec((1, CAST_TM, D_IN_PAD), lambda l, i: (l, i, 0)),
        out_shape=jax.ShapeDtypeStruct((nl, d, D_IN_PAD), BF16),
        compiler_params=_params("parallel", "parallel"),
        name="cast_w_in",
    )(w_in)


def _layout_w_uq(w):
    nl, r, _ = w.shape
    w = w.reshape(nl, r, MLA_HEADS, MLA_NOPE + MLA_ROPE)
    return _pad_cols(w, MLA_QK).reshape(nl, r, MLA_HEADS * MLA_QK).astype(BF16)


def _layout_w_ukv(w):
    nl, r, _ = w.shape
    w = w.reshape(nl, r, MLA_HEADS, MLA_NOPE + MLA_V)
    k = w[..., :MLA_NOPE].reshape(nl, r, MLA_HEADS * MLA_NOPE)
    v = w[..., MLA_NOPE:].reshape(nl, r, MLA_HEADS * MLA_V)
    return jnp.concatenate([k, v], axis=-1).astype(BF16)


def _rope_tables(positions):
    inv_freq = 1.0 / (ROPE_THETA ** (jnp.arange(0, MLA_ROPE, 2, dtype=F32) / MLA_ROPE))
    ang = positions.astype(F32)[..., None] * inv_freq
    cos, sin = jnp.cos(ang), jnp.sin(ang)
    z32 = jnp.zeros_like(cos)
    z64 = jnp.concatenate([z32, z32], axis=-1)
    return (jnp.concatenate([cos, cos, z64], axis=-1),
            jnp.concatenate([z32, sin, z64], axis=-1),
            jnp.concatenate([-sin, z32, z64], axis=-1))


def kernel(x, c, positions, w_in, s5_lambda_re, s5_lambda_im, s5_log_dt, s5_b_re, s5_b_im, s5_c_re, s5_c_im, s5_d, s5_w_glu, mla_q_norm, mla_w_uq, mla_kv_norm, mla_w_ukv, hg_lb_logits, hg_out_norm, w_out, mix_pre_norm, mix_post_norm, ffn_pre_norm, ffn_post_norm, ffn_w_up, ffn_conv_w, ffn_conv_b, ffn_w_down, w_ada, b_ada):
    nb, s, d = x.shape
    nl = w_in.shape[0]
    assert s % FLASH_T == 0 and s % FFN_TM == 0 and s % (S5_LS * SUBLANES) == 0

    mod = _adaln(c, w_ada, b_ada)
    cos_t, s1_t, s2_t = _rope_tables(positions)
    probs = jax.nn.softmax(hg_lb_logits.astype(F32), axis=0)
    lower_bounds = jnp.cumsum(probs, axis=0) - probs[0:1]

    w_in_p = _layout_w_in(w_in)
    w_uq_p = _layout_w_uq(mla_w_uq)
    w_ukv_p = _layout_w_ukv(mla_w_ukv)
    w_out_b = w_out.astype(BF16)
    w_glu_b = s5_w_glu.astype(BF16)
    w_up_p = _cast_w_up(ffn_w_up)
    conv_rows = jnp.concatenate([ffn_conv_w[..., :D_FF], ffn_conv_b[:, None, :D_FF],
                                 ffn_conv_w[..., D_FF:], ffn_conv_b[:, None, D_FF:]], axis=1)
    conv_p = _pad_cols(conv_rows, D_FF_PAD).reshape(nl, 2 * (CONV_W + 1), -1, FFN_TF).transpose(0, 2, 1, 3)
    w_down = jnp.pad(ffn_w_down, ((0, 0), (0, D_FF_PAD - D_FF), (0, 0))).astype(BF16)

    for l in range(nl):
        sh1, sc1, g1, sh2, sc2, g2 = [m[:, None, :] for m in jnp.split(mod[l], 6, axis=-1)]
        proj, u_rows = _proj_in(x, sc1, sh1, mix_pre_norm[l][None, :], w_in_p, l)
        bm, cm, lam, lam_chunk = _s5_prep(s5_lambda_re[l], s5_lambda_im[l], s5_log_dt[l],
                                          s5_b_re[l], s5_b_im[l], s5_c_re[l], s5_c_im[l])
        y_s5 = _s5(u_rows, bm, cm, lam, lam_chunk, s5_d[l][None, :], w_glu_b[l])
        q, k, v = _mla_prep(proj, mla_q_norm[l][None, :], mla_kv_norm[l][None, :], w_uq_p[l], w_ukv_p[l],
                            cos_t, s1_t, s2_t)
        y_mla = _flash(q, k, v)
        y_hg = _hgrn(proj, lower_bounds[l][None, :], hg_out_norm[l][None, :])
        x = _out_proj(y_s5, y_mla, y_hg, w_out_b, l, x, mix_post_norm[l][None, :], g1)
        x = _ffn(x, sc2, sh2, g2, ffn_pre_norm[l][None, :], ffn_post_norm[l][None, :],
                 w_up_p, conv_p, w_down, l)
    return x
```

```python
import functools
import math

import jax
import jax.numpy as jnp
from jax import lax
from jax.experimental import pallas as pl
from jax.experimental.pallas import tpu as pltpu

F32 = jnp.float32
BF16 = jnp.bfloat16

D_MODEL = 2048
DEPTH = 4
S5_WIDTH = 512
S5_GROUP = 16
S5_GROUPS = 32
S5_STATE = 64
MLA_HEADS = 8
MLA_NOPE = 128
MLA_ROPE = 64
MLA_V = 128
MLA_Q_RANK = 512
MLA_KV_RANK = 256
MLA_WIDTH = MLA_HEADS * MLA_V
ROPE_THETA = 10000.0
MASK_VALUE = -1e30
HG_HEADS = 4
HG_DK = 128
HG_DV = 128
HG_WIDTH = HG_HEADS * HG_DV
D_FF = 5504
CONV_W = 3
EPS = 1e-6

LANES = 128
SUBLANES = 8
MXU_DIM = 256
VMEM_LIMIT_BYTES = 56 * 1024 * 1024

COL_BLK = 512
N_COL_BLKS = 7
D_IN_PAD = N_COL_BLKS * COL_BLK
MLA_QK = 2 * LANES
D_FF_PAD = 5632

ADA_TN = 1024
PROJ_TM = 512
S5_LS = 16
S5_SEQS = 2
S5_HALF = S5_WIDTH // 2
S5_HSTATE = (S5_GROUPS // 2) * S5_STATE
PREP_TM = 512
FLASH_T = 512
FLASH_HEADS = 2
FLASH_ROWS = 32
HG_CHUNK = 128
HG_LEVELS = (16, 32, 64, 128)
HG_SEQS = 4
OUT_TM = 512
FFN_TM = 1024
FFN_TF = 512
FFN_ROWS = 64
FFN_SPLITS = 1
NORM_ROWS = 16
CAST_TM = 256
HALO = SUBLANES


def _sigmoid(x):
    return 1.0 / (1.0 + jnp.exp(-x))


def _silu(x):
    return x * _sigmoid(x)


def _gelu_tanh(x):
    c = math.sqrt(2.0 / math.pi)
    return x * (0.5 * (1.0 + jnp.tanh(c * (x + 0.044715 * (x * x * x)))))


def _rms(x, gain):
    return x * lax.rsqrt(jnp.mean(x * x, axis=-1, keepdims=True) + EPS) * gain


def _dot(a, b):
    return jnp.dot(a, b, preferred_element_type=F32)


def _dot_nt(a, b):
    return lax.dot_general(a, b, (((1,), (1,)), ((), ())), preferred_element_type=F32)


def _dot_tn(a, b):
    return lax.dot_general(a, b, (((0,), (0,)), ((), ())), preferred_element_type=F32)


def _params(*sem):
    return pltpu.CompilerParams(dimension_semantics=sem, vmem_limit_bytes=VMEM_LIMIT_BYTES)


def _adaln_kernel(c_ref, w_ref, b_ref, o_ref):
    ca = _silu(c_ref[...]).astype(BF16)
    o_ref[0] = _dot(ca, w_ref[0].astype(BF16)) + b_ref[0]


def _adaln(c, w_ada, b_ada):
    nb, d = c.shape
    nl, _, n = w_ada.shape
    rows = -(-nb // SUBLANES) * SUBLANES
    c_pad = jnp.pad(c, ((0, rows - nb), (0, 0)))
    out = pl.pallas_call(
        _adaln_kernel,
        grid=(nl, n // ADA_TN),
        in_specs=[
            pl.BlockSpec((rows, d), lambda l, j: (0, 0)),
            pl.BlockSpec((1, d, ADA_TN), lambda l, j: (l, 0, j)),
            pl.BlockSpec((1, 1, ADA_TN), lambda l, j: (l, 0, j)),
        ],
        out_specs=pl.BlockSpec((1, rows, ADA_TN), lambda l, j: (l, 0, j)),
        out_shape=jax.ShapeDtypeStruct((nl, rows, n), F32),
        compiler_params=_params("parallel", "parallel"),
        name="adaln",
    )(c_pad, w_ada, b_ada.reshape(nl, 1, n))
    return out[:, :nb]


def _proj_in_kernel(x_ref, sc_ref, sh_ref, gain_ref, w_ref, o_ref, u_ref, h_ref, slab_ref):
    scale = gain_ref[...] * (1.0 + sc_ref[0])
    shift = sh_ref[0]
    part = PROJ_TM // 2
    for p0 in range(0, PROJ_TM, part):
        for r0 in range(p0, p0 + part, NORM_ROWS):
            x = x_ref[0, r0:r0 + NORM_ROWS, :]
            inv = lax.rsqrt(jnp.mean(x * x, axis=-1, keepdims=True) + EPS)
            h_ref[r0:r0 + NORM_ROWS, :] = (x * inv * scale + shift).astype(BF16)
        res = _dot(h_ref[p0:p0 + part, :], w_ref[...])
        o_ref[0, p0:p0 + part, :] = res
        for c in range(S5_WIDTH // LANES):
            slab_ref[c, p0:p0 + part, :] = res[:, c * LANES:(c + 1) * LANES]

    for t in range(S5_LS):
        for c in range(S5_WIDTH // LANES):
            col = t * S5_WIDTH + c * LANES
            u_ref[0, :, col:col + LANES] = slab_ref[c, pl.ds(t, PROJ_TM // S5_LS, stride=S5_LS), :]


def _proj_in(x, sc, sh, gain, w, l):
    nb, s, d = x.shape
    n = w.shape[2]
    rows = PROJ_TM // S5_LS
    return pl.pallas_call(
        _proj_in_kernel,
        grid=(nb, s // PROJ_TM),
        in_specs=[
            pl.BlockSpec((1, PROJ_TM, d), lambda b, i: (b, i, 0)),
            pl.BlockSpec((1, 1, d), lambda b, i: (b, 0, 0)),
            pl.BlockSpec((1, 1, d), lambda b, i: (b, 0, 0)),
            pl.BlockSpec((1, d), lambda b, i: (0, 0)),
            pl.BlockSpec((None, d, n), lambda b, i: (l, 0, 0), pipeline_mode=pl.Buffered(1)),
        ],
        out_specs=[pl.BlockSpec((1, PROJ_TM, n), lambda b, i: (b, i, 0)),
                   pl.BlockSpec((1, rows, S5_LS * S5_WIDTH), lambda b, i: (b, i, 0))],
        out_shape=[jax.ShapeDtypeStruct((nb, s, n), F32),
                   jax.ShapeDtypeStruct((nb, s // S5_LS, S5_LS * S5_WIDTH), F32)],
        scratch_shapes=[pltpu.VMEM((PROJ_TM, d), BF16), pltpu.VMEM((S5_WIDTH // LANES, PROJ_TM, LANES), F32)],
        compiler_params=_params("parallel", "parallel"),
        name="proj_in",
    )(x, sc, sh, gain, w)


def _s5_kernel(u_ref, bm_ref, cm_ref, lam_ref, lam_chunk_ref, d_ref, wglu_ref, o_ref,
               sre, sim, hre, him, yloc):
    ph = pl.program_id(1)
    s = pl.program_id(2)
    nseq, kc, _ = u_ref.shape
    rows = nseq * kc

    @pl.when(ph == 0)
    def _():
        @pl.when(s == 0)
        def _():
            sre[...] = jnp.zeros_like(sre)
            sim[...] = jnp.zeros_like(sim)

        u = u_ref[...].reshape(rows, S5_WIDTH)
        ub = u.astype(BF16)
        for h in range(2):
            cols = slice(h * S5_HALF, (h + 1) * S5_HALF)
            v = _dot(ub[:, cols], bm_ref[h])
            lr = lam_ref[h:h + 1, :]
            li = lam_ref[2 + h:3 + h, :]
            re = sre[h]
            im = sim[h]
            nre = lr * re - li * im + v[:, :S5_HSTATE]
            nim = lr * im + li * re + v[:, S5_HSTATE:]
            sre[h] = nre
            sim[h] = nim
            yl = _dot(nre.astype(BF16), cm_ref[h, 0]) + _dot(nim.astype(BF16), cm_ref[h, 1])
            yloc[s, :, cols] = yl + d_ref[:, cols] * u[:, cols]

        @pl.when(s == S5_LS - 1)
        def _():
            def body(k, carry):
                new = []
                for n in range(nseq):
                    row = pl.ds(n * kc + k, 1)
                    for h in range(2):
                        cre, cim = carry[4 * n + 2 * h], carry[4 * n + 2 * h + 1]
                        hre[h, row, :] = cre
                        him[h, row, :] = cim
                        lr = lam_chunk_ref[h:h + 1, :]
                        li = lam_chunk_ref[2 + h:3 + h, :]
                        new.append(lr * cre - li * cim + sre[h, row, :])
                        new.append(lr * cim + li * cre + sim[h, row, :])
                return tuple(new)

            zero = jnp.zeros((1, S5_HSTATE), F32)
            lax.fori_loop(0, kc, body, (zero,) * (4 * nseq))

    @pl.when(ph == 1)
    def _():
        ys = []
        for h in range(2):
            cols = slice(h * S5_HALF, (h + 1) * S5_HALF)
            lr = lam_ref[h:h + 1, :]
            li = lam_ref[2 + h:3 + h, :]
            pre = hre[h]
            pim = him[h]
            nre = lr * pre - li * pim
            nim = lr * pim + li * pre
            hre[h] = nre
            him[h] = nim
            yc = _dot(nre.astype(BF16), cm_ref[h, 0]) + _dot(nim.astype(BF16), cm_ref[h, 1])
            ys.append(yloc[s, :, cols] + yc)
        g = _gelu_tanh(jnp.concatenate(ys, axis=1))
        gate = _sigmoid(_dot(g.astype(BF16), wglu_ref[...]))
        o_ref[...] = (g * gate).reshape(o_ref.shape).astype(o_ref.dtype)


def _s5(u_rows, bm, cm, lam, lam_chunk, d, wglu):
    nb, kc, _ = u_rows.shape
    ns = S5_SEQS if nb % S5_SEQS == 0 else 1
    rows = ns * kc
    return pl.pallas_call(
        _s5_kernel,
        grid=(nb // ns, 2, S5_LS),
        in_specs=[
            pl.BlockSpec((ns, kc, S5_WIDTH),
                         lambda b, p, t: (b, 0, jnp.where(p == 0, t, S5_LS - 1))),
            pl.BlockSpec(bm.shape, lambda b, p, t: (0, 0, 0)),
            pl.BlockSpec(cm.shape, lambda b, p, t: (0, 0, 0, 0)),
            pl.BlockSpec(lam.shape, lambda b, p, t: (0, 0)),
            pl.BlockSpec(lam_chunk.shape, lambda b, p, t: (0, 0)),
            pl.BlockSpec(d.shape, lambda b, p, t: (0, 0)),
            pl.BlockSpec(wglu.shape, lambda b, p, t: (0, 0)),
        ],
        out_specs=pl.BlockSpec((ns, kc, S5_WIDTH), lambda b, p, t: (b, 0, t * p)),
        out_shape=jax.ShapeDtypeStruct((nb, kc, S5_LS * S5_WIDTH), F32),
        scratch_shapes=[pltpu.VMEM((2, rows, S5_HSTATE), F32)] * 4
        + [pltpu.VMEM((S5_LS, rows, S5_WIDTH), F32)],
        compiler_params=_params("parallel", "arbitrary", "arbitrary"),
        name="s5",
    )(u_rows, bm, cm, lam, lam_chunk, d, wglu)


def _s5_prep(lam_re, lam_im, log_dt, b_re, b_im, c_re, c_im):
    lam_re, lam_im = lam_re.astype(F32), lam_im.astype(F32)
    dt = jnp.exp(log_dt.astype(F32))[:, None]
    mag = jnp.exp(lam_re * dt)
    bar_re, bar_im = mag * jnp.cos(lam_im * dt), mag * jnp.sin(lam_im * dt)
    den = lam_re * lam_re + lam_im * lam_im
    co_re = ((bar_re - 1.0) * lam_re + bar_im * lam_im) / den
    co_im = (bar_im * lam_re - (bar_re - 1.0) * lam_im) / den
    b_re, b_im = b_re.astype(F32), b_im.astype(F32)
    bb_re = co_re[..., None] * b_re - co_im[..., None] * b_im
    bb_im = co_re[..., None] * b_im + co_im[..., None] * b_re
    ch_re, ch_im = bar_re, bar_im
    for _ in range(int(math.log2(S5_LS))):
        ch_re, ch_im = ch_re * ch_re - ch_im * ch_im, 2.0 * ch_re * ch_im
    gh = S5_GROUPS // 2
    eye = jnp.eye(gh, dtype=F32)

    def b_tiles(part):
        t = part.reshape(2, gh, S5_STATE, S5_GROUP)
        return jnp.einsum('ab,hapc->hacbp', eye, t).reshape(2, S5_HALF, S5_HSTATE)

    def c_tiles(part):
        t = part.reshape(2, gh, S5_GROUP, S5_STATE)
        return jnp.einsum('ab,hbcp->hapbc', eye, t).reshape(2, S5_HSTATE, S5_HALF)

    bm = jnp.concatenate([b_tiles(bb_re), b_tiles(bb_im)], axis=-1).astype(BF16)
    cm = jnp.stack([c_tiles(c_re.astype(F32)), -c_tiles(c_im.astype(F32))], axis=1).astype(BF16)

    def rows(re, im):
        return jnp.concatenate([re.reshape(2, S5_HSTATE), im.reshape(2, S5_HSTATE)], 0)

    return bm, cm, rows(bar_re, bar_im), rows(ch_re, ch_im)


def _mla_prep_kernel(cq_ref, ckv_ref, qn_ref, kvn_ref, wq_ref, wkv_ref, cos_ref, s1_ref, s2_ref,
                     q_ref, k_ref, v_ref):
    cos_t = cos_ref[0]
    s1 = s1_ref[0]
    s2 = s2_ref[0]
    half = MLA_ROPE // 2

    def rope(x):
        return x * cos_t + pltpu.roll(x, half, 1) * s1 + pltpu.roll(x, LANES - half, 1) * s2

    scale = (MLA_NOPE + MLA_ROPE) ** -0.5 * math.log2(math.e)
    q = _dot(_rms(cq_ref[0], qn_ref[...]).astype(BF16), wq_ref[...])
    blk = ckv_ref[0]
    kv = _dot(_rms(blk[:, :MLA_KV_RANK], kvn_ref[...]).astype(BF16), wkv_ref[...])
    kr = rope(blk[:, MLA_KV_RANK:MLA_KV_RANK + LANES]).astype(BF16)
    for h in range(MLA_HEADS):
        qn = q[:, h * MLA_QK:h * MLA_QK + MLA_NOPE] * scale
        qr = rope(q[:, h * MLA_QK + MLA_NOPE:(h + 1) * MLA_QK]) * scale
        q_ref[0, h] = jnp.concatenate([qn, qr], axis=1).astype(BF16)
        k_ref[0, h] = jnp.concatenate([kv[:, h * MLA_NOPE:(h + 1) * MLA_NOPE].astype(BF16), kr], axis=1)
        vh = kv[:, MLA_WIDTH + h * MLA_V:MLA_WIDTH + (h + 1) * MLA_V]
        v_ref[0, h] = jnp.concatenate([vh, jnp.ones_like(vh)], axis=1).astype(BF16)


def _mla_prep(proj, qn, kvn, wq, wkv, cos_t, s1_t, s2_t):
    nb, s, _ = proj.shape
    tab = pl.BlockSpec((1, PREP_TM, LANES), lambda b, i: (b, i, 0))
    return pl.pallas_call(
        _mla_prep_kernel,
        grid=(nb, s // PREP_TM),
        in_specs=[
            pl.BlockSpec((1, PREP_TM, COL_BLK), lambda b, i: (b, i, 1)),
            pl.BlockSpec((1, PREP_TM, COL_BLK), lambda b, i: (b, i, 2)),
            pl.BlockSpec(qn.shape, lambda b, i: (0, 0)),
            pl.BlockSpec(kvn.shape, lambda b, i: (0, 0)),
            pl.BlockSpec(wq.shape, lambda b, i: (0, 0)),
            pl.BlockSpec(wkv.shape, lambda b, i: (0, 0)),
            tab, tab, tab,
        ],
        out_specs=[
            pl.BlockSpec((1, MLA_HEADS, PREP_TM, MLA_QK), lambda b, i: (b, 0, i, 0)),
            pl.BlockSpec((1, MLA_HEADS, PREP_TM, MLA_QK), lambda b, i: (b, 0, i, 0)),
            pl.BlockSpec((1, MLA_HEADS, PREP_TM, 2 * MLA_V), lambda b, i: (b, 0, i, 0)),
        ],
        out_shape=[
            jax.ShapeDtypeStruct((nb, MLA_HEADS, s, MLA_QK), BF16),
            jax.ShapeDtypeStruct((nb, MLA_HEADS, s, MLA_QK), BF16),
            jax.ShapeDtypeStruct((nb, MLA_HEADS, s, 2 * MLA_V), BF16),
        ],
        compiler_params=_params("parallel", "parallel"),
        name="mla_prep",
    )(proj, proj, qn, kvn, wq, wkv, cos_t, s1_t, s2_t)


def _flash_kernel(q_ref, k_ref, v_ref, o_ref, s_ref, p_ref, m_ref, a_ref, acc_ref):
    qi = pl.program_id(2)
    t = FLASH_T
    heads = range(FLASH_HEADS)
    nh = FLASH_HEADS
    m_ref[...] = jnp.full(m_ref.shape, -jnp.inf, F32)
    acc_ref[...] = jnp.zeros_like(acc_ref)
    p_ref[nh:] = jnp.zeros((nh,) + p_ref.shape[1:], BF16)
    a_ref[nh:] = jnp.ones((nh,) + a_ref.shape[1:], F32)

    def scores(j, slot):
        start = pl.multiple_of(j * t, t)
        for h in heads:
            s_ref[slot * nh + h] = _dot_nt(q_ref[0, h], k_ref[0, h, pl.ds(start, t), :])

    def softmax(slot, diagonal):
        for h in heads:
            for r0 in range(0, t, FLASH_ROWS):
                rows = slice(r0, r0 + FLASH_ROWS)
                sc = s_ref[slot * nh + h, rows, :]
                if diagonal:
                    row = lax.broadcasted_iota(jnp.int32, sc.shape, 0) + r0
                    col = lax.broadcasted_iota(jnp.int32, sc.shape, 1)
                    sc = jnp.where(col <= row, sc, MASK_VALUE)
                m_old = m_ref[h, rows, :]
                m_new = jnp.maximum(m_old, jnp.max(sc, axis=-1, keepdims=True))
                p_ref[slot * nh + h, rows, :] = jnp.exp2(sc - m_new[:, :1]).astype(BF16)
                a_ref[slot * nh + h, rows, :] = jnp.exp2(m_old - m_new)
                m_ref[h, rows, :] = m_new

    def accumulate(j, slot):
        start = pl.multiple_of(j * t, t)
        for h in heads:
            pv = _dot(p_ref[slot * nh + h], v_ref[0, h, pl.ds(start, t), :])
            alpha = a_ref[slot * nh + h]
            acc_ref[h] = jnp.concatenate([alpha, alpha], axis=1) * acc_ref[h] + pv

    def stage(j, slot, with_accumulate=True):
        softmax(slot, False)
        scores(j + 1, 1 - slot)
        if with_accumulate:
            accumulate(jnp.maximum(j - 1, 0), 1 - slot)

    odd = lax.rem(qi, 2)
    for slot in range(2):
        @pl.when(odd == slot)
        def _():
            scores(0, slot)

    @pl.when(odd == 1)
    def _():
        stage(0, 1, with_accumulate=False)

    def body(jj, carry):
        j = 2 * jj + odd
        stage(j, 0)
        stage(j + 1, 1)
        return carry

    lax.fori_loop(0, qi // 2, body, 0)
    softmax(0, True)
    accumulate(jnp.maximum(qi - 1, 0), 1)
    accumulate(qi, 0)

    for h in heads:
        acc = acc_ref[h]
        o_ref[0, :, h * MLA_V:(h + 1) * MLA_V] = (acc[:, :MLA_V] / acc[:, MLA_V:]).astype(o_ref.dtype)


def _flash(q, k, v):
    nb, nh, s, _ = q.shape
    hb = FLASH_HEADS
    return pl.pallas_call(
        _flash_kernel,
        grid=(nb, nh // hb, s // FLASH_T),
        in_specs=[
            pl.BlockSpec((1, hb, FLASH_T, MLA_QK), lambda b, h, i: (b, h, i, 0)),
            pl.BlockSpec((1, hb, s, MLA_QK), lambda b, h, i: (b, h, 0, 0)),
            pl.BlockSpec((1, hb, s, 2 * MLA_V), lambda b, h, i: (b, h, 0, 0)),
        ],
        out_specs=pl.BlockSpec((1, FLASH_T, hb * MLA_V), lambda b, h, i: (b, i, h)),
        out_shape=jax.ShapeDtypeStruct((nb, s, nh * MLA_V), BF16),
        scratch_shapes=[pltpu.VMEM((2 * hb, FLASH_T, FLASH_T), F32), pltpu.VMEM((2 * hb, FLASH_T, FLASH_T), BF16),
                        pltpu.VMEM((hb, FLASH_T, LANES), F32), pltpu.VMEM((2 * hb, FLASH_T, LANES), F32),
                        pltpu.VMEM((hb, FLASH_T, 2 * MLA_V), F32)],
        compiler_params=_params("parallel", "parallel", "arbitrary"),
        name="flash",
    )(q, k, v)


def _cumsum_rows(x, ltri):
    hi = x.astype(BF16)
    r1 = x - hi.astype(F32)
    mid = r1.astype(BF16)
    lo = (r1 - mid.astype(F32)).astype(BF16)
    return _dot(ltri, hi) + _dot(ltri, mid) + _dot(ltri, lo)


def _hgrn_kernel(q_ref, f_ref, i_ref, g_ref, lb_ref, on_ref, ones_ref, ltri_ref, o_ref, st_ref):
    @pl.when(pl.program_id(1) == 0)
    def _():
        st_ref[...] = jnp.zeros_like(st_ref)

    for n in range(q_ref.shape[0]):
        _hgrn_chunk(n, q_ref, f_ref, i_ref, g_ref, lb_ref, on_ref, ones_ref, ltri_ref, o_ref, st_ref)


def _hgrn_chunk(n, q_ref, f_ref, i_ref, g_ref, lb_ref, on_ref, ones_ref, ltri_ref, o_ref, st_ref):
    c, w = HG_CHUNK, HG_WIDTH
    z = f_ref[n]
    lb = lb_ref[...]
    e = jnp.exp(-jnp.abs(z))
    r = 1.0 / (1.0 + e)
    pos = z >= 0.0
    sig_z = jnp.where(pos, r, e * r)
    sig_mz = jnp.where(pos, e * r, r)
    kk = (1.0 - lb) * sig_mz
    q = _silu(q_ref[n])
    v = i_ref[n]
    gate = _silu(g_ref[n])
    f = lb + (1.0 - lb) * sig_z
    b = _cumsum_rows(jnp.log2(f), ltri_ref[...])
    row = lax.broadcasted_iota(jnp.int32, (c, w), 0)

    q_lvls, k_lvls = [], []
    for m in HG_LEVELS:
        half = m // 2
        b4 = b.reshape(c // m, 2, half, w)
        ref_b = b4[:, 0, half - 1:half, :]
        k_first = kk.reshape(c // m, 2, half, w)[:, 0] * jnp.exp2(ref_b - b4[:, 0])
        q_second = q.reshape(c // m, 2, half, w)[:, 1] * jnp.exp2(b4[:, 1] - ref_b)
        zero = jnp.zeros_like(k_first)
        q_lvls.append(jnp.stack([zero, q_second], axis=1).reshape(c, w).astype(BF16))
        k_lvls.append(jnp.stack([k_first, zero], axis=1).reshape(c, w).astype(BF16))

    pos8 = row & (SUBLANES - 1)

    def back1(x):
        x3 = x.reshape(c // SUBLANES, SUBLANES, w)
        return pltpu.roll(x3, 1, 1).reshape(c, w)

    f_in = jnp.where(pos8 >= 1, f, 0.0)
    kd_dl, vr = kk, v
    o_diag = jnp.zeros((c, w), F32)
    for dl in range(SUBLANES):
        if dl > 0:
            kd_dl = f_in * back1(kd_dl)
            vr = back1(vr)
        wb = (q * kd_dl).astype(BF16)
        a = jnp.concatenate([_dot(wb[:, :MXU_DIM], ones_ref[...]), _dot(wb[:, MXU_DIM:], ones_ref[...])], axis=1)
        o_diag = o_diag + a * vr

    tt = lax.broadcasted_iota(jnp.int32, (c, c), 0)
    ss = lax.broadcasted_iota(jnp.int32, (c, c), 1)
    same_blk = {m: (tt // m) == (ss // m) for m in HG_LEVELS}

    b_last = b[c - 1:c, :]
    qe = (q * jnp.exp2(b)).astype(BF16)
    kd = (kk * jnp.exp2(b_last - b)).astype(BF16)
    decay = jnp.exp2(b_last)
    vb = v.astype(BF16)
    for h in range(HG_HEADS):
        hs = slice(h * HG_DK, (h + 1) * HG_DK)
        st = st_ref[n, h]
        attn = jnp.zeros((c, c), F32)
        for m, ql, kl in zip(HG_LEVELS, q_lvls, k_lvls):
            part = _dot_nt(ql[:, hs], kl[:, hs])
            attn = attn + (part if m == c else jnp.where(same_blk[m], part, 0.0))
        o_h =_dot_nt(qe[:, hs], st.astype(BF16)) + _dot(attn.astype(BF16), vb[:, hs]) + o_diag[:, hs]
        st_ref[n, h] = st * decay[:, hs] + _dot_tn(vb[:, hs], kd[:, hs])
        o_ref[n, :, hs] = (_rms(o_h, on_ref[...]) * gate[:, hs]).astype(o_ref.dtype)


def _hgrn(proj, lb, onorm):
    nb, s, _ = proj.shape
    head_id = jnp.arange(MXU_DIM) // HG_DK
    ones_blk = (head_id[:, None] == head_id[None, :]).astype(BF16)
    t_id = jnp.arange(HG_CHUNK)
    ltri = (t_id[:, None] >= t_id[None, :]).astype(BF16)
    ns = HG_SEQS if nb % HG_SEQS == 0 else 1
    spec = lambda cb: pl.BlockSpec((ns, HG_CHUNK, COL_BLK), lambda b, i: (b, i, cb))
    return pl.pallas_call(
        _hgrn_kernel,
        grid=(nb // ns, s // HG_CHUNK),
        in_specs=[spec(3), spec(4), spec(5), spec(6),
                  pl.BlockSpec(lb.shape, lambda b, i: (0, 0)),
                  pl.BlockSpec(onorm.shape, lambda b, i: (0, 0)),
                  pl.BlockSpec(ones_blk.shape, lambda b, i: (0, 0)),
                  pl.BlockSpec(ltri.shape, lambda b, i: (0, 0))],
        out_specs=pl.BlockSpec((ns, HG_CHUNK, HG_WIDTH), lambda b, i: (b, i, 0)),
        out_shape=jax.ShapeDtypeStruct((nb, s, HG_WIDTH), BF16),
        scratch_shapes=[pltpu.VMEM((ns, HG_HEADS, HG_DV, HG_DK), F32)],
        compiler_params=_params("parallel", "arbitrary"),
        name="hgrn",
    )(proj, proj, proj, proj, lb, onorm, ones_blk, ltri)


def _out_proj_kernel(ys_ref, ym_ref, yh_ref, w_ref, x_ref, gain_ref, g_ref, o_ref, slab_ref):
    for t in range(S5_LS):
        for c in range(S5_WIDTH // LANES):
            col = t * S5_WIDTH + c * LANES
            slab_ref[c, pl.ds(t, OUT_TM // S5_LS, stride=S5_LS), :] = ys_ref[0, :, col:col + LANES]
    scale = gain_ref[...] * g_ref[0]
    part = OUT_TM // 2
    for p0 in range(0, OUT_TM, part):
        rows = slice(p0, p0 + part)
        ys = jnp.concatenate([slab_ref[c, rows, :] for c in range(S5_WIDTH // LANES)], axis=1).astype(BF16)
        mixed = (_dot(ys, w_ref[:S5_WIDTH, :])
                 + _dot(ym_ref[0, rows, :], w_ref[S5_WIDTH:S5_WIDTH + MLA_WIDTH, :])
                 + _dot(yh_ref[0, rows, :], w_ref[S5_WIDTH + MLA_WIDTH:, :]))
        inv = lax.rsqrt(jnp.mean(mixed * mixed, axis=-1, keepdims=True) + EPS)
        o_ref[0, rows, :] = x_ref[0, rows, :] + mixed * inv * scale


def _out_proj(y_s5_rows, y_mla, y_hg, w, l, x, gain, g):
    nb, s, d = x.shape
    row = lambda n: pl.BlockSpec((1, OUT_TM, n), lambda b, i: (b, i, 0))
    return pl.pallas_call(
        _out_proj_kernel,
        grid=(nb, s // OUT_TM),
        in_specs=[pl.BlockSpec((1, OUT_TM // S5_LS, S5_LS * S5_WIDTH), lambda b, i: (b, i, 0)),
                  row(MLA_WIDTH), row(HG_WIDTH),
                  pl.BlockSpec((None,) + w.shape[1:], lambda b, i: (l, 0, 0), pipeline_mode=pl.Buffered(1)),
                  row(d),
                  pl.BlockSpec((1, d), lambda b, i: (0, 0)),
                  pl.BlockSpec((1, 1, d), lambda b, i: (b, 0, 0))],
        out_specs=row(d),
        out_shape=jax.ShapeDtypeStruct((nb, s, d), F32),
        scratch_shapes=[pltpu.VMEM((S5_WIDTH // LANES, OUT_TM, LANES), F32)],
        compiler_params=_params("parallel", "parallel"),
        name="out_proj",
    )(y_s5_rows, y_mla, y_hg, w, x, gain, g)


def _ffn_kernel(x_ref, sc_ref, sh_ref, g_ref, pre_ref, post_ref, wu_ref, cp_ref, wd_ref, o_ref,
                h_ref, ug_ref, uv_ref, act_ref, tail_ref):
    i = pl.program_id(1)
    j = pl.program_id(2)

    @pl.when(j == 0)
    def _():
        scale = pre_ref[...] * (1.0 + sc_ref[0])
        shift = sh_ref[0]
        for r0 in range(0, FFN_TM, NORM_ROWS):
            x = x_ref[0, r0:r0 + NORM_ROWS, :]
            inv = lax.rsqrt(jnp.mean(x * x, axis=-1, keepdims=True) + EPS)
            h_ref[r0:r0 + NORM_ROWS, :] = (x * inv * scale + shift).astype(BF16)
        o_ref[...] = jnp.zeros_like(o_ref)

    cw = FFN_TF // FFN_SPLITS
    for t, u_ref in enumerate((ug_ref, uv_ref)):
        u_ref[:HALO, :] = jnp.where(i == 0, 0.0, tail_ref[j, t])
    for q in range(FFN_SPLITS):
        cols = slice(q * cw, (q + 1) * cw)
        u = _dot(h_ref[...], wu_ref[:, 2 * q * cw:2 * (q + 1) * cw])
        for t, u_ref in enumerate((ug_ref, uv_ref)):
            u_ref[HALO:, cols] = u[:, t * cw:(t + 1) * cw]
            tail_ref[j, t, :, cols] = u[FFN_TM - HALO:, t * cw:(t + 1) * cw]

    def conv(u_ref, p0, r0, cols):
        return (cp_ref[p0 + CONV_W:p0 + CONV_W + 1, cols]
                + u_ref[r0 + HALO - 2:r0 + HALO - 2 + FFN_ROWS, cols] * cp_ref[p0:p0 + 1, cols]
                + u_ref[r0 + HALO - 1:r0 + HALO - 1 + FFN_ROWS, cols] * cp_ref[p0 + 1:p0 + 2, cols]
                + u_ref[r0 + HALO:r0 + HALO + FFN_ROWS, cols] * cp_ref[p0 + 2:p0 + 3, cols])

    for q in range(FFN_SPLITS):
        cols = slice(q * cw, (q + 1) * cw)
        for r0 in range(0, FFN_TM, FFN_ROWS):
            gate = conv(ug_ref, 0, r0, cols)
            val = conv(uv_ref, CONV_W + 1, r0, cols)
            act_ref[r0:r0 + FFN_ROWS, cols] = (_gelu_tanh(gate) * val).astype(BF16)
        o_ref[0] += _dot(act_ref[:, cols], wd_ref[cols, :])

    @pl.when(j == pl.num_programs(2) - 1)
    def _():
        scale = post_ref[...] * g_ref[0]
        for r0 in range(0, FFN_TM, NORM_ROWS):
            y = o_ref[0, r0:r0 + NORM_ROWS, :]
            inv = lax.rsqrt(jnp.mean(y * y, axis=-1, keepdims=True) + EPS)
            o_ref[0, r0:r0 + NORM_ROWS, :] = x_ref[0, r0:r0 + NORM_ROWS, :] + y * inv * scale


def _ffn(x, sc, sh, g, pre, post, w_up, conv_p, w_down, l):
    nb, s, d = x.shape
    ff = w_down.shape[1]
    nf = ff // FFN_TF
    vec = pl.BlockSpec((1, 1, d), lambda b, i, j: (b, 0, 0))
    par = pl.BlockSpec((1, d), lambda b, i, j: (0, 0))
    return pl.pallas_call(
        _ffn_kernel,
        grid=(nb, s // FFN_TM, nf),
        in_specs=[
            pl.BlockSpec((1, FFN_TM, d), lambda b, i, j: (b, i, 0)),
            vec, vec, vec, par, par,
            pl.BlockSpec((None, d, 2 * FFN_TF), lambda b, i, j: (l, 0, j)),
            pl.BlockSpec((None, None, 2 * (CONV_W + 1), FFN_TF), lambda b, i, j: (l, j, 0, 0)),
            pl.BlockSpec((None, FFN_TF, d), lambda b, i, j: (l, j, 0)),
        ],
        out_specs=pl.BlockSpec((1, FFN_TM, d), lambda b, i, j: (b, i, 0), pipeline_mode=pl.Buffered(1)),
        out_shape=jax.ShapeDtypeStruct((nb, s, d), F32),
        scratch_shapes=[pltpu.VMEM((FFN_TM, d), BF16),
                        pltpu.VMEM((FFN_TM + HALO, FFN_TF), F32), pltpu.VMEM((FFN_TM + HALO, FFN_TF), F32),
                        pltpu.VMEM((FFN_TM, FFN_TF), BF16), pltpu.VMEM((nf, 2, HALO, FFN_TF), F32)],
        compiler_params=_params("parallel", "arbitrary", "arbitrary"),
        name="ffn",
    )(x, sc, sh, g, pre, post, w_up, conv_p, w_down)


def _pad_cols(a, n):
    return jnp.pad(a, [(0, 0)] * (a.ndim - 1) + [(0, n - a.shape[-1])])


def _cast_w_up_kernel(w_ref, o_ref):
    cw = FFN_TF // FFN_SPLITS
    for h in range(2):
        @pl.when(pl.program_id(2) == h)
        def _():
            for p in range(D_FF_PAD // cw):
                width = max(0, min(cw, D_FF - p * cw))
                dst = (2 * p + h) * cw
                if width > 0:
                    o_ref[0, :, dst:dst + width] = w_ref[0, :, p * cw:p * cw + width].astype(BF16)
                if width < cw:
                    o_ref[0, :, dst + width:dst + cw] = jnp.zeros((o_ref.shape[1], cw - width), BF16)


def _cast_w_up(w_up):
    nl, d, _ = w_up.shape
    return pl.pallas_call(
        _cast_w_up_kernel,
        grid=(nl, d // CAST_TM, 2),
        in_specs=[pl.BlockSpec((1, CAST_TM, D_FF), lambda l, i, h: (l, i, h))],
        out_specs=pl.BlockSpec((1, CAST_TM, 2 * D_FF_PAD), lambda l, i, h: (l, i, 0)),
        out_shape=jax.ShapeDtypeStruct((nl, d, 2 * D_FF_PAD), BF16),
        compiler_params=_params("parallel", "parallel", "arbitrary"),
        name="cast_w_up",
    )(w_up)


def _cast_w_down_kernel(w_ref, o_ref):
    o_ref[0, :D_FF, :] = w_ref[0].astype(BF16)
    o_ref[0, D_FF:, :] = jnp.zeros((D_FF_PAD - D_FF, o_ref.shape[2]), BF16)


def _cast_w_down(w_down):
    nl, ff, d = w_down.shape
    return pl.pallas_call(
        _cast_w_down_kernel,
        grid=(nl, d // MXU_DIM),
        in_specs=[pl.BlockSpec((1, ff, MXU_DIM), lambda l, j: (l, 0, j))],
        out_specs=pl.BlockSpec((1, D_FF_PAD, MXU_DIM), lambda l, j: (l, 0, j)),
        out_shape=jax.ShapeDtypeStruct((nl, D_FF_PAD, d), BF16),
        compiler_params=_params("parallel", "parallel"),
        name="cast_w_down",
    )(w_down)


W_IN_HEAD = S5_WIDTH + MLA_Q_RANK + MLA_KV_RANK + MLA_ROPE


def _cast_w_in_kernel(w_ref, o_ref):
    w = w_ref[0]
    gap = jnp.zeros((w.shape[0], 3 * COL_BLK - W_IN_HEAD), w.dtype)
    o_ref[0] = jnp.concatenate([w[:, :W_IN_HEAD], gap, w[:, W_IN_HEAD:]], axis=1).astype(BF16)


def _layout_w_in(w_in):
    nl, d, n = w_in.shape
    return pl.pallas_call(
        _cast_w_in_kernel,
        grid=(nl, d // CAST_TM),
        in_specs=[pl.BlockSpec((1, CAST_TM, n), lambda l, i: (l, i, 0))],
        out_specs=pl.BlockSpec((1, CAST_TM, D_IN_PAD), lambda l, i: (l, i, 0)),
        out_shape=jax.ShapeDtypeStruct((nl, d, D_IN_PAD), BF16),
        compiler_params=_params("parallel", "parallel"),
        name="cast_w_in",
    )(w_in)


def _layout_w_uq(w):
    nl, r, _ = w.shape
    w = w.reshape(nl, r, MLA_HEADS, MLA_NOPE + MLA_ROPE)
    return _pad_cols(w, MLA_QK).reshape(nl, r, MLA_HEADS * MLA_QK).astype(BF16)


def _layout_w_ukv(w):
    nl, r, _ = w.shape
    w = w.reshape(nl, r, MLA_HEADS, MLA_NOPE + MLA_V)
    k = w[..., :MLA_NOPE].reshape(nl, r, MLA_HEADS * MLA_NOPE)
    v = w[..., MLA_NOPE:].reshape(nl, r, MLA_HEADS * MLA_V)
    return jnp.concatenate([k, v], axis=-1).astype(BF16)


def _rope_tables(positions):
    inv_freq = 1.0 / (ROPE_THETA ** (jnp.arange(0, MLA_ROPE, 2, dtype=F32) / MLA_ROPE))
    ang = positions.astype(F32)[..., None] * inv_freq
    cos, sin = jnp.cos(ang), jnp.sin(ang)
    z32 = jnp.zeros_like(cos)
    z64 = jnp.concatenate([z32, z32], axis=-1)
    return (jnp.concatenate([cos, cos, z64], axis=-1),
            jnp.concatenate([z32, sin, z64], axis=-1),
            jnp.concatenate([-sin, z32, z64], axis=-1))


def kernel(x, c, positions, w_in, s5_lambda_re, s5_lambda_im, s5_log_dt, s5_b_re, s5_b_im, s5_c_re, s5_c_im, s5_d, s5_w_glu, mla_q_norm, mla_w_uq, mla_kv_norm, mla_w_ukv, hg_lb_logits, hg_out_norm, w_out, mix_pre_norm, mix_post_norm, ffn_pre_norm, ffn_post_norm, ffn_w_up, ffn_conv_w, ffn_conv_b, ffn_w_down, w_ada, b_ada):
    nb, s, d = x.shape
    nl = w_in.shape[0]
    assert s % FLASH_T == 0 and s % FFN_TM == 0 and s % (S5_LS * SUBLANES) == 0

    mod = _adaln(c, w_ada, b_ada)
    cos_t, s1_t, s2_t = _rope_tables(positions)
    probs = jax.nn.softmax(hg_lb_logits.astype(F32), axis=0)
    lower_bounds = jnp.cumsum(probs, axis=0) - probs[0:1]

    w_in_p = _layout_w_in(w_in)
    w_uq_p = _layout_w_uq(mla_w_uq)
    w_ukv_p = _layout_w_ukv(mla_w_ukv)
    w_out_b = w_out.astype(BF16)
    w_glu_b = s5_w_glu.astype(BF16)
    w_up_p = _cast_w_up(ffn_w_up)
    conv_rows = jnp.concatenate([ffn_conv_w[..., :D_FF], ffn_conv_b[:, None, :D_FF],
                                 ffn_conv_w[..., D_FF:], ffn_conv_b[:, None, D_FF:]], axis=1)
    conv_p = _pad_cols(conv_rows, D_FF_PAD).reshape(nl, 2 * (CONV_W + 1), -1, FFN_TF).transpose(0, 2, 1, 3)
    w_down = _cast_w_down(ffn_w_down)

    for l in range(nl):
        sh1, sc1, g1, sh2, sc2, g2 = [m[:, None, :] for m in jnp.split(mod[l], 6, axis=-1)]
        proj, u_rows = _proj_in(x, sc1, sh1, mix_pre_norm[l][None, :], w_in_p, l)
        bm, cm, lam, lam_chunk = _s5_prep(s5_lambda_re[l], s5_lambda_im[l], s5_log_dt[l],
                                          s5_b_re[l], s5_b_im[l], s5_c_re[l], s5_c_im[l])
        y_s5 = _s5(u_rows, bm, cm, lam, lam_chunk, s5_d[l][None, :], w_glu_b[l])
        q, k, v = _mla_prep(proj, mla_q_norm[l][None, :], mla_kv_norm[l][None, :], w_uq_p[l], w_ukv_p[l],
                            cos_t, s1_t, s2_t)
        y_mla = _flash(q, k, v)
        y_hg = _hgrn(proj, lower_bounds[l][None, :], hg_out_norm[l][None, :])
        x = _out_proj(y_s5, y_mla, y_hg, w_out_b, l, x, mix_post_norm[l][None, :], g1)
        x = _ffn(x, sc2, sh2, g2, ffn_pre_norm[l][None, :], ffn_post_norm[l][None, :],
                 w_up_p, conv_p, w_down, l)
    return x
```

```python
import functools
import math

import jax
import jax.numpy as jnp
from jax import lax
from jax.experimental import pallas as pl
from jax.experimental.pallas import tpu as pltpu

F32 = jnp.float32
BF16 = jnp.bfloat16

D_MODEL = 2048
DEPTH = 4
S5_WIDTH = 512
S5_GROUP = 16
S5_GROUPS = 32
S5_STATE = 64
MLA_HEADS = 8
MLA_NOPE = 128
MLA_ROPE = 64
MLA_V = 128
MLA_Q_RANK = 512
MLA_KV_RANK = 256
MLA_WIDTH = MLA_HEADS * MLA_V
ROPE_THETA = 10000.0
MASK_VALUE = -1e30
HG_HEADS = 4
HG_DK = 128
HG_DV = 128
HG_WIDTH = HG_HEADS * HG_DV
D_FF = 5504
CONV_W = 3
EPS = 1e-6

LANES = 128
SUBLANES = 8
MXU_DIM = 256
VMEM_LIMIT_BYTES = 56 * 1024 * 1024

COL_BLK = 512
N_COL_BLKS = 7
D_IN_PAD = N_COL_BLKS * COL_BLK
MLA_QK = 2 * LANES
D_FF_PAD = 5632

ADA_TN = 1024
PROJ_TM = 512
PROJ_PARTS = 1
S5_LS = 16
S5_SEQS = 2
S5_HALF = S5_WIDTH // 2
S5_HSTATE = (S5_GROUPS // 2) * S5_STATE
PREP_TM = 1024
FLASH_T = 512
FLASH_HEADS = 2
FLASH_ROWS = 32
HG_CHUNK = 128
HG_LEVELS = (16, 32, 64, 128)
HG_SEQS = 4
OUT_TM = 1024
FFN_TM = 1024
FFN_TF = 512
FFN_ROWS = 64
FFN_SPLITS = 1
NORM_ROWS = 16
CAST_TM = 256
HALO = SUBLANES


def _sigmoid(x):
    return 1.0 / (1.0 + jnp.exp(-x))


def _silu(x):
    return x * _sigmoid(x)


def _gelu_tanh(x):
    c = math.sqrt(2.0 / math.pi)
    return x * (0.5 * (1.0 + jnp.tanh(c * (x + 0.044715 * (x * x * x)))))


def _rms(x, gain):
    return x * lax.rsqrt(jnp.mean(x * x, axis=-1, keepdims=True) + EPS) * gain


def _dot(a, b):
    return jnp.dot(a, b, preferred_element_type=F32)


def _dot_nt(a, b):
    return lax.dot_general(a, b, (((1,), (1,)), ((), ())), preferred_element_type=F32)


def _dot_tn(a, b):
    return lax.dot_general(a, b, (((0,), (0,)), ((), ())), preferred_element_type=F32)


def _params(*sem):
    return pltpu.CompilerParams(dimension_semantics=sem, vmem_limit_bytes=VMEM_LIMIT_BYTES)


def _adaln_kernel(c_ref, w_ref, b_ref, o_ref):
    ca = _silu(c_ref[...]).astype(BF16)
    o_ref[0] = _dot(ca, w_ref[0].astype(BF16)) + b_ref[0]


def _adaln(c, w_ada, b_ada):
    nb, d = c.shape
    nl, _, n = w_ada.shape
    rows = -(-nb // SUBLANES) * SUBLANES
    c_pad = jnp.pad(c, ((0, rows - nb), (0, 0)))
    out = pl.pallas_call(
        _adaln_kernel,
        grid=(nl, n // ADA_TN),
        in_specs=[
            pl.BlockSpec((rows, d), lambda l, j: (0, 0)),
            pl.BlockSpec((1, d, ADA_TN), lambda l, j: (l, 0, j)),
            pl.BlockSpec((1, 1, ADA_TN), lambda l, j: (l, 0, j)),
        ],
        out_specs=pl.BlockSpec((1, rows, ADA_TN), lambda l, j: (l, 0, j)),
        out_shape=jax.ShapeDtypeStruct((nl, rows, n), F32),
        compiler_params=_params("parallel", "parallel"),
        name="adaln",
    )(c_pad, w_ada, b_ada.reshape(nl, 1, n))
    return out[:, :nb]


def _proj_in_kernel(x_ref, sc_ref, sh_ref, gain_ref, w_ref, o_ref, u_ref, h_ref, slab_ref):
    scale = gain_ref[...] * (1.0 + sc_ref[0])
    shift = sh_ref[0]
    part = PROJ_TM // PROJ_PARTS
    for p0 in range(0, PROJ_TM, part):
        for r0 in range(p0, p0 + part, NORM_ROWS):
            x = x_ref[0, r0:r0 + NORM_ROWS, :]
            inv = lax.rsqrt(jnp.mean(x * x, axis=-1, keepdims=True) + EPS)
            h_ref[r0:r0 + NORM_ROWS, :] = (x * inv * scale + shift).astype(BF16)
        res = _dot(h_ref[p0:p0 + part, :], w_ref[...])
        o_ref[0, p0:p0 + part, :] = res
        for c in range(S5_WIDTH // LANES):
            slab_ref[c, p0:p0 + part, :] = res[:, c * LANES:(c + 1) * LANES]

    for t in range(S5_LS):
        for c in range(S5_WIDTH // LANES):
            col = t * S5_WIDTH + c * LANES
            u_ref[0, :, col:col + LANES] = slab_ref[c, pl.ds(t, PROJ_TM // S5_LS, stride=S5_LS), :]


def _proj_in(x, sc, sh, gain, w, l):
    nb, s, d = x.shape
    n = w.shape[2]
    rows = PROJ_TM // S5_LS
    return pl.pallas_call(
        _proj_in_kernel,
        grid=(nb, s // PROJ_TM),
        in_specs=[
            pl.BlockSpec((1, PROJ_TM, d), lambda b, i: (b, i, 0)),
            pl.BlockSpec((1, 1, d), lambda b, i: (b, 0, 0)),
            pl.BlockSpec((1, 1, d), lambda b, i: (b, 0, 0)),
            pl.BlockSpec((1, d), lambda b, i: (0, 0)),
            pl.BlockSpec((None, d, n), lambda b, i: (l, 0, 0), pipeline_mode=pl.Buffered(1)),
        ],
        out_specs=[pl.BlockSpec((1, PROJ_TM, n), lambda b, i: (b, i, 0)),
                   pl.BlockSpec((1, rows, S5_LS * S5_WIDTH), lambda b, i: (b, i, 0))],
        out_shape=[jax.ShapeDtypeStruct((nb, s, n), F32),
                   jax.ShapeDtypeStruct((nb, s // S5_LS, S5_LS * S5_WIDTH), F32)],
        scratch_shapes=[pltpu.VMEM((PROJ_TM, d), BF16), pltpu.VMEM((S5_WIDTH // LANES, PROJ_TM, LANES), F32)],
        compiler_params=_params("parallel", "parallel"),
        name="proj_in",
    )(x, sc, sh, gain, w)


def _s5_kernel(u_ref, bm_ref, cm_ref, lam_ref, lam_chunk_ref, d_ref, wglu_ref, o_ref,
               sre, sim, hre, him, yloc):
    ph = pl.program_id(1)
    s = pl.program_id(2)
    nseq, kc, _ = u_ref.shape
    rows = nseq * kc

    @pl.when(ph == 0)
    def _():
        @pl.when(s == 0)
        def _():
            sre[...] = jnp.zeros_like(sre)
            sim[...] = jnp.zeros_like(sim)

        u = u_ref[...].reshape(rows, S5_WIDTH)
        ub = u.astype(BF16)
        for h in range(2):
            cols = slice(h * S5_HALF, (h + 1) * S5_HALF)
            v = _dot(ub[:, cols], bm_ref[h])
            lr = lam_ref[h:h + 1, :]
            li = lam_ref[2 + h:3 + h, :]
            re = sre[h]
            im = sim[h]
            nre = lr * re - li * im + v[:, :S5_HSTATE]
            nim = lr * im + li * re + v[:, S5_HSTATE:]
            sre[h] = nre
            sim[h] = nim
            yl = _dot(nre.astype(BF16), cm_ref[h, 0]) + _dot(nim.astype(BF16), cm_ref[h, 1])
            yloc[s, :, cols] = yl + d_ref[:, cols] * u[:, cols]

        @pl.when(s == S5_LS - 1)
        def _():
            def body(k, carry):
                new = []
                for n in range(nseq):
                    row = pl.ds(n * kc + k, 1)
                    for h in range(2):
                        cre, cim = carry[4 * n + 2 * h], carry[4 * n + 2 * h + 1]
                        hre[h, row, :] = cre
                        him[h, row, :] = cim
                        lr = lam_chunk_ref[h:h + 1, :]
                        li = lam_chunk_ref[2 + h:3 + h, :]
                        new.append(lr * cre - li * cim + sre[h, row, :])
                        new.append(lr * cim + li * cre + sim[h, row, :])
                return tuple(new)

            zero = jnp.zeros((1, S5_HSTATE), F32)
            lax.fori_loop(0, kc, body, (zero,) * (4 * nseq))

    @pl.when(ph == 1)
    def _():
        ys = []
        for h in range(2):
            cols = slice(h * S5_HALF, (h + 1) * S5_HALF)
            lr = lam_ref[h:h + 1, :]
            li = lam_ref[2 + h:3 + h, :]
            pre = hre[h]
            pim = him[h]
            nre = lr * pre - li * pim
            nim = lr * pim + li * pre
            hre[h] = nre
            him[h] = nim
            yc = _dot(nre.astype(BF16), cm_ref[h, 0]) + _dot(nim.astype(BF16), cm_ref[h, 1])
            ys.append(yloc[s, :, cols] + yc)
        g = _gelu_tanh(jnp.concatenate(ys, axis=1))
        gate = _sigmoid(_dot(g.astype(BF16), wglu_ref[...]))
        o_ref[...] = (g * gate).reshape(o_ref.shape).astype(o_ref.dtype)


def _s5(u_rows, bm, cm, lam, lam_chunk, d, wglu):
    nb, kc, _ = u_rows.shape
    ns = S5_SEQS if nb % S5_SEQS == 0 else 1
    rows = ns * kc
    return pl.pallas_call(
        _s5_kernel,
        grid=(nb // ns, 2, S5_LS),
        in_specs=[
            pl.BlockSpec((ns, kc, S5_WIDTH),
                         lambda b, p, t: (b, 0, jnp.where(p == 0, t, S5_LS - 1))),
            pl.BlockSpec(bm.shape, lambda b, p, t: (0, 0, 0)),
            pl.BlockSpec(cm.shape, lambda b, p, t: (0, 0, 0, 0)),
            pl.BlockSpec(lam.shape, lambda b, p, t: (0, 0)),
            pl.BlockSpec(lam_chunk.shape, lambda b, p, t: (0, 0)),
            pl.BlockSpec(d.shape, lambda b, p, t: (0, 0)),
            pl.BlockSpec(wglu.shape, lambda b, p, t: (0, 0)),
        ],
        out_specs=pl.BlockSpec((ns, kc, S5_WIDTH), lambda b, p, t: (b, 0, t * p)),
        out_shape=jax.ShapeDtypeStruct((nb, kc, S5_LS * S5_WIDTH), F32),
        scratch_shapes=[pltpu.VMEM((2, rows, S5_HSTATE), F32)] * 4
        + [pltpu.VMEM((S5_LS, rows, S5_WIDTH), F32)],
        compiler_params=_params("parallel", "arbitrary", "arbitrary"),
        name="s5",
    )(u_rows, bm, cm, lam, lam_chunk, d, wglu)


def _s5_prep(lam_re, lam_im, log_dt, b_re, b_im, c_re, c_im):
    lam_re, lam_im = lam_re.astype(F32), lam_im.astype(F32)
    dt = jnp.exp(log_dt.astype(F32))[:, None]
    mag = jnp.exp(lam_re * dt)
    bar_re, bar_im = mag * jnp.cos(lam_im * dt), mag * jnp.sin(lam_im * dt)
    den = lam_re * lam_re + lam_im * lam_im
    co_re = ((bar_re - 1.0) * lam_re + bar_im * lam_im) / den
    co_im = (bar_im * lam_re - (bar_re - 1.0) * lam_im) / den
    b_re, b_im = b_re.astype(F32), b_im.astype(F32)
    bb_re = co_re[..., None] * b_re - co_im[..., None] * b_im
    bb_im = co_re[..., None] * b_im + co_im[..., None] * b_re
    ch_re, ch_im = bar_re, bar_im
    for _ in range(int(math.log2(S5_LS))):
        ch_re, ch_im = ch_re * ch_re - ch_im * ch_im, 2.0 * ch_re * ch_im
    gh = S5_GROUPS // 2
    eye = jnp.eye(gh, dtype=F32)

    def b_tiles(part):
        t = part.reshape(2, gh, S5_STATE, S5_GROUP)
        return jnp.einsum('ab,hapc->hacbp', eye, t).reshape(2, S5_HALF, S5_HSTATE)

    def c_tiles(part):
        t = part.reshape(2, gh, S5_GROUP, S5_STATE)
        return jnp.einsum('ab,hbcp->hapbc', eye, t).reshape(2, S5_HSTATE, S5_HALF)

    bm = jnp.concatenate([b_tiles(bb_re), b_tiles(bb_im)], axis=-1).astype(BF16)
    cm = jnp.stack([c_tiles(c_re.astype(F32)), -c_tiles(c_im.astype(F32))], axis=1).astype(BF16)

    def rows(re, im):
        return jnp.concatenate([re.reshape(2, S5_HSTATE), im.reshape(2, S5_HSTATE)], 0)

    return bm, cm, rows(bar_re, bar_im), rows(ch_re, ch_im)


def _mla_prep_kernel(cq_ref, ckv_ref, qn_ref, kvn_ref, wq_ref, wkv_ref, cos_ref, s1_ref, s2_ref,
                     q_ref, k_ref, v_ref):
    cos_t = cos_ref[0]
    s1 = s1_ref[0]
    s2 = s2_ref[0]
    half = MLA_ROPE // 2

    def rope(x):
        return x * cos_t + pltpu.roll(x, half, 1) * s1 + pltpu.roll(x, LANES - half, 1) * s2

    scale = (MLA_NOPE + MLA_ROPE) ** -0.5 * math.log2(math.e)
    q = _dot(_rms(cq_ref[0], qn_ref[...]).astype(BF16), wq_ref[...])
    blk = ckv_ref[0]
    kv = _dot(_rms(blk[:, :MLA_KV_RANK], kvn_ref[...]).astype(BF16), wkv_ref[...])
    kr = rope(blk[:, MLA_KV_RANK:MLA_KV_RANK + LANES]).astype(BF16)
    for h in range(MLA_HEADS):
        qn = q[:, h * MLA_QK:h * MLA_QK + MLA_NOPE] * scale
        qr = rope(q[:, h * MLA_QK + MLA_NOPE:(h + 1) * MLA_QK]) * scale
        q_ref[0, h] = jnp.concatenate([qn, qr], axis=1).astype(BF16)
        k_ref[0, h] = jnp.concatenate([kv[:, h * MLA_NOPE:(h + 1) * MLA_NOPE].astype(BF16), kr], axis=1)
        vh = kv[:, MLA_WIDTH + h * MLA_V:MLA_WIDTH + (h + 1) * MLA_V]
        v_ref[0, h] = jnp.concatenate([vh, jnp.ones_like(vh)], axis=1).astype(BF16)


def _mla_prep(proj, qn, kvn, wq, wkv, cos_t, s1_t, s2_t):
    nb, s, _ = proj.shape
    tab = pl.BlockSpec((1, PREP_TM, LANES), lambda b, i: (b, i, 0))
    return pl.pallas_call(
        _mla_prep_kernel,
        grid=(nb, s // PREP_TM),
        in_specs=[
            pl.BlockSpec((1, PREP_TM, COL_BLK), lambda b, i: (b, i, 1)),
            pl.BlockSpec((1, PREP_TM, COL_BLK), lambda b, i: (b, i, 2)),
            pl.BlockSpec(qn.shape, lambda b, i: (0, 0)),
            pl.BlockSpec(kvn.shape, lambda b, i: (0, 0)),
            pl.BlockSpec(wq.shape, lambda b, i: (0, 0)),
            pl.BlockSpec(wkv.shape, lambda b, i: (0, 0)),
            tab, tab, tab,
        ],
        out_specs=[
            pl.BlockSpec((1, MLA_HEADS, PREP_TM, MLA_QK), lambda b, i: (b, 0, i, 0)),
            pl.BlockSpec((1, MLA_HEADS, PREP_TM, MLA_QK), lambda b, i: (b, 0, i, 0)),
            pl.BlockSpec((1, MLA_HEADS, PREP_TM, 2 * MLA_V), lambda b, i: (b, 0, i, 0)),
        ],
        out_shape=[
            jax.ShapeDtypeStruct((nb, MLA_HEADS, s, MLA_QK), BF16),
            jax.ShapeDtypeStruct((nb, MLA_HEADS, s, MLA_QK), BF16),
            jax.ShapeDtypeStruct((nb, MLA_HEADS, s, 2 * MLA_V), BF16),
        ],
        compiler_params=_params("parallel", "parallel"),
        name="mla_prep",
    )(proj, proj, qn, kvn, wq, wkv, cos_t, s1_t, s2_t)


def _flash_kernel(q_ref, k_ref, v_ref, o_ref, s_ref, p_ref, m_ref, a_ref, acc_ref):
    qi = pl.program_id(2)
    t = FLASH_T
    heads = range(FLASH_HEADS)
    nh = FLASH_HEADS
    m_ref[...] = jnp.full(m_ref.shape, -jnp.inf, F32)
    acc_ref[...] = jnp.zeros_like(acc_ref)
    p_ref[nh:] = jnp.zeros((nh,) + p_ref.shape[1:], BF16)
    a_ref[nh:] = jnp.ones((nh,) + a_ref.shape[1:], F32)

    def scores(j, slot):
        start = pl.multiple_of(j * t, t)
        for h in heads:
            s_ref[slot * nh + h] = _dot_nt(q_ref[0, h], k_ref[0, h, pl.ds(start, t), :])

    def softmax(slot, diagonal):
        for h in heads:
            for r0 in range(0, t, FLASH_ROWS):
                rows = slice(r0, r0 + FLASH_ROWS)
                sc = s_ref[slot * nh + h, rows, :]
                if diagonal:
                    row = lax.broadcasted_iota(jnp.int32, sc.shape, 0) + r0
                    col = lax.broadcasted_iota(jnp.int32, sc.shape, 1)
                    sc = jnp.where(col <= row, sc, MASK_VALUE)
                m_old = m_ref[h, rows, :]
                m_new = jnp.maximum(m_old, jnp.max(sc, axis=-1, keepdims=True))
                p_ref[slot * nh + h, rows, :] = jnp.exp2(sc - m_new[:, :1]).astype(BF16)
                a_ref[slot * nh + h, rows, :] = jnp.exp2(m_old - m_new)
                m_ref[h, rows, :] = m_new

    def accumulate(j, slot):
        start = pl.multiple_of(j * t, t)
        for h in heads:
            pv = _dot(p_ref[slot * nh + h], v_ref[0, h, pl.ds(start, t), :])
            alpha = a_ref[slot * nh + h]
            acc_ref[h] = jnp.concatenate([alpha, alpha], axis=1) * acc_ref[h] + pv

    def stage(j, slot, with_accumulate=True):
        softmax(slot, False)
        scores(j + 1, 1 - slot)
        if with_accumulate:
            accumulate(jnp.maximum(j - 1, 0), 1 - slot)

    odd = lax.rem(qi, 2)
    for slot in range(2):
        @pl.when(odd == slot)
        def _():
            scores(0, slot)

    @pl.when(odd == 1)
    def _():
        stage(0, 1, with_accumulate=False)

    def body(jj, carry):
        j = 2 * jj + odd
        stage(j, 0)
        stage(j + 1, 1)
        return carry

    lax.fori_loop(0, qi // 2, body, 0)
    softmax(0, True)
    accumulate(jnp.maximum(qi - 1, 0), 1)
    accumulate(qi, 0)

    for h in heads:
        acc = acc_ref[h]
        o_ref[0, :, h * MLA_V:(h + 1) * MLA_V] = (acc[:, :MLA_V] / acc[:, MLA_V:]).astype(o_ref.dtype)


def _flash(q, k, v):
    nb, nh, s, _ = q.shape
    hb = FLASH_HEADS
    return pl.pallas_call(
        _flash_kernel,
        grid=(nb, nh // hb, s // FLASH_T),
        in_specs=[
            pl.BlockSpec((1, hb, FLASH_T, MLA_QK), lambda b, h, i: (b, h, i, 0)),
            pl.BlockSpec((1, hb, s, MLA_QK), lambda b, h, i: (b, h, 0, 0)),
            pl.BlockSpec((1, hb, s, 2 * MLA_V), lambda b, h, i: (b, h, 0, 0)),
        ],
        out_specs=pl.BlockSpec((1, FLASH_T, hb * MLA_V), lambda b, h, i: (b, i, h)),
        out_shape=jax.ShapeDtypeStruct((nb, s, nh * MLA_V), BF16),
        scratch_shapes=[pltpu.VMEM((2 * hb, FLASH_T, FLASH_T), F32), pltpu.VMEM((2 * hb, FLASH_T, FLASH_T), BF16),
                        pltpu.VMEM((hb, FLASH_T, LANES), F32), pltpu.VMEM((2 * hb, FLASH_T, LANES), F32),
                        pltpu.VMEM((hb, FLASH_T, 2 * MLA_V), F32)],
        compiler_params=_params("parallel", "parallel", "arbitrary"),
        name="flash",
    )(q, k, v)


def _cumsum_rows(x, ltri):
    hi = x.astype(BF16)
    r1 = x - hi.astype(F32)
    mid = r1.astype(BF16)
    lo = (r1 - mid.astype(F32)).astype(BF16)
    return _dot(ltri, hi) + _dot(ltri, mid) + _dot(ltri, lo)


def _hgrn_kernel(q_ref, f_ref, i_ref, g_ref, lb_ref, on_ref, ones_ref, ltri_ref, o_ref, st_ref):
    @pl.when(pl.program_id(1) == 0)
    def _():
        st_ref[...] = jnp.zeros_like(st_ref)

    for n in range(q_ref.shape[0]):
        _hgrn_chunk(n, q_ref, f_ref, i_ref, g_ref, lb_ref, on_ref, ones_ref, ltri_ref, o_ref, st_ref)


def _hgrn_chunk(n, q_ref, f_ref, i_ref, g_ref, lb_ref, on_ref, ones_ref, ltri_ref, o_ref, st_ref):
    c, w = HG_CHUNK, HG_WIDTH
    z = f_ref[n]
    lb = lb_ref[...]
    e = jnp.exp(-jnp.abs(z))
    r = 1.0 / (1.0 + e)
    pos = z >= 0.0
    sig_z = jnp.where(pos, r, e * r)
    sig_mz = jnp.where(pos, e * r, r)
    kk = (1.0 - lb) * sig_mz
    q = _silu(q_ref[n])
    v = i_ref[n]
    gate = _silu(g_ref[n])
    f = lb + (1.0 - lb) * sig_z
    b = _cumsum_rows(jnp.log2(f), ltri_ref[...])
    row = lax.broadcasted_iota(jnp.int32, (c, w), 0)

    q_lvls, k_lvls = [], []
    for m in HG_LEVELS:
        half = m // 2
        b4 = b.reshape(c // m, 2, half, w)
        ref_b = b4[:, 0, half - 1:half, :]
        k_first = kk.reshape(c // m, 2, half, w)[:, 0] * jnp.exp2(ref_b - b4[:, 0])
        q_second = q.reshape(c // m, 2, half, w)[:, 1] * jnp.exp2(b4[:, 1] - ref_b)
        zero = jnp.zeros_like(k_first)
        q_lvls.append(jnp.stack([zero, q_second], axis=1).reshape(c, w).astype(BF16))
        k_lvls.append(jnp.stack([k_first, zero], axis=1).reshape(c, w).astype(BF16))

    pos8 = row & (SUBLANES - 1)

    def back1(x):
        x3 = x.reshape(c // SUBLANES, SUBLANES, w)
        return pltpu.roll(x3, 1, 1).reshape(c, w)

    f_in = jnp.where(pos8 >= 1, f, 0.0)
    kd_dl, vr = kk, v
    o_diag = jnp.zeros((c, w), F32)
    for dl in range(SUBLANES):
        if dl > 0:
            kd_dl = f_in * back1(kd_dl)
            vr = back1(vr)
        wb = (q * kd_dl).astype(BF16)
        a = jnp.concatenate([_dot(wb[:, :MXU_DIM], ones_ref[...]), _dot(wb[:, MXU_DIM:], ones_ref[...])], axis=1)
        o_diag = o_diag + a * vr

    tt = lax.broadcasted_iota(jnp.int32, (c, c), 0)
    ss = lax.broadcasted_iota(jnp.int32, (c, c), 1)
    same_blk = {m: (tt // m) == (ss // m) for m in HG_LEVELS}

    b_last = b[c - 1:c, :]
    qe = (q * jnp.exp2(b)).astype(BF16)
    kd = (kk * jnp.exp2(b_last - b)).astype(BF16)
    decay = jnp.exp2(b_last)
    vb = v.astype(BF16)
    for h in range(HG_HEADS):
        hs = slice(h * HG_DK, (h + 1) * HG_DK)
        st = st_ref[n, h]
        attn = jnp.zeros((c, c), F32)
        for m, ql, kl in zip(HG_LEVELS, q_lvls, k_lvls):
            part = _dot_nt(ql[:, hs], kl[:, hs])
            attn = attn + (part if m == c else jnp.where(same_blk[m], part, 0.0))
        o_h =_dot_nt(qe[:, hs], st.astype(BF16)) + _dot(attn.astype(BF16), vb[:, hs]) + o_diag[:, hs]
        st_ref[n, h] = st * decay[:, hs] + _dot_tn(vb[:, hs], kd[:, hs])
        o_ref[n, :, hs] = (_rms(o_h, on_ref[...]) * gate[:, hs]).astype(o_ref.dtype)


def _hgrn(proj, lb, onorm):
    nb, s, _ = proj.shape
    head_id = jnp.arange(MXU_DIM) // HG_DK
    ones_blk = (head_id[:, None] == head_id[None, :]).astype(BF16)
    t_id = jnp.arange(HG_CHUNK)
    ltri = (t_id[:, None] >= t_id[None, :]).astype(BF16)
    ns = HG_SEQS if nb % HG_SEQS == 0 else 1
    spec = lambda cb: pl.BlockSpec((ns, HG_CHUNK, COL_BLK), lambda b, i: (b, i, cb))
    return pl.pallas_call(
        _hgrn_kernel,
        grid=(nb // ns, s // HG_CHUNK),
        in_specs=[spec(3), spec(4), spec(5), spec(6),
                  pl.BlockSpec(lb.shape, lambda b, i: (0, 0)),
                  pl.BlockSpec(onorm.shape, lambda b, i: (0, 0)),
                  pl.BlockSpec(ones_blk.shape, lambda b, i: (0, 0)),
                  pl.BlockSpec(ltri.shape, lambda b, i: (0, 0))],
        out_specs=pl.BlockSpec((ns, HG_CHUNK, HG_WIDTH), lambda b, i: (b, i, 0)),
        out_shape=jax.ShapeDtypeStruct((nb, s, HG_WIDTH), BF16),
        scratch_shapes=[pltpu.VMEM((ns, HG_HEADS, HG_DV, HG_DK), F32)],
        compiler_params=_params("parallel", "arbitrary"),
        name="hgrn",
    )(proj, proj, proj, proj, lb, onorm, ones_blk, ltri)


def _out_proj_kernel(ys_ref, ym_ref, yh_ref, w_ref, x_ref, gain_ref, g_ref, o_ref, slab_ref):
    for t in range(S5_LS):
        for c in range(S5_WIDTH // LANES):
            col = t * S5_WIDTH + c * LANES
            slab_ref[c, pl.ds(t, OUT_TM // S5_LS, stride=S5_LS), :] = ys_ref[0, :, col:col + LANES]
    scale = gain_ref[...] * g_ref[0]
    part = OUT_TM // 2
    for p0 in range(0, OUT_TM, part):
        rows = slice(p0, p0 + part)
        ys = jnp.concatenate([slab_ref[c, rows, :] for c in range(S5_WIDTH // LANES)], axis=1).astype(BF16)
        mixed = (_dot(ys, w_ref[:S5_WIDTH, :])
                 + _dot(ym_ref[0, rows, :], w_ref[S5_WIDTH:S5_WIDTH + MLA_WIDTH, :])
                 + _dot(yh_ref[0, rows, :], w_ref[S5_WIDTH + MLA_WIDTH:, :]))
        inv = lax.rsqrt(jnp.mean(mixed * mixed, axis=-1, keepdims=True) + EPS)
        o_ref[0, rows, :] = x_ref[0, rows, :] + mixed * inv * scale


def _out_proj(y_s5_rows, y_mla, y_hg, w, l, x, gain, g):
    nb, s, d = x.shape
    row = lambda n: pl.BlockSpec((1, OUT_TM, n), lambda b, i: (b, i, 0))
    return pl.pallas_call(
        _out_proj_kernel,
        grid=(nb, s // OUT_TM),
        in_specs=[pl.BlockSpec((1, OUT_TM // S5_LS, S5_LS * S5_WIDTH), lambda b, i: (b, i, 0)),
                  row(MLA_WIDTH), row(HG_WIDTH),
                  pl.BlockSpec((None,) + w.shape[1:], lambda b, i: (l, 0, 0), pipeline_mode=pl.Buffered(1)),
                  row(d),
                  pl.BlockSpec((1, d), lambda b, i: (0, 0)),
                  pl.BlockSpec((1, 1, d), lambda b, i: (b, 0, 0))],
        out_specs=pl.BlockSpec((1, OUT_TM, d), lambda b, i: (b, i, 0), pipeline_mode=pl.Buffered(1)),
        out_shape=jax.ShapeDtypeStruct((nb, s, d), F32),
        scratch_shapes=[pltpu.VMEM((S5_WIDTH // LANES, OUT_TM, LANES), F32)],
        compiler_params=_params("parallel", "parallel"),
        name="out_proj",
    )(y_s5_rows, y_mla, y_hg, w, x, gain, g)


def _ffn_kernel(x_ref, sc_ref, sh_ref, g_ref, pre_ref, post_ref, wu_ref, cp_ref, wd_ref, o_ref,
                h_ref, ug_ref, uv_ref, act_ref, tail_ref):
    i = pl.program_id(1)
    j = pl.program_id(2)

    @pl.when(j == 0)
    def _():
        scale = pre_ref[...] * (1.0 + sc_ref[0])
        shift = sh_ref[0]
        for r0 in range(0, FFN_TM, NORM_ROWS):
            x = x_ref[0, r0:r0 + NORM_ROWS, :]
            inv = lax.rsqrt(jnp.mean(x * x, axis=-1, keepdims=True) + EPS)
            h_ref[r0:r0 + NORM_ROWS, :] = (x * inv * scale + shift).astype(BF16)
        o_ref[...] = jnp.zeros_like(o_ref)

    cw = FFN_TF // FFN_SPLITS
    for t, u_ref in enumerate((ug_ref, uv_ref)):
        u_ref[:HALO, :] = jnp.where(i == 0, 0.0, tail_ref[j, t])
    for q in range(FFN_SPLITS):
        cols = slice(q * cw, (q + 1) * cw)
        u = _dot(h_ref[...], wu_ref[:, 2 * q * cw:2 * (q + 1) * cw])
        for t, u_ref in enumerate((ug_ref, uv_ref)):
            u_ref[HALO:, cols] = u[:, t * cw:(t + 1) * cw]
            tail_ref[j, t, :, cols] = u[FFN_TM - HALO:, t * cw:(t + 1) * cw]

    def conv(u_ref, p0, r0, cols):
        return (cp_ref[p0 + CONV_W:p0 + CONV_W + 1, cols]
                + u_ref[r0 + HALO - 2:r0 + HALO - 2 + FFN_ROWS, cols] * cp_ref[p0:p0 + 1, cols]
                + u_ref[r0 + HALO - 1:r0 + HALO - 1 + FFN_ROWS, cols] * cp_ref[p0 + 1:p0 + 2, cols]
                + u_ref[r0 + HALO:r0 + HALO + FFN_ROWS, cols] * cp_ref[p0 + 2:p0 + 3, cols])

    for q in range(FFN_SPLITS):
        cols = slice(q * cw, (q + 1) * cw)
        for r0 in range(0, FFN_TM, FFN_ROWS):
            gate = conv(ug_ref, 0, r0, cols)
            val = conv(uv_ref, CONV_W + 1, r0, cols)
            act_ref[r0:r0 + FFN_ROWS, cols] = (_gelu_tanh(gate) * val).astype(BF16)
        o_ref[0] += _dot(act_ref[:, cols], wd_ref[cols, :])

    @pl.when(j == pl.num_programs(2) - 1)
    def _():
        scale = post_ref[...] * g_ref[0]
        for r0 in range(0, FFN_TM, NORM_ROWS):
            y = o_ref[0, r0:r0 + NORM_ROWS, :]
            inv = lax.rsqrt(jnp.mean(y * y, axis=-1, keepdims=True) + EPS)
            o_ref[0, r0:r0 + NORM_ROWS, :] = x_ref[0, r0:r0 + NORM_ROWS, :] + y * inv * scale


def _ffn(x, sc, sh, g, pre, post, w_up, conv_p, w_down, l):
    nb, s, d = x.shape
    ff = w_down.shape[1]
    nf = ff // FFN_TF
    vec = pl.BlockSpec((1, 1, d), lambda b, i, j: (b, 0, 0))
    par = pl.BlockSpec((1, d), lambda b, i, j: (0, 0))
    return pl.pallas_call(
        _ffn_kernel,
        grid=(nb, s // FFN_TM, nf),
        in_specs=[
            pl.BlockSpec((1, FFN_TM, d), lambda b, i, j: (b, i, 0)),
            vec, vec, vec, par, par,
            pl.BlockSpec((None, d, 2 * FFN_TF), lambda b, i, j: (l, 0, j)),
            pl.BlockSpec((None, None, 2 * (CONV_W + 1), FFN_TF), lambda b, i, j: (l, j, 0, 0)),
            pl.BlockSpec((None, FFN_TF, d), lambda b, i, j: (l, j, 0)),
        ],
        out_specs=pl.BlockSpec((1, FFN_TM, d), lambda b, i, j: (b, i, 0), pipeline_mode=pl.Buffered(1)),
        out_shape=jax.ShapeDtypeStruct((nb, s, d), F32),
        scratch_shapes=[pltpu.VMEM((FFN_TM, d), BF16),
                        pltpu.VMEM((FFN_TM + HALO, FFN_TF), F32), pltpu.VMEM((FFN_TM + HALO, FFN_TF), F32),
                        pltpu.VMEM((FFN_TM, FFN_TF), BF16), pltpu.VMEM((nf, 2, HALO, FFN_TF), F32)],
        compiler_params=_params("parallel", "arbitrary", "arbitrary"),
        name="ffn",
    )(x, sc, sh, g, pre, post, w_up, conv_p, w_down)


def _pad_cols(a, n):
    return jnp.pad(a, [(0, 0)] * (a.ndim - 1) + [(0, n - a.shape[-1])])


def _cast_w_up_kernel(w_ref, o_ref):
    cw = FFN_TF // FFN_SPLITS
    for h in range(2):
        @pl.when(pl.program_id(2) == h)
        def _():
            for p in range(D_FF_PAD // cw):
                width = max(0, min(cw, D_FF - p * cw))
                dst = (2 * p + h) * cw
                if width > 0:
                    o_ref[0, :, dst:dst + width] = w_ref[0, :, p * cw:p * cw + width].astype(BF16)
                if width < cw:
                    o_ref[0, :, dst + width:dst + cw] = jnp.zeros((o_ref.shape[1], cw - width), BF16)


def _cast_w_up(w_up):
    nl, d, _ = w_up.shape
    return pl.pallas_call(
        _cast_w_up_kernel,
        grid=(nl, d // CAST_TM, 2),
        in_specs=[pl.BlockSpec((1, CAST_TM, D_FF), lambda l, i, h: (l, i, h))],
        out_specs=pl.BlockSpec((1, CAST_TM, 2 * D_FF_PAD), lambda l, i, h: (l, i, 0)),
        out_shape=jax.ShapeDtypeStruct((nl, d, 2 * D_FF_PAD), BF16),
        compiler_params=_params("parallel", "parallel", "arbitrary"),
        name="cast_w_up",
    )(w_up)


def _cast_w_down_kernel(w_ref, o_ref):
    o_ref[0, :D_FF, :] = w_ref[0].astype(BF16)
    o_ref[0, D_FF:, :] = jnp.zeros((D_FF_PAD - D_FF, o_ref.shape[2]), BF16)


def _cast_w_down(w_down):
    nl, ff, d = w_down.shape
    return pl.pallas_call(
        _cast_w_down_kernel,
        grid=(nl, d // MXU_DIM),
        in_specs=[pl.BlockSpec((1, ff, MXU_DIM), lambda l, j: (l, 0, j))],
        out_specs=pl.BlockSpec((1, D_FF_PAD, MXU_DIM), lambda l, j: (l, 0, j)),
        out_shape=jax.ShapeDtypeStruct((nl, D_FF_PAD, d), BF16),
        compiler_params=_params("parallel", "parallel"),
        name="cast_w_down",
    )(w_down)


W_IN_HEAD = S5_WIDTH + MLA_Q_RANK + MLA_KV_RANK + MLA_ROPE


def _cast_w_in_kernel(w_ref, o_ref):
    w = w_ref[0]
    gap = jnp.zeros((w.shape[0], 3 * COL_BLK - W_IN_HEAD), w.dtype)
    o_ref[0] = jnp.concatenate([w[:, :W_IN_HEAD], gap, w[:, W_IN_HEAD:]], axis=1).astype(BF16)


def _layout_w_in(w_in):
    nl, d, n = w_in.shape
    return pl.pallas_call(
        _cast_w_in_kernel,
        grid=(nl, d // CAST_TM),
        in_specs=[pl.BlockSpec((1, CAST_TM, n), lambda l, i: (l, i, 0))],
        out_specs=pl.BlockSpec((1, CAST_TM, D_IN_PAD), lambda l, i: (l, i, 0)),
        out_shape=jax.ShapeDtypeStruct((nl, d, D_IN_PAD), BF16),
        compiler_params=_params("parallel", "parallel"),
        name="cast_w_in",
    )(w_in)


def _layout_w_uq(w):
    nl, r, _ = w.shape
    w = w.reshape(nl, r, MLA_HEADS, MLA_NOPE + MLA_ROPE)
    return _pad_cols(w, MLA_QK).reshape(nl, r, MLA_HEADS * MLA_QK).astype(BF16)


def _layout_w_ukv(w):
    nl, r, _ = w.shape
    w = w.reshape(nl, r, MLA_HEADS, MLA_NOPE + MLA_V)
    k = w[..., :MLA_NOPE].reshape(nl, r, MLA_HEADS * MLA_NOPE)
    v = w[..., MLA_NOPE:].reshape(nl, r, MLA_HEADS * MLA_V)
    return jnp.concatenate([k, v], axis=-1).astype(BF16)


def _rope_tables(positions):
    inv_freq = 1.0 / (ROPE_THETA ** (jnp.arange(0, MLA_ROPE, 2, dtype=F32) / MLA_ROPE))
    ang = positions.astype(F32)[..., None] * inv_freq
    cos, sin = jnp.cos(ang), jnp.sin(ang)
    z32 = jnp.zeros_like(cos)
    z64 = jnp.concatenate([z32, z32], axis=-1)
    return (jnp.concatenate([cos, cos, z64], axis=-1),
            jnp.concatenate([z32, sin, z64], axis=-1),
            jnp.concatenate([-sin, z32, z64], axis=-1))


def kernel(x, c, positions, w_in, s5_lambda_re, s5_lambda_im, s5_log_dt, s5_b_re, s5_b_im, s5_c_re, s5_c_im, s5_d, s5_w_glu, mla_q_norm, mla_w_uq, mla_kv_norm, mla_w_ukv, hg_lb_logits, hg_out_norm, w_out, mix_pre_norm, mix_post_norm, ffn_pre_norm, ffn_post_norm, ffn_w_up, ffn_conv_w, ffn_conv_b, ffn_w_down, w_ada, b_ada):
    nb, s, d = x.shape
    nl = w_in.shape[0]
    assert s % FLASH_T == 0 and s % FFN_TM == 0 and s % (S5_LS * SUBLANES) == 0

    mod = _adaln(c, w_ada, b_ada)
    cos_t, s1_t, s2_t = _rope_tables(positions)
    probs = jax.nn.softmax(hg_lb_logits.astype(F32), axis=0)
    lower_bounds = jnp.cumsum(probs, axis=0) - probs[0:1]

    w_in_p = _layout_w_in(w_in)
    w_uq_p = _layout_w_uq(mla_w_uq)
    w_ukv_p = _layout_w_ukv(mla_w_ukv)
    w_out_b = w_out.astype(BF16)
    w_glu_b = s5_w_glu.astype(BF16)
    w_up_p = _cast_w_up(ffn_w_up)
    conv_rows = jnp.concatenate([ffn_conv_w[..., :D_FF], ffn_conv_b[:, None, :D_FF],
                                 ffn_conv_w[..., D_FF:], ffn_conv_b[:, None, D_FF:]], axis=1)
    conv_p = _pad_cols(conv_rows, D_FF_PAD).reshape(nl, 2 * (CONV_W + 1), -1, FFN_TF).transpose(0, 2, 1, 3)
    w_down = _cast_w_down(ffn_w_down)

    for l in range(nl):
        sh1, sc1, g1, sh2, sc2, g2 = [m[:, None, :] for m in jnp.split(mod[l], 6, axis=-1)]
        proj, u_rows = _proj_in(x, sc1, sh1, mix_pre_norm[l][None, :], w_in_p, l)
        bm, cm, lam, lam_chunk = _s5_prep(s5_lambda_re[l], s5_lambda_im[l], s5_log_dt[l],
                                          s5_b_re[l], s5_b_im[l], s5_c_re[l], s5_c_im[l])
        y_s5 = _s5(u_rows, bm, cm, lam, lam_chunk, s5_d[l][None, :], w_glu_b[l])
        q, k, v = _mla_prep(proj, mla_q_norm[l][None, :], mla_kv_norm[l][None, :], w_uq_p[l], w_ukv_p[l],
                            cos_t, s1_t, s2_t)
        y_mla = _flash(q, k, v)
        y_hg = _hgrn(proj, lower_bounds[l][None, :], hg_out_norm[l][None, :])
        x = _out_proj(y_s5, y_mla, y_hg, w_out_b, l, x, mix_post_norm[l][None, :], g1)
        x = _ffn(x, sc2, sh2, g2, ffn_pre_norm[l][None, :], ffn_post_norm[l][None, :],
                 w_up_p, conv_p, w_down, l)
    return x
```

```python
import functools
import math

import jax
import jax.numpy as jnp
from jax import lax
from jax.experimental import pallas as pl
from jax.experimental.pallas import tpu as pltpu

F32 = jnp.float32
BF16 = jnp.bfloat16

D_MODEL = 2048
DEPTH = 4
S5_WIDTH = 512
S5_GROUP = 16
S5_GROUPS = 32
S5_STATE = 64
MLA_HEADS = 8
MLA_NOPE = 128
MLA_ROPE = 64
MLA_V = 128
MLA_Q_RANK = 512
MLA_KV_RANK = 256
MLA_WIDTH = MLA_HEADS * MLA_V
ROPE_THETA = 10000.0
MASK_VALUE = -1e30
HG_HEADS = 4
HG_DK = 128
HG_DV = 128
HG_WIDTH = HG_HEADS * HG_DV
D_FF = 5504
CONV_W = 3
EPS = 1e-6

LANES = 128
SUBLANES = 8
MXU_DIM = 256
VMEM_LIMIT_BYTES = 56 * 1024 * 1024

COL_BLK = 512
N_COL_BLKS = 7
D_IN_PAD = N_COL_BLKS * COL_BLK
MLA_QK = 2 * LANES
D_FF_PAD = 5632

ADA_TN = 1024
PROJ_TM = 512
PROJ_PARTS = 2
S5_LS = 16
S5_SEQS = 2
S5_HALF = S5_WIDTH // 2
S5_HSTATE = (S5_GROUPS // 2) * S5_STATE
PREP_TM = 1024
FLASH_T = 512
FLASH_HEADS = 2
FLASH_ROWS = 32
HG_CHUNK = 128
HG_LEVELS = (16, 32, 64, 128)
HG_SEQS = 4
OUT_TM = 512
FFN_TM = 1024
FFN_TF = 512
FFN_ROWS = 64
FFN_SPLITS = 1
NORM_ROWS = 16
CAST_TM = 256
HALO = SUBLANES


def _sigmoid(x):
    return 1.0 / (1.0 + jnp.exp(-x))


def _silu(x):
    return x * _sigmoid(x)


def _gelu_tanh(x):
    c = math.sqrt(2.0 / math.pi)
    return x * (0.5 * (1.0 + jnp.tanh(c * (x + 0.044715 * (x * x * x)))))


def _rms(x, gain):
    return x * lax.rsqrt(jnp.mean(x * x, axis=-1, keepdims=True) + EPS) * gain


def _dot(a, b):
    return jnp.dot(a, b, preferred_element_type=F32)


def _dot_nt(a, b):
    return lax.dot_general(a, b, (((1,), (1,)), ((), ())), preferred_element_type=F32)


def _dot_tn(a, b):
    return lax.dot_general(a, b, (((0,), (0,)), ((), ())), preferred_element_type=F32)


def _params(*sem):
    return pltpu.CompilerParams(dimension_semantics=sem, vmem_limit_bytes=VMEM_LIMIT_BYTES)


def _adaln_kernel(c_ref, w_ref, b_ref, o_ref):
    ca = _silu(c_ref[...]).astype(BF16)
    o_ref[0] = _dot(ca, w_ref[0].astype(BF16)) + b_ref[0]


def _adaln(c, w_ada, b_ada):
    nb, d = c.shape
    nl, _, n = w_ada.shape
    rows = -(-nb // SUBLANES) * SUBLANES
    c_pad = jnp.pad(c, ((0, rows - nb), (0, 0)))
    out = pl.pallas_call(
        _adaln_kernel,
        grid=(nl, n // ADA_TN),
        in_specs=[
            pl.BlockSpec((rows, d), lambda l, j: (0, 0)),
            pl.BlockSpec((1, d, ADA_TN), lambda l, j: (l, 0, j)),
            pl.BlockSpec((1, 1, ADA_TN), lambda l, j: (l, 0, j)),
        ],
        out_specs=pl.BlockSpec((1, rows, ADA_TN), lambda l, j: (l, 0, j)),
        out_shape=jax.ShapeDtypeStruct((nl, rows, n), F32),
        compiler_params=_params("parallel", "parallel"),
        name="adaln",
    )(c_pad, w_ada, b_ada.reshape(nl, 1, n))
    return out[:, :nb]


def _proj_in_kernel(x_ref, sc_ref, sh_ref, gain_ref, w_ref, o_ref, u_ref, h_ref, slab_ref):
    scale = gain_ref[...] * (1.0 + sc_ref[0])
    shift = sh_ref[0]
    part = PROJ_TM // PROJ_PARTS
    for p0 in range(0, PROJ_TM, part):
        for r0 in range(p0, p0 + part, NORM_ROWS):
            x = x_ref[0, r0:r0 + NORM_ROWS, :]
            inv = lax.rsqrt(jnp.mean(x * x, axis=-1, keepdims=True) + EPS)
            h_ref[r0:r0 + NORM_ROWS, :] = (x * inv * scale + shift).astype(BF16)
        res = _dot(h_ref[p0:p0 + part, :], w_ref[...])
        o_ref[0, p0:p0 + part, :] = res
        for c in range(S5_WIDTH // LANES):
            slab_ref[c, p0:p0 + part, :] = res[:, c * LANES:(c + 1) * LANES]

    for t in range(S5_LS):
        for c in range(S5_WIDTH // LANES):
            col = t * S5_WIDTH + c * LANES
            u_ref[0, :, col:col + LANES] = slab_ref[c, pl.ds(t, PROJ_TM // S5_LS, stride=S5_LS), :]


def _proj_in(x, sc, sh, gain, w, l):
    nb, s, d = x.shape
    n = w.shape[2]
    rows = PROJ_TM // S5_LS
    return pl.pallas_call(
        _proj_in_kernel,
        grid=(nb, s // PROJ_TM),
        in_specs=[
            pl.BlockSpec((1, PROJ_TM, d), lambda b, i: (b, i, 0)),
            pl.BlockSpec((1, 1, d), lambda b, i: (b, 0, 0)),
            pl.BlockSpec((1, 1, d), lambda b, i: (b, 0, 0)),
            pl.BlockSpec((1, d), lambda b, i: (0, 0)),
            pl.BlockSpec((None, d, n), lambda b, i: (l, 0, 0), pipeline_mode=pl.Buffered(1)),
        ],
        out_specs=[pl.BlockSpec((1, PROJ_TM, n), lambda b, i: (b, i, 0)),
                   pl.BlockSpec((1, rows, S5_LS * S5_WIDTH), lambda b, i: (b, i, 0))],
        out_shape=[jax.ShapeDtypeStruct((nb, s, n), F32),
                   jax.ShapeDtypeStruct((nb, s // S5_LS, S5_LS * S5_WIDTH), F32)],
        scratch_shapes=[pltpu.VMEM((PROJ_TM, d), BF16), pltpu.VMEM((S5_WIDTH // LANES, PROJ_TM, LANES), F32)],
        compiler_params=_params("parallel", "parallel"),
        name="proj_in",
    )(x, sc, sh, gain, w)


def _s5_kernel(u_ref, bm_ref, cm_ref, lam_ref, lam_chunk_ref, d_ref, wglu_ref, o_ref,
               sre, sim, hre, him, yloc):
    ph = pl.program_id(1)
    s = pl.program_id(2)
    nseq, kc, _ = u_ref.shape
    rows = nseq * kc

    @pl.when(ph == 0)
    def _():
        @pl.when(s == 0)
        def _():
            sre[...] = jnp.zeros_like(sre)
            sim[...] = jnp.zeros_like(sim)

        u = u_ref[...].reshape(rows, S5_WIDTH)
        ub = u.astype(BF16)
        for h in range(2):
            cols = slice(h * S5_HALF, (h + 1) * S5_HALF)
            v = _dot(ub[:, cols], bm_ref[h])
            lr = lam_ref[h:h + 1, :]
            li = lam_ref[2 + h:3 + h, :]
            re = sre[h]
            im = sim[h]
            nre = lr * re - li * im + v[:, :S5_HSTATE]
            nim = lr * im + li * re + v[:, S5_HSTATE:]
            sre[h] = nre
            sim[h] = nim
            yl = _dot(nre.astype(BF16), cm_ref[h, 0]) + _dot(nim.astype(BF16), cm_ref[h, 1])
            yloc[s, :, cols] = yl + d_ref[:, cols] * u[:, cols]

        @pl.when(s == S5_LS - 1)
        def _():
            def body(k, carry):
                new = []
                for n in range(nseq):
                    row = pl.ds(n * kc + k, 1)
                    for h in range(2):
                        cre, cim = carry[4 * n + 2 * h], carry[4 * n + 2 * h + 1]
                        hre[h, row, :] = cre
                        him[h, row, :] = cim
                        lr = lam_chunk_ref[h:h + 1, :]
                        li = lam_chunk_ref[2 + h:3 + h, :]
                        new.append(lr * cre - li * cim + sre[h, row, :])
                        new.append(lr * cim + li * cre + sim[h, row, :])
                return tuple(new)

            zero = jnp.zeros((1, S5_HSTATE), F32)
            lax.fori_loop(0, kc, body, (zero,) * (4 * nseq))

    @pl.when(ph == 1)
    def _():
        ys = []
        for h in range(2):
            cols = slice(h * S5_HALF, (h + 1) * S5_HALF)
            lr = lam_ref[h:h + 1, :]
            li = lam_ref[2 + h:3 + h, :]
            pre = hre[h]
            pim = him[h]
            nre = lr * pre - li * pim
            nim = lr * pim + li * pre
            hre[h] = nre
            him[h] = nim
            yc = _dot(nre.astype(BF16), cm_ref[h, 0]) + _dot(nim.astype(BF16), cm_ref[h, 1])
            ys.append(yloc[s, :, cols] + yc)
        g = _gelu_tanh(jnp.concatenate(ys, axis=1))
        gate = _sigmoid(_dot(g.astype(BF16), wglu_ref[...]))
        o_ref[...] = (g * gate).reshape(o_ref.shape).astype(o_ref.dtype)


def _s5(u_rows, bm, cm, lam, lam_chunk, d, wglu):
    nb, kc, _ = u_rows.shape
    ns = S5_SEQS if nb % S5_SEQS == 0 else 1
    rows = ns * kc
    return pl.pallas_call(
        _s5_kernel,
        grid=(nb // ns, 2, S5_LS),
        in_specs=[
            pl.BlockSpec((ns, kc, S5_WIDTH),
                         lambda b, p, t: (b, 0, jnp.where(p == 0, t, S5_LS - 1))),
            pl.BlockSpec(bm.shape, lambda b, p, t: (0, 0, 0)),
            pl.BlockSpec(cm.shape, lambda b, p, t: (0, 0, 0, 0)),
            pl.BlockSpec(lam.shape, lambda b, p, t: (0, 0)),
            pl.BlockSpec(lam_chunk.shape, lambda b, p, t: (0, 0)),
            pl.BlockSpec(d.shape, lambda b, p, t: (0, 0)),
            pl.BlockSpec(wglu.shape, lambda b, p, t: (0, 0)),
        ],
        out_specs=pl.BlockSpec((ns, kc, S5_WIDTH), lambda b, p, t: (b, 0, t * p)),
        out_shape=jax.ShapeDtypeStruct((nb, kc, S5_LS * S5_WIDTH), F32),
        scratch_shapes=[pltpu.VMEM((2, rows, S5_HSTATE), F32)] * 4
        + [pltpu.VMEM((S5_LS, rows, S5_WIDTH), F32)],
        compiler_params=_params("parallel", "arbitrary", "arbitrary"),
        name="s5",
    )(u_rows, bm, cm, lam, lam_chunk, d, wglu)


def _s5_prep(lam_re, lam_im, log_dt, b_re, b_im, c_re, c_im):
    lam_re, lam_im = lam_re.astype(F32), lam_im.astype(F32)
    dt = jnp.exp(log_dt.astype(F32))[:, None]
    mag = jnp.exp(lam_re * dt)
    bar_re, bar_im = mag * jnp.cos(lam_im * dt), mag * jnp.sin(lam_im * dt)
    den = lam_re * lam_re + lam_im * lam_im
    co_re = ((bar_re - 1.0) * lam_re + bar_im * lam_im) / den
    co_im = (bar_im * lam_re - (bar_re - 1.0) * lam_im) / den
    b_re, b_im = b_re.astype(F32), b_im.astype(F32)
    bb_re = co_re[..., None] * b_re - co_im[..., None] * b_im
    bb_im = co_re[..., None] * b_im + co_im[..., None] * b_re
    ch_re, ch_im = bar_re, bar_im
    for _ in range(int(math.log2(S5_LS))):
        ch_re, ch_im = ch_re * ch_re - ch_im * ch_im, 2.0 * ch_re * ch_im
    gh = S5_GROUPS // 2
    eye = jnp.eye(gh, dtype=F32)

    def b_tiles(part):
        t = part.reshape(2, gh, S5_STATE, S5_GROUP)
        return jnp.einsum('ab,hapc->hacbp', eye, t).reshape(2, S5_HALF, S5_HSTATE)

    def c_tiles(part):
        t = part.reshape(2, gh, S5_GROUP, S5_STATE)
        return jnp.einsum('ab,hbcp->hapbc', eye, t).reshape(2, S5_HSTATE, S5_HALF)

    bm = jnp.concatenate([b_tiles(bb_re), b_tiles(bb_im)], axis=-1).astype(BF16)
    cm = jnp.stack([c_tiles(c_re.astype(F32)), -c_tiles(c_im.astype(F32))], axis=1).astype(BF16)

    def rows(re, im):
        return jnp.concatenate([re.reshape(2, S5_HSTATE), im.reshape(2, S5_HSTATE)], 0)

    return bm, cm, rows(bar_re, bar_im), rows(ch_re, ch_im)


def _mla_prep_kernel(cq_ref, ckv_ref, qn_ref, kvn_ref, wq_ref, wkv_ref, cos_ref, s1_ref, s2_ref,
                     q_ref, k_ref, v_ref):
    cos_t = cos_ref[0]
    s1 = s1_ref[0]
    s2 = s2_ref[0]
    half = MLA_ROPE // 2

    def rope(x):
        return x * cos_t + pltpu.roll(x, half, 1) * s1 + pltpu.roll(x, LANES - half, 1) * s2

    scale = (MLA_NOPE + MLA_ROPE) ** -0.5 * math.log2(math.e)
    q = _dot(_rms(cq_ref[0], qn_ref[...]).astype(BF16), wq_ref[...])
    blk = ckv_ref[0]
    kv = _dot(_rms(blk[:, :MLA_KV_RANK], kvn_ref[...]).astype(BF16), wkv_ref[...])
    kr = rope(blk[:, MLA_KV_RANK:MLA_KV_RANK + LANES]).astype(BF16)
    for h in range(MLA_HEADS):
        qn = q[:, h * MLA_QK:h * MLA_QK + MLA_NOPE] * scale
        qr = rope(q[:, h * MLA_QK + MLA_NOPE:(h + 1) * MLA_QK]) * scale
        q_ref[0, h] = jnp.concatenate([qn, qr], axis=1).astype(BF16)
        k_ref[0, h] = jnp.concatenate([kv[:, h * MLA_NOPE:(h + 1) * MLA_NOPE].astype(BF16), kr], axis=1)
        vh = kv[:, MLA_WIDTH + h * MLA_V:MLA_WIDTH + (h + 1) * MLA_V]
        v_ref[0, h] = jnp.concatenate([vh, jnp.ones_like(vh)], axis=1).astype(BF16)


def _mla_prep(proj, qn, kvn, wq, wkv, cos_t, s1_t, s2_t):
    nb, s, _ = proj.shape
    tab = pl.BlockSpec((1, PREP_TM, LANES), lambda b, i: (b, i, 0))
    return pl.pallas_call(
        _mla_prep_kernel,
        grid=(nb, s // PREP_TM),
        in_specs=[
            pl.BlockSpec((1, PREP_TM, COL_BLK), lambda b, i: (b, i, 1)),
            pl.BlockSpec((1, PREP_TM, COL_BLK), lambda b, i: (b, i, 2)),
            pl.BlockSpec(qn.shape, lambda b, i: (0, 0)),
            pl.BlockSpec(kvn.shape, lambda b, i: (0, 0)),
            pl.BlockSpec(wq.shape, lambda b, i: (0, 0)),
            pl.BlockSpec(wkv.shape, lambda b, i: (0, 0)),
            tab, tab, tab,
        ],
        out_specs=[
            pl.BlockSpec((1, MLA_HEADS, PREP_TM, MLA_QK), lambda b, i: (b, 0, i, 0)),
            pl.BlockSpec((1, MLA_HEADS, PREP_TM, MLA_QK), lambda b, i: (b, 0, i, 0)),
            pl.BlockSpec((1, MLA_HEADS, PREP_TM, 2 * MLA_V), lambda b, i: (b, 0, i, 0)),
        ],
        out_shape=[
            jax.ShapeDtypeStruct((nb, MLA_HEADS, s, MLA_QK), BF16),
            jax.ShapeDtypeStruct((nb, MLA_HEADS, s, MLA_QK), BF16),
            jax.ShapeDtypeStruct((nb, MLA_HEADS, s, 2 * MLA_V), BF16),
        ],
        compiler_params=_params("parallel", "parallel"),
        name="mla_prep",
    )(proj, proj, qn, kvn, wq, wkv, cos_t, s1_t, s2_t)


def _flash_kernel(q_ref, k_ref, v_ref, o_ref, s_ref, p_ref, m_ref, a_ref, acc_ref):
    qi = pl.program_id(2)
    t = FLASH_T
    heads = range(FLASH_HEADS)
    nh = FLASH_HEADS
    m_ref[...] = jnp.full(m_ref.shape, -jnp.inf, F32)
    acc_ref[...] = jnp.zeros_like(acc_ref)
    p_ref[nh:] = jnp.zeros((nh,) + p_ref.shape[1:], BF16)
    a_ref[nh:] = jnp.ones((nh,) + a_ref.shape[1:], F32)

    def scores(j, slot):
        start = pl.multiple_of(j * t, t)
        for h in heads:
            s_ref[slot * nh + h] = _dot_nt(q_ref[0, h], k_ref[0, h, pl.ds(start, t), :])

    def softmax(slot, diagonal):
        for h in heads:
            for r0 in range(0, t, FLASH_ROWS):
                rows = slice(r0, r0 + FLASH_ROWS)
                sc = s_ref[slot * nh + h, rows, :]
                if diagonal:
                    row = lax.broadcasted_iota(jnp.int32, sc.shape, 0) + r0
                    col = lax.broadcasted_iota(jnp.int32, sc.shape, 1)
                    sc = jnp.where(col <= row, sc, MASK_VALUE)
                m_old = m_ref[h, rows, :]
                m_new = jnp.maximum(m_old, jnp.max(sc, axis=-1, keepdims=True))
                p_ref[slot * nh + h, rows, :] = jnp.exp2(sc - m_new[:, :1]).astype(BF16)
                a_ref[slot * nh + h, rows, :] = jnp.exp2(m_old - m_new)
                m_ref[h, rows, :] = m_new

    def accumulate(j, slot):
        start = pl.multiple_of(j * t, t)
        for h in heads:
            pv = _dot(p_ref[slot * nh + h], v_ref[0, h, pl.ds(start, t), :])
            alpha = a_ref[slot * nh + h]
            acc_ref[h] = jnp.concatenate([alpha, alpha], axis=1) * acc_ref[h] + pv

    def stage(j, slot, with_accumulate=True):
        softmax(slot, False)
        scores(j + 1, 1 - slot)
        if with_accumulate:
            accumulate(jnp.maximum(j - 1, 0), 1 - slot)

    odd = lax.rem(qi, 2)
    for slot in range(2):
        @pl.when(odd == slot)
        def _():
            scores(0, slot)

    @pl.when(odd == 1)
    def _():
        stage(0, 1, with_accumulate=False)

    def body(jj, carry):
        j = 2 * jj + odd
        stage(j, 0)
        stage(j + 1, 1)
        return carry

    lax.fori_loop(0, qi // 2, body, 0)
    softmax(0, True)
    accumulate(jnp.maximum(qi - 1, 0), 1)
    accumulate(qi, 0)

    for h in heads:
        acc = acc_ref[h]
        o_ref[0, :, h * MLA_V:(h + 1) * MLA_V] = (acc[:, :MLA_V] / acc[:, MLA_V:]).astype(o_ref.dtype)


def _flash(q, k, v):
    nb, nh, s, _ = q.shape
    hb = FLASH_HEADS
    return pl.pallas_call(
        _flash_kernel,
        grid=(nb, nh // hb, s // FLASH_T),
        in_specs=[
            pl.BlockSpec((1, hb, FLASH_T, MLA_QK), lambda b, h, i: (b, h, i, 0)),
            pl.BlockSpec((1, hb, s, MLA_QK), lambda b, h, i: (b, h, 0, 0)),
            pl.BlockSpec((1, hb, s, 2 * MLA_V), lambda b, h, i: (b, h, 0, 0)),
        ],
        out_specs=pl.BlockSpec((1, FLASH_T, hb * MLA_V), lambda b, h, i: (b, i, h)),
        out_shape=jax.ShapeDtypeStruct((nb, s, nh * MLA_V), BF16),
        scratch_shapes=[pltpu.VMEM((2 * hb, FLASH_T, FLASH_T), F32), pltpu.VMEM((2 * hb, FLASH_T, FLASH_T), BF16),
                        pltpu.VMEM((hb, FLASH_T, LANES), F32), pltpu.VMEM((2 * hb, FLASH_T, LANES), F32),
                        pltpu.VMEM((hb, FLASH_T, 2 * MLA_V), F32)],
        compiler_params=_params("parallel", "parallel", "arbitrary"),
        name="flash",
    )(q, k, v)


def _cumsum_rows(x, ltri):
    hi = x.astype(BF16)
    r1 = x - hi.astype(F32)
    mid = r1.astype(BF16)
    lo = (r1 - mid.astype(F32)).astype(BF16)
    return _dot(ltri, hi) + _dot(ltri, mid) + _dot(ltri, lo)


def _hgrn_kernel(q_ref, f_ref, i_ref, g_ref, lb_ref, on_ref, ones_ref, ltri_ref, o_ref, st_ref):
    @pl.when(pl.program_id(1) == 0)
    def _():
        st_ref[...] = jnp.zeros_like(st_ref)

    for n in range(q_ref.shape[0]):
        _hgrn_chunk(n, q_ref, f_ref, i_ref, g_ref, lb_ref, on_ref, ones_ref, ltri_ref, o_ref, st_ref)


def _hgrn_chunk(n, q_ref, f_ref, i_ref, g_ref, lb_ref, on_ref, ones_ref, ltri_ref, o_ref, st_ref):
    c, w = HG_CHUNK, HG_WIDTH
    z = f_ref[n]
    lb = lb_ref[...]
    e = jnp.exp(-jnp.abs(z))
    r = 1.0 / (1.0 + e)
    pos = z >= 0.0
    sig_z = jnp.where(pos, r, e * r)
    sig_mz = jnp.where(pos, e * r, r)
    kk = (1.0 - lb) * sig_mz
    q = _silu(q_ref[n])
    v = i_ref[n]
    gate = _silu(g_ref[n])
    f = lb + (1.0 - lb) * sig_z
    b = _cumsum_rows(jnp.log2(f), ltri_ref[...])
    row = lax.broadcasted_iota(jnp.int32, (c, w), 0)

    q_lvls, k_lvls = [], []
    for m in HG_LEVELS:
        half = m // 2
        b4 = b.reshape(c // m, 2, half, w)
        ref_b = b4[:, 0, half - 1:half, :]
        k_first = kk.reshape(c // m, 2, half, w)[:, 0] * jnp.exp2(ref_b - b4[:, 0])
        q_second = q.reshape(c // m, 2, half, w)[:, 1] * jnp.exp2(b4[:, 1] - ref_b)
        zero = jnp.zeros_like(k_first)
        q_lvls.append(jnp.stack([zero, q_second], axis=1).reshape(c, w).astype(BF16))
        k_lvls.append(jnp.stack([k_first, zero], axis=1).reshape(c, w).astype(BF16))

    pos8 = row & (SUBLANES - 1)

    def back1(x):
        x3 = x.reshape(c // SUBLANES, SUBLANES, w)
        return pltpu.roll(x3, 1, 1).reshape(c, w)

    f_in = jnp.where(pos8 >= 1, f, 0.0)
    kd_dl, vr = kk, v
    o_diag = jnp.zeros((c, w), F32)
    for dl in range(SUBLANES):
        if dl > 0:
            kd_dl = f_in * back1(kd_dl)
            vr = back1(vr)
        wb = (q * kd_dl).astype(BF16)
        a = jnp.concatenate([_dot(wb[:, :MXU_DIM], ones_ref[...]), _dot(wb[:, MXU_DIM:], ones_ref[...])], axis=1)
        o_diag = o_diag + a * vr

    tt = lax.broadcasted_iota(jnp.int32, (c, c), 0)
    ss = lax.broadcasted_iota(jnp.int32, (c, c), 1)
    same_blk = {m: (tt // m) == (ss // m) for m in HG_LEVELS}

    b_last = b[c - 1:c, :]
    qe = (q * jnp.exp2(b)).astype(BF16)
    kd = (kk * jnp.exp2(b_last - b)).astype(BF16)
    decay = jnp.exp2(b_last)
    vb = v.astype(BF16)
    for h in range(HG_HEADS):
        hs = slice(h * HG_DK, (h + 1) * HG_DK)
        st = st_ref[n, h]
        attn = jnp.zeros((c, c), F32)
        for m, ql, kl in zip(HG_LEVELS, q_lvls, k_lvls):
            part = _dot_nt(ql[:, hs], kl[:, hs])
            attn = attn + (part if m == c else jnp.where(same_blk[m], part, 0.0))
        o_h =_dot_nt(qe[:, hs], st.astype(BF16)) + _dot(attn.astype(BF16), vb[:, hs]) + o_diag[:, hs]
        st_ref[n, h] = st * decay[:, hs] + _dot_tn(vb[:, hs], kd[:, hs])
        o_ref[n, :, hs] = (_rms(o_h, on_ref[...]) * gate[:, hs]).astype(o_ref.dtype)


def _hgrn(proj, lb, onorm):
    nb, s, _ = proj.shape
    head_id = jnp.arange(MXU_DIM) // HG_DK
    ones_blk = (head_id[:, None] == head_id[None, :]).astype(BF16)
    t_id = jnp.arange(HG_CHUNK)
    ltri = (t_id[:, None] >= t_id[None, :]).astype(BF16)
    ns = HG_SEQS if nb % HG_SEQS == 0 else 1
    spec = lambda cb: pl.BlockSpec((ns, HG_CHUNK, COL_BLK), lambda b, i: (b, i, cb))
    return pl.pallas_call(
        _hgrn_kernel,
        grid=(nb // ns, s // HG_CHUNK),
        in_specs=[spec(3), spec(4), spec(5), spec(6),
                  pl.BlockSpec(lb.shape, lambda b, i: (0, 0)),
                  pl.BlockSpec(onorm.shape, lambda b, i: (0, 0)),
                  pl.BlockSpec(ones_blk.shape, lambda b, i: (0, 0)),
                  pl.BlockSpec(ltri.shape, lambda b, i: (0, 0))],
        out_specs=pl.BlockSpec((ns, HG_CHUNK, HG_WIDTH), lambda b, i: (b, i, 0)),
        out_shape=jax.ShapeDtypeStruct((nb, s, HG_WIDTH), BF16),
        scratch_shapes=[pltpu.VMEM((ns, HG_HEADS, HG_DV, HG_DK), F32)],
        compiler_params=_params("parallel", "arbitrary"),
        name="hgrn",
    )(proj, proj, proj, proj, lb, onorm, ones_blk, ltri)


def _out_proj_kernel(ys_ref, ym_ref, yh_ref, w_ref, x_ref, gain_ref, g_ref, o_ref, slab_ref):
    for t in range(S5_LS):
        for c in range(S5_WIDTH // LANES):
            col = t * S5_WIDTH + c * LANES
            slab_ref[c, pl.ds(t, OUT_TM // S5_LS, stride=S5_LS), :] = ys_ref[0, :, col:col + LANES]
    scale = gain_ref[...] * g_ref[0]
    part = OUT_TM // 2
    for p0 in range(0, OUT_TM, part):
        rows = slice(p0, p0 + part)
        ys = jnp.concatenate([slab_ref[c, rows, :] for c in range(S5_WIDTH // LANES)], axis=1).astype(BF16)
        mixed = (_dot(ys, w_ref[:S5_WIDTH, :])
                 + _dot(ym_ref[0, rows, :], w_ref[S5_WIDTH:S5_WIDTH + MLA_WIDTH, :])
                 + _dot(yh_ref[0, rows, :], w_ref[S5_WIDTH + MLA_WIDTH:, :]))
        inv = lax.rsqrt(jnp.mean(mixed * mixed, axis=-1, keepdims=True) + EPS)
        o_ref[0, rows, :] = x_ref[0, rows, :] + mixed * inv * scale


def _out_proj(y_s5_rows, y_mla, y_hg, w, l, x, gain, g):
    nb, s, d = x.shape
    row = lambda n: pl.BlockSpec((1, OUT_TM, n), lambda b, i: (b, i, 0))
    return pl.pallas_call(
        _out_proj_kernel,
        grid=(nb, s // OUT_TM),
        in_specs=[pl.BlockSpec((1, OUT_TM // S5_LS, S5_LS * S5_WIDTH), lambda b, i: (b, i, 0)),
                  row(MLA_WIDTH), row(HG_WIDTH),
                  pl.BlockSpec((None,) + w.shape[1:], lambda b, i: (l, 0, 0), pipeline_mode=pl.Buffered(1)),
                  row(d),
                  pl.BlockSpec((1, d), lambda b, i: (0, 0)),
                  pl.BlockSpec((1, 1, d), lambda b, i: (b, 0, 0))],
        out_specs=row(d),
        out_shape=jax.ShapeDtypeStruct((nb, s, d), F32),
        scratch_shapes=[pltpu.VMEM((S5_WIDTH // LANES, OUT_TM, LANES), F32)],
        compiler_params=_params("parallel", "parallel"),
        name="out_proj",
    )(y_s5_rows, y_mla, y_hg, w, x, gain, g)


def _ffn_kernel(x_ref, sc_ref, sh_ref, g_ref, pre_ref, post_ref, wu_ref, cp_ref, wd_ref, o_ref,
                h_ref, ug_ref, uv_ref, act_ref, tail_ref):
    i = pl.program_id(1)
    j = pl.program_id(2)

    @pl.when(j == 0)
    def _():
        scale = pre_ref[...] * (1.0 + sc_ref[0])
        shift = sh_ref[0]
        for r0 in range(0, FFN_TM, NORM_ROWS):
            x = x_ref[0, r0:r0 + NORM_ROWS, :]
            inv = lax.rsqrt(jnp.mean(x * x, axis=-1, keepdims=True) + EPS)
            h_ref[r0:r0 + NORM_ROWS, :] = (x * inv * scale + shift).astype(BF16)
        o_ref[...] = jnp.zeros_like(o_ref)

    cw = FFN_TF // FFN_SPLITS
    for t, u_ref in enumerate((ug_ref, uv_ref)):
        u_ref[:HALO, :] = jnp.where(i == 0, 0.0, tail_ref[j, t])
    for q in range(FFN_SPLITS):
        cols = slice(q * cw, (q + 1) * cw)
        u = _dot(h_ref[...], wu_ref[:, 2 * q * cw:2 * (q + 1) * cw])
        for t, u_ref in enumerate((ug_ref, uv_ref)):
            u_ref[HALO:, cols] = u[:, t * cw:(t + 1) * cw]
            tail_ref[j, t, :, cols] = u[FFN_TM - HALO:, t * cw:(t + 1) * cw]

    def conv(u_ref, p0, r0, cols):
        return (cp_ref[p0 + CONV_W:p0 + CONV_W + 1, cols]
                + u_ref[r0 + HALO - 2:r0 + HALO - 2 + FFN_ROWS, cols] * cp_ref[p0:p0 + 1, cols]
                + u_ref[r0 + HALO - 1:r0 + HALO - 1 + FFN_ROWS, cols] * cp_ref[p0 + 1:p0 + 2, cols]
                + u_ref[r0 + HALO:r0 + HALO + FFN_ROWS, cols] * cp_ref[p0 + 2:p0 + 3, cols])

    for q in range(FFN_SPLITS):
        cols = slice(q * cw, (q + 1) * cw)
        for r0 in range(0, FFN_TM, FFN_ROWS):
            gate = conv(ug_ref, 0, r0, cols)
            val = conv(uv_ref, CONV_W + 1, r0, cols)
            act_ref[r0:r0 + FFN_ROWS, cols] = (_gelu_tanh(gate) * val).astype(BF16)
        o_ref[0] += _dot(act_ref[:, cols], wd_ref[cols, :])

    @pl.when(j == pl.num_programs(2) - 1)
    def _():
        scale = post_ref[...] * g_ref[0]
        for r0 in range(0, FFN_TM, NORM_ROWS):
            y = o_ref[0, r0:r0 + NORM_ROWS, :]
            inv = lax.rsqrt(jnp.mean(y * y, axis=-1, keepdims=True) + EPS)
            o_ref[0, r0:r0 + NORM_ROWS, :] = x_ref[0, r0:r0 + NORM_ROWS, :] + y * inv * scale


def _ffn(x, sc, sh, g, pre, post, w_up, conv_p, w_down, l):
    nb, s, d = x.shape
    ff = w_down.shape[1]
    nf = ff // FFN_TF
    vec = pl.BlockSpec((1, 1, d), lambda b, i, j: (b, 0, 0))
    par = pl.BlockSpec((1, d), lambda b, i, j: (0, 0))
    return pl.pallas_call(
        _ffn_kernel,
        grid=(nb, s // FFN_TM, nf),
        in_specs=[
            pl.BlockSpec((1, FFN_TM, d), lambda b, i, j: (b, i, 0)),
            vec, vec, vec, par, par,
            pl.BlockSpec((None, d, 2 * FFN_TF), lambda b, i, j: (l, 0, j)),
            pl.BlockSpec((None, None, 2 * (CONV_W + 1), FFN_TF), lambda b, i, j: (l, j, 0, 0)),
            pl.BlockSpec((None, FFN_TF, d), lambda b, i, j: (l, j, 0)),
        ],
        out_specs=pl.BlockSpec((1, FFN_TM, d), lambda b, i, j: (b, i, 0), pipeline_mode=pl.Buffered(1)),
        out_shape=jax.ShapeDtypeStruct((nb, s, d), F32),
        scratch_shapes=[pltpu.VMEM((FFN_TM, d), BF16),
                        pltpu.VMEM((FFN_TM + HALO, FFN_TF), F32), pltpu.VMEM((FFN_TM + HALO, FFN_TF), F32),
                        pltpu.VMEM((FFN_TM, FFN_TF), BF16), pltpu.VMEM((nf, 2, HALO, FFN_TF), F32)],
        compiler_params=_params("parallel", "arbitrary", "arbitrary"),
        name="ffn",
    )(x, sc, sh, g, pre, post, w_up, conv_p, w_down)


def _pad_cols(a, n):
    return jnp.pad(a, [(0, 0)] * (a.ndim - 1) + [(0, n - a.shape[-1])])


def _cast_w_up_kernel(w_ref, o_ref):
    cw = FFN_TF // FFN_SPLITS
    for h in range(2):
        @pl.when(pl.program_id(2) == h)
        def _():
            for p in range(D_FF_PAD // cw):
                width = max(0, min(cw, D_FF - p * cw))
                dst = (2 * p + h) * cw
                if width > 0:
                    o_ref[0, :, dst:dst + width] = w_ref[0, :, p * cw:p * cw + width].astype(BF16)
                if width < cw:
                    o_ref[0, :, dst + width:dst + cw] = jnp.zeros((o_ref.shape[1], cw - width), BF16)


def _cast_w_up(w_up):
    nl, d, _ = w_up.shape
    return pl.pallas_call(
        _cast_w_up_kernel,
        grid=(nl, d // CAST_TM, 2),
        in_specs=[pl.BlockSpec((1, CAST_TM, D_FF), lambda l, i, h: (l, i, h))],
        out_specs=pl.BlockSpec((1, CAST_TM, 2 * D_FF_PAD), lambda l, i, h: (l, i, 0)),
        out_shape=jax.ShapeDtypeStruct((nl, d, 2 * D_FF_PAD), BF16),
        compiler_params=_params("parallel", "parallel", "arbitrary"),
        name="cast_w_up",
    )(w_up)


def _cast_w_down_kernel(w_ref, o_ref):
    o_ref[0, :D_FF, :] = w_ref[0].astype(BF16)
    o_ref[0, D_FF:, :] = jnp.zeros((D_FF_PAD - D_FF, o_ref.shape[2]), BF16)


def _cast_w_down(w_down):
    nl, ff, d = w_down.shape
    return pl.pallas_call(
        _cast_w_down_kernel,
        grid=(nl, d // MXU_DIM),
        in_specs=[pl.BlockSpec((1, ff, MXU_DIM), lambda l, j: (l, 0, j))],
        out_specs=pl.BlockSpec((1, D_FF_PAD, MXU_DIM), lambda l, j: (l, 0, j)),
        out_shape=jax.ShapeDtypeStruct((nl, D_FF_PAD, d), BF16),
        compiler_params=_params("parallel", "parallel"),
        name="cast_w_down",
    )(w_down)


W_IN_HEAD = S5_WIDTH + MLA_Q_RANK + MLA_KV_RANK + MLA_ROPE


def _cast_w_in_kernel(w_ref, o_ref):
    w = w_ref[0]
    gap = jnp.zeros((w.shape[0], 3 * COL_BLK - W_IN_HEAD), w.dtype)
    o_ref[0] = jnp.concatenate([w[:, :W_IN_HEAD], gap, w[:, W_IN_HEAD:]], axis=1).astype(BF16)


def _layout_w_in(w_in):
    nl, d, n = w_in.shape
    return pl.pallas_call(
        _cast_w_in_kernel,
        grid=(nl, d // CAST_TM),
        in_specs=[pl.BlockSpec((1, CAST_TM, n), lambda l, i: (l, i, 0))],
        out_specs=pl.BlockSpec((1, CAST_TM, D_IN_PAD), lambda l, i: (l, i, 0)),
        out_shape=jax.ShapeDtypeStruct((nl, d, D_IN_PAD), BF16),
        compiler_params=_params("parallel", "parallel"),
        name="cast_w_in",
    )(w_in)


def _layout_w_uq(w):
    nl, r, _ = w.shape
    w = w.reshape(nl, r, MLA_HEADS, MLA_NOPE + MLA_ROPE)
    return _pad_cols(w, MLA_QK).reshape(nl, r, MLA_HEADS * MLA_QK).astype(BF16)


def _layout_w_ukv(w):
    nl, r, _ = w.shape
    w = w.reshape(nl, r, MLA_HEADS, MLA_NOPE + MLA_V)
    k = w[..., :MLA_NOPE].reshape(nl, r, MLA_HEADS * MLA_NOPE)
    v = w[..., MLA_NOPE:].reshape(nl, r, MLA_HEADS * MLA_V)
    return jnp.concatenate([k, v], axis=-1).astype(BF16)


def _rope_tables(positions):
    inv_freq = 1.0 / (ROPE_THETA ** (jnp.arange(0, MLA_ROPE, 2, dtype=F32) / MLA_ROPE))
    ang = positions.astype(F32)[..., None] * inv_freq
    cos, sin = jnp.cos(ang), jnp.sin(ang)
    z32 = jnp.zeros_like(cos)
    z64 = jnp.concatenate([z32, z32], axis=-1)
    return (jnp.concatenate([cos, cos, z64], axis=-1),
            jnp.concatenate([z32, sin, z64], axis=-1),
            jnp.concatenate([-sin, z32, z64], axis=-1))


def kernel(x, c, positions, w_in, s5_lambda_re, s5_lambda_im, s5_log_dt, s5_b_re, s5_b_im, s5_c_re, s5_c_im, s5_d, s5_w_glu, mla_q_norm, mla_w_uq, mla_kv_norm, mla_w_ukv, hg_lb_logits, hg_out_norm, w_out, mix_pre_norm, mix_post_norm, ffn_pre_norm, ffn_post_norm, ffn_w_up, ffn_conv_w, ffn_conv_b, ffn_w_down, w_ada, b_ada):
    nb, s, d = x.shape
    nl = w_in.shape[0]
    assert s % FLASH_T == 0 and s % FFN_TM == 0 and s % (S5_LS * SUBLANES) == 0

    mod = _adaln(c, w_ada, b_ada)
    cos_t, s1_t, s2_t = _rope_tables(positions)
    probs = jax.nn.softmax(hg_lb_logits.astype(F32), axis=0)
    lower_bounds = jnp.cumsum(probs, axis=0) - probs[0:1]

    w_in_p = _layout_w_in(w_in)
    w_uq_p = _layout_w_uq(mla_w_uq)
    w_ukv_p = _layout_w_ukv(mla_w_ukv)
    w_out_b = w_out.astype(BF16)
    w_glu_b = s5_w_glu.astype(BF16)
    w_up_p = _cast_w_up(ffn_w_up)
    conv_rows = jnp.concatenate([ffn_conv_w[..., :D_FF], ffn_conv_b[:, None, :D_FF],
                                 ffn_conv_w[..., D_FF:], ffn_conv_b[:, None, D_FF:]], axis=1)
    conv_p = _pad_cols(conv_rows, D_FF_PAD).reshape(nl, 2 * (CONV_W + 1), -1, FFN_TF).transpose(0, 2, 1, 3)
    w_down = _cast_w_down(ffn_w_down)

    for l in range(nl):
        sh1, sc1, g1, sh2, sc2, g2 = [m[:, None, :] for m in jnp.split(mod[l], 6, axis=-1)]
        proj, u_rows = _proj_in(x, sc1, sh1, mix_pre_norm[l][None, :], w_in_p, l)
        bm, cm, lam, lam_chunk = _s5_prep(s5_lambda_re[l], s5_lambda_im[l], s5_log_dt[l],
                                          s5_b_re[l], s5_b_im[l], s5_c_re[l], s5_c_im[l])
        y_s5 = _s5(u_rows, bm, cm, lam, lam_chunk, s5_d[l][None, :], w_glu_b[l])
        q, k, v = _mla_prep(proj, mla_q_norm[l][None, :], mla_kv_norm[l][None, :], w_uq_p[l], w_ukv_p[l],
                            cos_t, s1_t, s2_t)
        y_mla = _flash(q, k, v)
        y_hg = _hgrn(proj, lower_bounds[l][None, :], hg_out_norm[l][None, :])
        x = _out_proj(y_s5, y_mla, y_hg, w_out_b, l, x, mix_post_norm[l][None, :], g1)
        x = _ffn(x, sc2, sh2, g2, ffn_pre_norm[l][None, :], ffn_post_norm[l][None, :],
                 w_up_p, conv_p, w_down, l)
    return x
```

```python
import math

import jax
import jax.numpy as jnp
from jax import lax
from jax.experimental import pallas as pl
from jax.experimental.pallas import tpu as pltpu

F32 = jnp.float32
BF16 = jnp.bfloat16

D_MODEL = 2048
DEPTH = 4
S5_WIDTH = 512
S5_GROUP = 16
S5_GROUPS = 32
S5_STATE = 64
MLA_HEADS = 8
MLA_NOPE = 128
MLA_ROPE = 64
MLA_V = 128
MLA_Q_RANK = 512
MLA_KV_RANK = 256
MLA_WIDTH = MLA_HEADS * MLA_V
ROPE_THETA = 10000.0
MASK_VALUE = -1e30
HG_HEADS = 4
HG_DK = 128
HG_DV = 128
HG_WIDTH = HG_HEADS * HG_DV
D_FF = 5504
CONV_W = 3
EPS = 1e-6

LANES = 128
SUBLANES = 8
MXU_DIM = 256
VMEM_LIMIT_BYTES = 56 * 1024 * 1024

COL_BLK = 512
N_COL_BLKS = 7
D_IN_PAD = N_COL_BLKS * COL_BLK
MLA_QK = 2 * LANES
D_FF_PAD = 5632

ADA_TN = 1024
PROJ_TM = 512
PROJ_PARTS = 2
S5_LS = 16
S5_SEQS = 2
S5_HALF = S5_WIDTH // 2
S5_HSTATE = (S5_GROUPS // 2) * S5_STATE
PREP_TM = 1024
FLASH_T = 512
FLASH_HEADS = 2
FLASH_ROWS = 32
HG_CHUNK = 128
HG_LEVELS = (16, 32, 64, 128)
HG_SEQS = 4
OUT_TM = 512
FFN_TM = 1024
FFN_TF = 512
FFN_ROWS = 64
FFN_SPLITS = 1
NORM_ROWS = 16
CAST_TM = 256
HALO = SUBLANES


def _sigmoid(x):
    return 1.0 / (1.0 + jnp.exp(-x))


def _silu(x):
    return x * _sigmoid(x)


def _gelu_tanh(x):
    c = math.sqrt(2.0 / math.pi)
    return x * (0.5 * (1.0 + jnp.tanh(c * (x + 0.044715 * (x * x * x)))))


def _rms(x, gain):
    return x * lax.rsqrt(jnp.mean(x * x, axis=-1, keepdims=True) + EPS) * gain


def _dot(a, b):
    return jnp.dot(a, b, preferred_element_type=F32)


def _dot_nt(a, b):
    return lax.dot_general(a, b, (((1,), (1,)), ((), ())), preferred_element_type=F32)


def _dot_tn(a, b):
    return lax.dot_general(a, b, (((0,), (0,)), ((), ())), preferred_element_type=F32)


def _params(*sem):
    return pltpu.CompilerParams(dimension_semantics=sem, vmem_limit_bytes=VMEM_LIMIT_BYTES)


def _adaln_kernel(c_ref, w_ref, b_ref, o_ref):
    ca = _silu(c_ref[...]).astype(BF16)
    o_ref[0] = _dot(ca, w_ref[0].astype(BF16)) + b_ref[0]


def _adaln(c, w_ada, b_ada):
    nb, d = c.shape
    nl, _, n = w_ada.shape
    rows = -(-nb // SUBLANES) * SUBLANES
    c_pad = jnp.pad(c, ((0, rows - nb), (0, 0)))
    out = pl.pallas_call(
        _adaln_kernel,
        grid=(nl, n // ADA_TN),
        in_specs=[
            pl.BlockSpec((rows, d), lambda l, j: (0, 0)),
            pl.BlockSpec((1, d, ADA_TN), lambda l, j: (l, 0, j)),
            pl.BlockSpec((1, 1, ADA_TN), lambda l, j: (l, 0, j)),
        ],
        out_specs=pl.BlockSpec((1, rows, ADA_TN), lambda l, j: (l, 0, j)),
        out_shape=jax.ShapeDtypeStruct((nl, rows, n), F32),
        compiler_params=_params("parallel", "parallel"),
        name="adaln",
    )(c_pad, w_ada, b_ada.reshape(nl, 1, n))
    return out[:, :nb]


def _proj_in_kernel(x_ref, sc_ref, sh_ref, gain_ref, w_ref, o_ref, u_ref, h_ref, slab_ref):
    scale = gain_ref[...] * (1.0 + sc_ref[0])
    shift = sh_ref[0]
    part = PROJ_TM // PROJ_PARTS
    for p0 in range(0, PROJ_TM, part):
        for r0 in range(p0, p0 + part, NORM_ROWS):
            x = x_ref[0, r0:r0 + NORM_ROWS, :]
            inv = lax.rsqrt(jnp.mean(x * x, axis=-1, keepdims=True) + EPS)
            h_ref[r0:r0 + NORM_ROWS, :] = (x * inv * scale + shift).astype(BF16)
        res = _dot(h_ref[p0:p0 + part, :], w_ref[...])
        o_ref[0, p0:p0 + part, :] = res
        for c in range(S5_WIDTH // LANES):
            slab_ref[c, p0:p0 + part, :] = res[:, c * LANES:(c + 1) * LANES]

    for t in range(S5_LS):
        for c in range(S5_WIDTH // LANES):
            col = t * S5_WIDTH + c * LANES
            u_ref[0, :, col:col + LANES] = slab_ref[c, pl.ds(t, PROJ_TM // S5_LS, stride=S5_LS), :]


def _proj_in(x, sc, sh, gain, w, l):
    nb, s, d = x.shape
    n = w.shape[2]
    rows = PROJ_TM // S5_LS
    return pl.pallas_call(
        _proj_in_kernel,
        grid=(nb, s // PROJ_TM),
        in_specs=[
            pl.BlockSpec((1, PROJ_TM, d), lambda b, i: (b, i, 0)),
            pl.BlockSpec((1, 1, d), lambda b, i: (b, 0, 0)),
            pl.BlockSpec((1, 1, d), lambda b, i: (b, 0, 0)),
            pl.BlockSpec((1, d), lambda b, i: (0, 0)),
            pl.BlockSpec((None, d, n), lambda b, i: (l, 0, 0), pipeline_mode=pl.Buffered(1)),
        ],
        out_specs=[pl.BlockSpec((1, PROJ_TM, n), lambda b, i: (b, i, 0)),
                   pl.BlockSpec((1, rows, S5_LS * S5_WIDTH), lambda b, i: (b, i, 0))],
        out_shape=[jax.ShapeDtypeStruct((nb, s, n), F32),
                   jax.ShapeDtypeStruct((nb, s // S5_LS, S5_LS * S5_WIDTH), F32)],
        scratch_shapes=[pltpu.VMEM((PROJ_TM, d), BF16), pltpu.VMEM((S5_WIDTH // LANES, PROJ_TM, LANES), F32)],
        compiler_params=_params("parallel", "parallel"),
        name="proj_in",
    )(x, sc, sh, gain, w)


def _s5_kernel(u_ref, bm_ref, cm_ref, lam_ref, lam_chunk_ref, d_ref, wglu_ref, o_ref,
               sre, sim, hre, him, yloc):
    ph = pl.program_id(1)
    s = pl.program_id(2)
    nseq, kc, _ = u_ref.shape
    rows = nseq * kc

    @pl.when(ph == 0)
    def _():
        @pl.when(s == 0)
        def _():
            sre[...] = jnp.zeros_like(sre)
            sim[...] = jnp.zeros_like(sim)

        u = u_ref[...].reshape(rows, S5_WIDTH)
        ub = u.astype(BF16)
        for h in range(2):
            cols = slice(h * S5_HALF, (h + 1) * S5_HALF)
            v = _dot(ub[:, cols], bm_ref[h])
            lr = lam_ref[h:h + 1, :]
            li = lam_ref[2 + h:3 + h, :]
            re = sre[h]
            im = sim[h]
            nre = lr * re - li * im + v[:, :S5_HSTATE]
            nim = lr * im + li * re + v[:, S5_HSTATE:]
            sre[h] = nre
            sim[h] = nim
            yl = _dot(nre.astype(BF16), cm_ref[h, 0]) + _dot(nim.astype(BF16), cm_ref[h, 1])
            yloc[s, :, cols] = yl + d_ref[:, cols] * u[:, cols]

        @pl.when(s == S5_LS - 1)
        def _():
            def body(k, carry):
                new = []
                for n in range(nseq):
                    row = pl.ds(n * kc + k, 1)
                    for h in range(2):
                        cre, cim = carry[4 * n + 2 * h], carry[4 * n + 2 * h + 1]
                        hre[h, row, :] = cre
                        him[h, row, :] = cim
                        lr = lam_chunk_ref[h:h + 1, :]
                        li = lam_chunk_ref[2 + h:3 + h, :]
                        new.append(lr * cre - li * cim + sre[h, row, :])
                        new.append(lr * cim + li * cre + sim[h, row, :])
                return tuple(new)

            zero = jnp.zeros((1, S5_HSTATE), F32)
            lax.fori_loop(0, kc, body, (zero,) * (4 * nseq))

    @pl.when(ph == 1)
    def _():
        ys = []
        for h in range(2):
            cols = slice(h * S5_HALF, (h + 1) * S5_HALF)
            lr = lam_ref[h:h + 1, :]
            li = lam_ref[2 + h:3 + h, :]
            pre = hre[h]
            pim = him[h]
            nre = lr * pre - li * pim
            nim = lr * pim + li * pre
            hre[h] = nre
            him[h] = nim
            yc = _dot(nre.astype(BF16), cm_ref[h, 0]) + _dot(nim.astype(BF16), cm_ref[h, 1])
            ys.append(yloc[s, :, cols] + yc)
        g = _gelu_tanh(jnp.concatenate(ys, axis=1))
        gate = _sigmoid(_dot(g.astype(BF16), wglu_ref[...]))
        o_ref[...] = (g * gate).reshape(o_ref.shape).astype(o_ref.dtype)


def _s5(u_rows, bm, cm, lam, lam_chunk, d, wglu):
    nb, kc, _ = u_rows.shape
    ns = S5_SEQS if nb % S5_SEQS == 0 else 1
    rows = ns * kc
    return pl.pallas_call(
        _s5_kernel,
        grid=(nb // ns, 2, S5_LS),
        in_specs=[
            pl.BlockSpec((ns, kc, S5_WIDTH),
                         lambda b, p, t: (b, 0, jnp.where(p == 0, t, S5_LS - 1))),
            pl.BlockSpec(bm.shape, lambda b, p, t: (0, 0, 0)),
            pl.BlockSpec(cm.shape, lambda b, p, t: (0, 0, 0, 0)),
            pl.BlockSpec(lam.shape, lambda b, p, t: (0, 0)),
            pl.BlockSpec(lam_chunk.shape, lambda b, p, t: (0, 0)),
            pl.BlockSpec(d.shape, lambda b, p, t: (0, 0)),
            pl.BlockSpec(wglu.shape, lambda b, p, t: (0, 0)),
        ],
        out_specs=pl.BlockSpec((ns, kc, S5_WIDTH), lambda b, p, t: (b, 0, t * p)),
        out_shape=jax.ShapeDtypeStruct((nb, kc, S5_LS * S5_WIDTH), F32),
        scratch_shapes=[pltpu.VMEM((2, rows, S5_HSTATE), F32)] * 4
        + [pltpu.VMEM((S5_LS, rows, S5_WIDTH), F32)],
        compiler_params=_params("parallel", "arbitrary", "arbitrary"),
        name="s5",
    )(u_rows, bm, cm, lam, lam_chunk, d, wglu)


def _s5_prep(lam_re, lam_im, log_dt, b_re, b_im, c_re, c_im):
    lam_re, lam_im = lam_re.astype(F32), lam_im.astype(F32)
    dt = jnp.exp(log_dt.astype(F32))[:, None]
    mag = jnp.exp(lam_re * dt)
    bar_re, bar_im = mag * jnp.cos(lam_im * dt), mag * jnp.sin(lam_im * dt)
    den = lam_re * lam_re + lam_im * lam_im
    co_re = ((bar_re - 1.0) * lam_re + bar_im * lam_im) / den
    co_im = (bar_im * lam_re - (bar_re - 1.0) * lam_im) / den
    b_re, b_im = b_re.astype(F32), b_im.astype(F32)
    bb_re = co_re[..., None] * b_re - co_im[..., None] * b_im
    bb_im = co_re[..., None] * b_im + co_im[..., None] * b_re
    ch_re, ch_im = bar_re, bar_im
    for _ in range(int(math.log2(S5_LS))):
        ch_re, ch_im = ch_re * ch_re - ch_im * ch_im, 2.0 * ch_re * ch_im
    gh = S5_GROUPS // 2
    eye = jnp.eye(gh, dtype=F32)

    def b_tiles(part):
        t = part.reshape(2, gh, S5_STATE, S5_GROUP)
        return jnp.einsum('ab,hapc->hacbp', eye, t).reshape(2, S5_HALF, S5_HSTATE)

    def c_tiles(part):
        t = part.reshape(2, gh, S5_GROUP, S5_STATE)
        return jnp.einsum('ab,hbcp->hapbc', eye, t).reshape(2, S5_HSTATE, S5_HALF)

    bm = jnp.concatenate([b_tiles(bb_re), b_tiles(bb_im)], axis=-1).astype(BF16)
    cm = jnp.stack([c_tiles(c_re.astype(F32)), -c_tiles(c_im.astype(F32))], axis=1).astype(BF16)

    def rows(re, im):
        return jnp.concatenate([re.reshape(2, S5_HSTATE), im.reshape(2, S5_HSTATE)], 0)

    return bm, cm, rows(bar_re, bar_im), rows(ch_re, ch_im)


def _mla_prep_kernel(cq_ref, ckv_ref, qn_ref, kvn_ref, wq_ref, wkv_ref, cos_ref, s1_ref, s2_ref,
                     q_ref, k_ref, v_ref):
    cos_t = cos_ref[0]
    s1 = s1_ref[0]
    s2 = s2_ref[0]
    half = MLA_ROPE // 2

    def rope(x):
        return x * cos_t + pltpu.roll(x, half, 1) * s1 + pltpu.roll(x, LANES - half, 1) * s2

    scale = (MLA_NOPE + MLA_ROPE) ** -0.5 * math.log2(math.e)
    q = _dot(_rms(cq_ref[0], qn_ref[...]).astype(BF16), wq_ref[...])
    blk = ckv_ref[0]
    kv = _dot(_rms(blk[:, :MLA_KV_RANK], kvn_ref[...]).astype(BF16), wkv_ref[...])
    kr = rope(blk[:, MLA_KV_RANK:MLA_KV_RANK + LANES]).astype(BF16)
    for h in range(MLA_HEADS):
        qn = q[:, h * MLA_QK:h * MLA_QK + MLA_NOPE] * scale
        qr = rope(q[:, h * MLA_QK + MLA_NOPE:(h + 1) * MLA_QK]) * scale
        q_ref[0, h] = jnp.concatenate([qn, qr], axis=1).astype(BF16)
        k_ref[0, h] = jnp.concatenate([kv[:, h * MLA_NOPE:(h + 1) * MLA_NOPE].astype(BF16), kr], axis=1)
        vh = kv[:, MLA_WIDTH + h * MLA_V:MLA_WIDTH + (h + 1) * MLA_V]
        v_ref[0, h] = jnp.concatenate([vh, jnp.ones_like(vh)], axis=1).astype(BF16)


def _mla_prep(proj, qn, kvn, wq, wkv, cos_t, s1_t, s2_t):
    nb, s, _ = proj.shape
    tab = pl.BlockSpec((1, PREP_TM, LANES), lambda b, i: (b, i, 0))
    return pl.pallas_call(
        _mla_prep_kernel,
        grid=(nb, s // PREP_TM),
        in_specs=[
            pl.BlockSpec((1, PREP_TM, COL_BLK), lambda b, i: (b, i, 1)),
            pl.BlockSpec((1, PREP_TM, COL_BLK), lambda b, i: (b, i, 2)),
            pl.BlockSpec(qn.shape, lambda b, i: (0, 0)),
            pl.BlockSpec(kvn.shape, lambda b, i: (0, 0)),
            pl.BlockSpec(wq.shape, lambda b, i: (0, 0)),
            pl.BlockSpec(wkv.shape, lambda b, i: (0, 0)),
            tab, tab, tab,
        ],
        out_specs=[
            pl.BlockSpec((1, MLA_HEADS, PREP_TM, MLA_QK), lambda b, i: (b, 0, i, 0)),
            pl.BlockSpec((1, MLA_HEADS, PREP_TM, MLA_QK), lambda b, i: (b, 0, i, 0)),
            pl.BlockSpec((1, MLA_HEADS, PREP_TM, 2 * MLA_V), lambda b, i: (b, 0, i, 0)),
        ],
        out_shape=[
            jax.ShapeDtypeStruct((nb, MLA_HEADS, s, MLA_QK), BF16),
            jax.ShapeDtypeStruct((nb, MLA_HEADS, s, MLA_QK), BF16),
            jax.ShapeDtypeStruct((nb, MLA_HEADS, s, 2 * MLA_V), BF16),
        ],
        compiler_params=_params("parallel", "parallel"),
        name="mla_prep",
    )(proj, proj, qn, kvn, wq, wkv, cos_t, s1_t, s2_t)


def _flash_kernel(q_ref, k_ref, v_ref, o_ref, s_ref, p_ref, m_ref, a_ref, acc_ref):
    qi = pl.program_id(2)
    t = FLASH_T
    heads = range(FLASH_HEADS)
    nh = FLASH_HEADS
    m_ref[...] = jnp.full(m_ref.shape, -jnp.inf, F32)
    acc_ref[...] = jnp.zeros_like(acc_ref)
    p_ref[nh:] = jnp.zeros((nh,) + p_ref.shape[1:], BF16)
    a_ref[nh:] = jnp.ones((nh,) + a_ref.shape[1:], F32)

    def scores(j, slot):
        start = pl.multiple_of(j * t, t)
        for h in heads:
            s_ref[slot * nh + h] = _dot_nt(q_ref[0, h], k_ref[0, h, pl.ds(start, t), :])

    def softmax(slot, diagonal):
        for h in heads:
            for r0 in range(0, t, FLASH_ROWS):
                rows = slice(r0, r0 + FLASH_ROWS)
                sc = s_ref[slot * nh + h, rows, :]
                if diagonal:
                    row = lax.broadcasted_iota(jnp.int32, sc.shape, 0) + r0
                    col = lax.broadcasted_iota(jnp.int32, sc.shape, 1)
                    sc = jnp.where(col <= row, sc, MASK_VALUE)
                m_old = m_ref[h, rows, :]
                m_new = jnp.maximum(m_old, jnp.max(sc, axis=-1, keepdims=True))
                p_ref[slot * nh + h, rows, :] = jnp.exp2(sc - m_new[:, :1]).astype(BF16)
                a_ref[slot * nh + h, rows, :] = jnp.exp2(m_old - m_new)
                m_ref[h, rows, :] = m_new

    def accumulate(j, slot):
        start = pl.multiple_of(j * t, t)
        for h in heads:
            pv = _dot(p_ref[slot * nh + h], v_ref[0, h, pl.ds(start, t), :])
            alpha = a_ref[slot * nh + h]
            acc_ref[h] = jnp.concatenate([alpha, alpha], axis=1) * acc_ref[h] + pv

    def stage(j, slot, with_accumulate=True):
        softmax(slot, False)
        scores(j + 1, 1 - slot)
        if with_accumulate:
            accumulate(jnp.maximum(j - 1, 0), 1 - slot)

    odd = lax.rem(qi, 2)
    for slot in range(2):
        @pl.when(odd == slot)
        def _():
            scores(0, slot)

    @pl.when(odd == 1)
    def _():
        stage(0, 1, with_accumulate=False)

    def body(jj, carry):
        j = 2 * jj + odd
        stage(j, 0)
        stage(j + 1, 1)
        return carry

    lax.fori_loop(0, qi // 2, body, 0)
    softmax(0, True)
    accumulate(jnp.maximum(qi - 1, 0), 1)
    accumulate(qi, 0)

    for h in heads:
        acc = acc_ref[h]
        o_ref[0, :, h * MLA_V:(h + 1) * MLA_V] = (acc[:, :MLA_V] / acc[:, MLA_V:]).astype(o_ref.dtype)


def _flash(q, k, v):
    nb, nh, s, _ = q.shape
    hb = FLASH_HEADS
    return pl.pallas_call(
        _flash_kernel,
        grid=(nb, nh // hb, s // FLASH_T),
        in_specs=[
            pl.BlockSpec((1, hb, FLASH_T, MLA_QK), lambda b, h, i: (b, h, i, 0)),
            pl.BlockSpec((1, hb, s, MLA_QK), lambda b, h, i: (b, h, 0, 0)),
            pl.BlockSpec((1, hb, s, 2 * MLA_V), lambda b, h, i: (b, h, 0, 0)),
        ],
        out_specs=pl.BlockSpec((1, FLASH_T, hb * MLA_V), lambda b, h, i: (b, i, h)),
        out_shape=jax.ShapeDtypeStruct((nb, s, nh * MLA_V), BF16),
        scratch_shapes=[pltpu.VMEM((2 * hb, FLASH_T, FLASH_T), F32), pltpu.VMEM((2 * hb, FLASH_T, FLASH_T), BF16),
                        pltpu.VMEM((hb, FLASH_T, LANES), F32), pltpu.VMEM((2 * hb, FLASH_T, LANES), F32),
                        pltpu.VMEM((hb, FLASH_T, 2 * MLA_V), F32)],
        compiler_params=_params("parallel", "parallel", "arbitrary"),
        name="flash",
    )(q, k, v)


def _cumsum_rows(x, ltri):
    hi = x.astype(BF16)
    r1 = x - hi.astype(F32)
    mid = r1.astype(BF16)
    lo = (r1 - mid.astype(F32)).astype(BF16)
    return _dot(ltri, hi) + _dot(ltri, mid) + _dot(ltri, lo)


def _hgrn_kernel(q_ref, f_ref, i_ref, g_ref, lb_ref, on_ref, ones_ref, ltri_ref, o_ref, st_ref):
    @pl.when(pl.program_id(1) == 0)
    def _():
        st_ref[...] = jnp.zeros_like(st_ref)

    for n in range(q_ref.shape[0]):
        _hgrn_chunk(n, q_ref, f_ref, i_ref, g_ref, lb_ref, on_ref, ones_ref, ltri_ref, o_ref, st_ref)


def _hgrn_chunk(n, q_ref, f_ref, i_ref, g_ref, lb_ref, on_ref, ones_ref, ltri_ref, o_ref, st_ref):
    c, w = HG_CHUNK, HG_WIDTH
    z = f_ref[n]
    lb = lb_ref[...]
    e = jnp.exp(-jnp.abs(z))
    r = 1.0 / (1.0 + e)
    pos = z >= 0.0
    sig_z = jnp.where(pos, r, e * r)
    sig_mz = jnp.where(pos, e * r, r)
    kk = (1.0 - lb) * sig_mz
    q = _silu(q_ref[n])
    v = i_ref[n]
    gate = _silu(g_ref[n])
    f = lb + (1.0 - lb) * sig_z
    b = _cumsum_rows(jnp.log2(f), ltri_ref[...])
    row = lax.broadcasted_iota(jnp.int32, (c, w), 0)

    q_lvls, k_lvls = [], []
    for m in HG_LEVELS:
        half = m // 2
        b4 = b.reshape(c // m, 2, half, w)
        ref_b = b4[:, 0, half - 1:half, :]
        k_first = kk.reshape(c // m, 2, half, w)[:, 0] * jnp.exp2(ref_b - b4[:, 0])
        q_second = q.reshape(c // m, 2, half, w)[:, 1] * jnp.exp2(b4[:, 1] - ref_b)
        zero = jnp.zeros_like(k_first)
        q_lvls.append(jnp.stack([zero, q_second], axis=1).reshape(c, w).astype(BF16))
        k_lvls.append(jnp.stack([k_first, zero], axis=1).reshape(c, w).astype(BF16))

    pos8 = row & (SUBLANES - 1)

    def back1(x):
        x3 = x.reshape(c // SUBLANES, SUBLANES, w)
        return pltpu.roll(x3, 1, 1).reshape(c, w)

    f_in = jnp.where(pos8 >= 1, f, 0.0)
    kd_dl, vr = kk, v
    o_diag = jnp.zeros((c, w), F32)
    for dl in range(SUBLANES):
        if dl > 0:
            kd_dl = f_in * back1(kd_dl)
            vr = back1(vr)
        wb = (q * kd_dl).astype(BF16)
        a = jnp.concatenate([_dot(wb[:, :MXU_DIM], ones_ref[...]), _dot(wb[:, MXU_DIM:], ones_ref[...])], axis=1)
        o_diag = o_diag + a * vr

    tt = lax.broadcasted_iota(jnp.int32, (c, c), 0)
    ss = lax.broadcasted_iota(jnp.int32, (c, c), 1)
    same_blk = {m: (tt // m) == (ss // m) for m in HG_LEVELS}

    b_last = b[c - 1:c, :]
    qe = (q * jnp.exp2(b)).astype(BF16)
    kd = (kk * jnp.exp2(b_last - b)).astype(BF16)
    decay = jnp.exp2(b_last)
    vb = v.astype(BF16)
    for h in range(HG_HEADS):
        hs = slice(h * HG_DK, (h + 1) * HG_DK)
        st = st_ref[n, h]
        attn = jnp.zeros((c, c), F32)
        for m, ql, kl in zip(HG_LEVELS, q_lvls, k_lvls):
            part = _dot_nt(ql[:, hs], kl[:, hs])
            attn = attn + (part if m == c else jnp.where(same_blk[m], part, 0.0))
        o_h =_dot_nt(qe[:, hs], st.astype(BF16)) + _dot(attn.astype(BF16), vb[:, hs]) + o_diag[:, hs]
        st_ref[n, h] = st * decay[:, hs] + _dot_tn(vb[:, hs], kd[:, hs])
        o_ref[n, :, hs] = (_rms(o_h, on_ref[...]) * gate[:, hs]).astype(o_ref.dtype)


def _hgrn(proj, lb, onorm):
    nb, s, _ = proj.shape
    head_id = jnp.arange(MXU_DIM) // HG_DK
    ones_blk = (head_id[:, None] == head_id[None, :]).astype(BF16)
    t_id = jnp.arange(HG_CHUNK)
    ltri = (t_id[:, None] >= t_id[None, :]).astype(BF16)
    ns = HG_SEQS if nb % HG_SEQS == 0 else 1
    spec = lambda cb: pl.BlockSpec((ns, HG_CHUNK, COL_BLK), lambda b, i: (b, i, cb))
    return pl.pallas_call(
        _hgrn_kernel,
        grid=(nb // ns, s // HG_CHUNK),
        in_specs=[spec(3), spec(4), spec(5), spec(6),
                  pl.BlockSpec(lb.shape, lambda b, i: (0, 0)),
                  pl.BlockSpec(onorm.shape, lambda b, i: (0, 0)),
                  pl.BlockSpec(ones_blk.shape, lambda b, i: (0, 0)),
                  pl.BlockSpec(ltri.shape, lambda b, i: (0, 0))],
        out_specs=pl.BlockSpec((ns, HG_CHUNK, HG_WIDTH), lambda b, i: (b, i, 0)),
        out_shape=jax.ShapeDtypeStruct((nb, s, HG_WIDTH), BF16),
        scratch_shapes=[pltpu.VMEM((ns, HG_HEADS, HG_DV, HG_DK), F32)],
        compiler_params=_params("parallel", "arbitrary"),
        name="hgrn",
    )(proj, proj, proj, proj, lb, onorm, ones_blk, ltri)


def _out_proj_kernel(ys_ref, ym_ref, yh_ref, w_ref, x_ref, gain_ref, g_ref, o_ref, slab_ref):
    for t in range(S5_LS):
        for c in range(S5_WIDTH // LANES):
            col = t * S5_WIDTH + c * LANES
            slab_ref[c, pl.ds(t, OUT_TM // S5_LS, stride=S5_LS), :] = ys_ref[0, :, col:col + LANES]
    scale = gain_ref[...] * g_ref[0]
    part = OUT_TM // 2
    for p0 in range(0, OUT_TM, part):
        rows = slice(p0, p0 + part)
        ys = jnp.concatenate([slab_ref[c, rows, :] for c in range(S5_WIDTH // LANES)], axis=1).astype(BF16)
        mixed = (_dot(ys, w_ref[:S5_WIDTH, :])
                 + _dot(ym_ref[0, rows, :], w_ref[S5_WIDTH:S5_WIDTH + MLA_WIDTH, :])
                 + _dot(yh_ref[0, rows, :], w_ref[S5_WIDTH + MLA_WIDTH:, :]))
        inv = lax.rsqrt(jnp.mean(mixed * mixed, axis=-1, keepdims=True) + EPS)
        o_ref[0, rows, :] = x_ref[0, rows, :] + mixed * inv * scale


def _out_proj(y_s5_rows, y_mla, y_hg, w, l, x, gain, g):
    nb, s, d = x.shape
    row = lambda n: pl.BlockSpec((1, OUT_TM, n), lambda b, i: (b, i, 0))
    return pl.pallas_call(
        _out_proj_kernel,
        grid=(nb, s // OUT_TM),
        in_specs=[pl.BlockSpec((1, OUT_TM // S5_LS, S5_LS * S5_WIDTH), lambda b, i: (b, i, 0)),
                  row(MLA_WIDTH), row(HG_WIDTH),
                  pl.BlockSpec((None,) + w.shape[1:], lambda b, i: (l, 0, 0), pipeline_mode=pl.Buffered(1)),
                  row(d),
                  pl.BlockSpec((1, d), lambda b, i: (0, 0)),
                  pl.BlockSpec((1, 1, d), lambda b, i: (b, 0, 0))],
        out_specs=row(d),
        out_shape=jax.ShapeDtypeStruct((nb, s, d), F32),
        scratch_shapes=[pltpu.VMEM((S5_WIDTH // LANES, OUT_TM, LANES), F32)],
        compiler_params=_params("parallel", "parallel"),
        name="out_proj",
    )(y_s5_rows, y_mla, y_hg, w, x, gain, g)


def _ffn_kernel(x_ref, sc_ref, sh_ref, g_ref, pre_ref, post_ref, wu_ref, cp_ref, wd_ref, o_ref,
                h_ref, ug_ref, uv_ref, act_ref, tail_ref):
    i = pl.program_id(1)
    j = pl.program_id(2)

    @pl.when(j == 0)
    def _():
        scale = pre_ref[...] * (1.0 + sc_ref[0])
        shift = sh_ref[0]
        for r0 in range(0, FFN_TM, NORM_ROWS):
            x = x_ref[0, r0:r0 + NORM_ROWS, :]
            inv = lax.rsqrt(jnp.mean(x * x, axis=-1, keepdims=True) + EPS)
            h_ref[r0:r0 + NORM_ROWS, :] = (x * inv * scale + shift).astype(BF16)
        o_ref[...] = jnp.zeros_like(o_ref)

    cw = FFN_TF // FFN_SPLITS
    for t, u_ref in enumerate((ug_ref, uv_ref)):
        u_ref[:HALO, :] = jnp.where(i == 0, 0.0, tail_ref[j, t])
    for q in range(FFN_SPLITS):
        cols = slice(q * cw, (q + 1) * cw)
        u = _dot(h_ref[...], wu_ref[:, 2 * q * cw:2 * (q + 1) * cw])
        for t, u_ref in enumerate((ug_ref, uv_ref)):
            u_ref[HALO:, cols] = u[:, t * cw:(t + 1) * cw]
            tail_ref[j, t, :, cols] = u[FFN_TM - HALO:, t * cw:(t + 1) * cw]

    def conv(u_ref, p0, r0, cols):
        return (cp_ref[p0 + CONV_W:p0 + CONV_W + 1, cols]
                + u_ref[r0 + HALO - 2:r0 + HALO - 2 + FFN_ROWS, cols] * cp_ref[p0:p0 + 1, cols]
                + u_ref[r0 + HALO - 1:r0 + HALO - 1 + FFN_ROWS, cols] * cp_ref[p0 + 1:p0 + 2, cols]
                + u_ref[r0 + HALO:r0 + HALO + FFN_ROWS, cols] * cp_ref[p0 + 2:p0 + 3, cols])

    for q in range(FFN_SPLITS):
        cols = slice(q * cw, (q + 1) * cw)
        for r0 in range(0, FFN_TM, FFN_ROWS):
            gate = conv(ug_ref, 0, r0, cols)
            val = conv(uv_ref, CONV_W + 1, r0, cols)
            act_ref[r0:r0 + FFN_ROWS, cols] = (_gelu_tanh(gate) * val).astype(BF16)
        o_ref[0] += _dot(act_ref[:, cols], wd_ref[cols, :])

    @pl.when(j == pl.num_programs(2) - 1)
    def _():
        scale = post_ref[...] * g_ref[0]
        for r0 in range(0, FFN_TM, NORM_ROWS):
            y = o_ref[0, r0:r0 + NORM_ROWS, :]
            inv = lax.rsqrt(jnp.mean(y * y, axis=-1, keepdims=True) + EPS)
            o_ref[0, r0:r0 + NORM_ROWS, :] = x_ref[0, r0:r0 + NORM_ROWS, :] + y * inv * scale


def _ffn(x, sc, sh, g, pre, post, w_up, conv_p, w_down, l):
    nb, s, d = x.shape
    ff = w_down.shape[1]
    nf = ff // FFN_TF
    vec = pl.BlockSpec((1, 1, d), lambda b, i, j: (b, 0, 0))
    par = pl.BlockSpec((1, d), lambda b, i, j: (0, 0))
    return pl.pallas_call(
        _ffn_kernel,
        grid=(nb, s // FFN_TM, nf),
        in_specs=[
            pl.BlockSpec((1, FFN_TM, d), lambda b, i, j: (b, i, 0)),
            vec, vec, vec, par, par,
            pl.BlockSpec((None, d, 2 * FFN_TF), lambda b, i, j: (l, 0, j)),
            pl.BlockSpec((None, None, 2 * (CONV_W + 1), FFN_TF), lambda b, i, j: (l, j, 0, 0)),
            pl.BlockSpec((None, FFN_TF, d), lambda b, i, j: (l, j, 0)),
        ],
        out_specs=pl.BlockSpec((1, FFN_TM, d), lambda b, i, j: (b, i, 0), pipeline_mode=pl.Buffered(1)),
        out_shape=jax.ShapeDtypeStruct((nb, s, d), F32),
        scratch_shapes=[pltpu.VMEM((FFN_TM, d), BF16),
                        pltpu.VMEM((FFN_TM + HALO, FFN_TF), F32), pltpu.VMEM((FFN_TM + HALO, FFN_TF), F32),
                        pltpu.VMEM((FFN_TM, FFN_TF), BF16), pltpu.VMEM((nf, 2, HALO, FFN_TF), F32)],
        compiler_params=_params("parallel", "arbitrary", "arbitrary"),
        name="ffn",
    )(x, sc, sh, g, pre, post, w_up, conv_p, w_down)


def _pad_cols(a, n):
    return jnp.pad(a, [(0, 0)] * (a.ndim - 1) + [(0, n - a.shape[-1])])


def _cast_w_up_kernel(w_ref, o_ref):
    cw = FFN_TF // FFN_SPLITS
    for h in range(2):
        @pl.when(pl.program_id(2) == h)
        def _():
            for p in range(D_FF_PAD // cw):
                width = max(0, min(cw, D_FF - p * cw))
                dst = (2 * p + h) * cw
                if width > 0:
                    o_ref[0, :, dst:dst + width] = w_ref[0, :, p * cw:p * cw + width].astype(BF16)
                if width < cw:
                    o_ref[0, :, dst + width:dst + cw] = jnp.zeros((o_ref.shape[1], cw - width), BF16)


def _cast_w_up(w_up):
    nl, d, _ = w_up.shape
    return pl.pallas_call(
        _cast_w_up_kernel,
        grid=(nl, d // CAST_TM, 2),
        in_specs=[pl.BlockSpec((1, CAST_TM, D_FF), lambda l, i, h: (l, i, h))],
        out_specs=pl.BlockSpec((1, CAST_TM, 2 * D_FF_PAD), lambda l, i, h: (l, i, 0)),
        out_shape=jax.ShapeDtypeStruct((nl, d, 2 * D_FF_PAD), BF16),
        compiler_params=_params("parallel", "parallel", "arbitrary"),
        name="cast_w_up",
    )(w_up)


def _cast_w_down_kernel(w_ref, o_ref):
    o_ref[0, :D_FF, :] = w_ref[0].astype(BF16)
    o_ref[0, D_FF:, :] = jnp.zeros((D_FF_PAD - D_FF, o_ref.shape[2]), BF16)


def _cast_w_down(w_down):
    nl, ff, d = w_down.shape
    return pl.pallas_call(
        _cast_w_down_kernel,
        grid=(nl, d // MXU_DIM),
        in_specs=[pl.BlockSpec((1, ff, MXU_DIM), lambda l, j: (l, 0, j))],
        out_specs=pl.BlockSpec((1, D_FF_PAD, MXU_DIM), lambda l, j: (l, 0, j)),
        out_shape=jax.ShapeDtypeStruct((nl, D_FF_PAD, d), BF16),
        compiler_params=_params("parallel", "parallel"),
        name="cast_w_down",
    )(w_down)


W_IN_HEAD = S5_WIDTH + MLA_Q_RANK + MLA_KV_RANK + MLA_ROPE


def _cast_w_in_kernel(w_ref, o_ref):
    w = w_ref[0]
    gap = jnp.zeros((w.shape[0], 3 * COL_BLK - W_IN_HEAD), w.dtype)
    o_ref[0] = jnp.concatenate([w[:, :W_IN_HEAD], gap, w[:, W_IN_HEAD:]], axis=1).astype(BF16)


def _layout_w_in(w_in):
    nl, d, n = w_in.shape
    return pl.pallas_call(
        _cast_w_in_kernel,
        grid=(nl, d // CAST_TM),
        in_specs=[pl.BlockSpec((1, CAST_TM, n), lambda l, i: (l, i, 0))],
        out_specs=pl.BlockSpec((1, CAST_TM, D_IN_PAD), lambda l, i: (l, i, 0)),
        out_shape=jax.ShapeDtypeStruct((nl, d, D_IN_PAD), BF16),
        compiler_params=_params("parallel", "parallel"),
        name="cast_w_in",
    )(w_in)


def _layout_w_uq(w):
    nl, r, _ = w.shape
    w = w.reshape(nl, r, MLA_HEADS, MLA_NOPE + MLA_ROPE)
    return _pad_cols(w, MLA_QK).reshape(nl, r, MLA_HEADS * MLA_QK).astype(BF16)


def _layout_w_ukv(w):
    nl, r, _ = w.shape
    w = w.reshape(nl, r, MLA_HEADS, MLA_NOPE + MLA_V)
    k = w[..., :MLA_NOPE].reshape(nl, r, MLA_HEADS * MLA_NOPE)
    v = w[..., MLA_NOPE:].reshape(nl, r, MLA_HEADS * MLA_V)
    return jnp.concatenate([k, v], axis=-1).astype(BF16)


def _rope_tables(positions):
    inv_freq = 1.0 / (ROPE_THETA ** (jnp.arange(0, MLA_ROPE, 2, dtype=F32) / MLA_ROPE))
    ang = positions.astype(F32)[..., None] * inv_freq
    cos, sin = jnp.cos(ang), jnp.sin(ang)
    z32 = jnp.zeros_like(cos)
    z64 = jnp.concatenate([z32, z32], axis=-1)
    return (jnp.concatenate([cos, cos, z64], axis=-1),
            jnp.concatenate([z32, sin, z64], axis=-1),
            jnp.concatenate([-sin, z32, z64], axis=-1))


def kernel(x, c, positions, w_in, s5_lambda_re, s5_lambda_im, s5_log_dt, s5_b_re, s5_b_im, s5_c_re, s5_c_im, s5_d, s5_w_glu, mla_q_norm, mla_w_uq, mla_kv_norm, mla_w_ukv, hg_lb_logits, hg_out_norm, w_out, mix_pre_norm, mix_post_norm, ffn_pre_norm, ffn_post_norm, ffn_w_up, ffn_conv_w, ffn_conv_b, ffn_w_down, w_ada, b_ada):
    nb, s, d = x.shape
    nl = w_in.shape[0]
    assert s % FLASH_T == 0 and s % FFN_TM == 0 and s % (S5_LS * SUBLANES) == 0

    mod = _adaln(c, w_ada, b_ada)
    cos_t, s1_t, s2_t = _rope_tables(positions)
    probs = jax.nn.softmax(hg_lb_logits.astype(F32), axis=0)
    lower_bounds = jnp.cumsum(probs, axis=0) - probs[0:1]

    w_in_p = _layout_w_in(w_in)
    w_uq_p = _layout_w_uq(mla_w_uq)
    w_ukv_p = _layout_w_ukv(mla_w_ukv)
    w_out_b = w_out.astype(BF16)
    w_glu_b = s5_w_glu.astype(BF16)
    w_up_p = _cast_w_up(ffn_w_up)
    conv_rows = jnp.concatenate([ffn_conv_w[..., :D_FF], ffn_conv_b[:, None, :D_FF],
                                 ffn_conv_w[..., D_FF:], ffn_conv_b[:, None, D_FF:]], axis=1)
    conv_p = _pad_cols(conv_rows, D_FF_PAD).reshape(nl, 2 * (CONV_W + 1), -1, FFN_TF).transpose(0, 2, 1, 3)
    w_down = _cast_w_down(ffn_w_down)

    for l in range(nl):
        sh1, sc1, g1, sh2, sc2, g2 = [m[:, None, :] for m in jnp.split(mod[l], 6, axis=-1)]
        proj, u_rows = _proj_in(x, sc1, sh1, mix_pre_norm[l][None, :], w_in_p, l)
        bm, cm, lam, lam_chunk = _s5_prep(s5_lambda_re[l], s5_lambda_im[l], s5_log_dt[l],
                                          s5_b_re[l], s5_b_im[l], s5_c_re[l], s5_c_im[l])
        y_s5 = _s5(u_rows, bm, cm, lam, lam_chunk, s5_d[l][None, :], w_glu_b[l])
        q, k, v = _mla_prep(proj, mla_q_norm[l][None, :], mla_kv_norm[l][None, :], w_uq_p[l], w_ukv_p[l],
                            cos_t, s1_t, s2_t)
        y_mla = _flash(q, k, v)
        y_hg = _hgrn(proj, lower_bounds[l][None, :], hg_out_norm[l][None, :])
        x = _out_proj(y_s5, y_mla, y_hg, w_out_b, l, x, mix_post_norm[l][None, :], g1)
        x = _ffn(x, sc2, sh2, g2, ffn_pre_norm[l][None, :], ffn_post_norm[l][None, :],
                 w_up_p, conv_p, w_down, l)
    return x
```
